```python
import math
import jax, jax.numpy as jnp
from jax import lax
import numpy as np

D_MODEL = 1024
BATCH = 32
SEQ = 256
DEPTH = 2
DEC_BATCH = 4
DEC_SEQ = 4096
PAST_LEN = 256

GRID_W = 64
HEAD_DIM = 64
H_A = 4
H_B = 8
H_C = 8
KV_C = 2
NA_KH = 8
NA_KW = 16
ROPE_THETA = 10000.0
N_EXPERTS = 16
N_GROUPS = 4
EXPERTS_PER_GROUP = N_EXPERTS // N_GROUPS
TOP_K = 2
D_EXPERT = 256
Q_BLOCK = 128
N_BRANCH = 3
BRANCH_W = H_B * HEAD_DIM
LN_EPS = 1e-5
RMS_EPS = 1e-6
IN_SPLITS = (2 * H_A * HEAD_DIM, 2 * H_A * HEAD_DIM, H_A * 2 * HEAD_DIM,
             H_B * HEAD_DIM, H_B * HEAD_DIM, H_B * HEAD_DIM,
             H_C * HEAD_DIM, KV_C * HEAD_DIM, KV_C * HEAD_DIM)
V_PIECES = (2, 5, 8)

kernel_name = 'hybrid_diffusion_parallel_mixer_step'


def _in_offsets():
    offs = []
    acc = 0
    for s in IN_SPLITS[:-1]:
        acc += s
        offs.append(acc)
    return offs


def _rms(x, w):
    xf = x.astype(jnp.float32)
    y = xf * lax.rsqrt(jnp.mean(xf * xf, axis=-1, keepdims=True) + RMS_EPS) * w.astype(jnp.float32)
    return y.astype(x.dtype)


def _layernorm(x, g, b):
    xf = x.astype(jnp.float32)
    mu = jnp.mean(xf, axis=-1, keepdims=True)
    var = jnp.mean(jnp.square(xf - mu), axis=-1, keepdims=True)
    y = (xf - mu) * lax.rsqrt(var + LN_EPS) * g.astype(jnp.float32) + b.astype(jnp.float32)
    return y.astype(x.dtype)


def _rope_2d(t_len):
    t = jnp.arange(t_len)
    row = (t // GRID_W).astype(jnp.float32)
    col = (t % GRID_W).astype(jnp.float32)
    n = HEAD_DIM // 4
    freqs = ROPE_THETA ** (-jnp.arange(n, dtype=jnp.float32) / n)
    ang = jnp.concatenate([row[:, None] * freqs, col[:, None] * freqs], axis=-1)
    return jnp.cos(ang), jnp.sin(ang)


def _apply_rope(x, cos, sin):
    shape = (1, x.shape[1]) + (1,) * (x.ndim - 3) + (HEAD_DIM // 2,)
    c = cos.reshape(shape)
    s = sin.reshape(shape)
    xf = x.astype(jnp.float32).reshape(x.shape[:-1] + (HEAD_DIM // 2, 2))
    x1, x2 = xf[..., 0], xf[..., 1]
    out = jnp.stack([x1 * c - x2 * s, x1 * s + x2 * c], axis=-1).reshape(x.shape)
    return out.astype(x.dtype)


def _diff_lambda(lam_l, layer):
    lam_init = 0.8 - 0.6 * math.exp(-0.3 * layer)
    lf = lam_l.astype(jnp.float32)
    lam_v = jnp.exp(jnp.sum(lf[0] * lf[1])) - jnp.exp(jnp.sum(lf[2] * lf[3])) + lam_init
    return lam_v, lam_init


def _diff_attend(q, k, v, lam_v):
    s = jnp.einsum('bqmhd,bkmhd->bmhqk', q, k, preferred_element_type=jnp.float32) * (HEAD_DIM ** -0.5)
    p = jax.nn.softmax(s, axis=-1)
    a = p[:, 0] - lam_v * p[:, 1]
    return jnp.einsum('bhqk,bkhe->bqhe', a.astype(v.dtype), v)


def _diff_out(o, subln_w, lam_init):
    b, t = o.shape[0], o.shape[1]
    return (_rms(o, subln_w) * (1.0 - lam_init)).reshape(b, t, BRANCH_W)


def _mha_attend(q, k, v):
    s = jnp.einsum('bqhd,bkhd->bhqk', q, k, preferred_element_type=jnp.float32) * (HEAD_DIM ** -0.5)
    p = jax.nn.softmax(s, axis=-1)
    return jnp.einsum('bhqk,bkhd->bqhd', p.astype(v.dtype), v)


def _gqa_attend(q, k, v):
    s = jnp.einsum('bqngd,bknd->bngqk', q, k, preferred_element_type=jnp.float32) * (HEAD_DIM ** -0.5)
    p = jax.nn.softmax(s, axis=-1)
    return jnp.einsum('bngqk,bknd->bqngd', p.astype(v.dtype), v)


def _blockwise(fn, q):
    b, t = q.shape[0], q.shape[1]
    nb = t // Q_BLOCK
    qb = jnp.moveaxis(q.reshape((b, nb, Q_BLOCK) + q.shape[2:]), 1, 0)
    o = jnp.moveaxis(lax.map(fn, qb), 0, 1)
    return o.reshape((b, t) + o.shape[3:])


def _neighbourhood_attend(q, k_lat, v_lat, k_ctx, v_ctx, rpb_l):
    b, t, h, d = q.shape
    rows = t // GRID_W
    kh = min(NA_KH, rows)
    r = jnp.arange(rows)
    row_start = jnp.clip(r - kh // 2, 0, rows - kh)
    row_keys = row_start[:, None] + jnp.arange(kh)[None, :]
    dy = row_keys - r[:, None]
    cidx = jnp.arange(GRID_W)
    col_start = jnp.clip(cidx - NA_KW // 2, 0, GRID_W - NA_KW)
    col_keys = col_start[:, None] + jnp.arange(NA_KW)[None, :]
    dx = col_keys - cidx[:, None]
    n_loc = kh * NA_KW
    scale = HEAD_DIM ** -0.5
    q_rows = jnp.moveaxis(q.reshape(b, rows, GRID_W, h, d), 1, 0)

    def row_body(args):
        q_r, rk, dy_r = args
        idx = (rk[None, :, None] * GRID_W + col_keys[:, None, :]).reshape(GRID_W, n_loc)
        k_sel = k_lat[:, idx]
        v_sel = v_lat[:, idx]
        bias = rpb_l[:, dy_r[None, :, None] + (NA_KH - 1), dx[:, None, :] + (NA_KW - 1)]
        bias = bias.reshape(h, GRID_W, n_loc).astype(jnp.float32)
        s_loc = jnp.einsum('bqhd,bqkhd->bhqk', q_r, k_sel, preferred_element_type=jnp.float32) * scale + bias[None]
        s_ctx = jnp.einsum('bqhd,bkhd->bhqk', q_r, k_ctx, preferred_element_type=jnp.float32) * scale
        p = jax.nn.softmax(jnp.concatenate([s_loc, s_ctx], axis=-1), axis=-1).astype(v_lat.dtype)
        return (jnp.einsum('bhqk,bqkhd->bqhd', p[..., :n_loc], v_sel)
                + jnp.einsum('bhqk,bkhd->bqhd', p[..., n_loc:], v_ctx))

    o = lax.map(row_body, (q_rows, row_keys, dy))
    return jnp.moveaxis(o, 0, 1).reshape(b, t, h, d)


def _moe(h, w_router, router_bias, w1, w3, w2):
    shp = h.shape
    xt = h.reshape(-1, D_MODEL)
    scores = jax.nn.sigmoid(jnp.dot(xt, w_router, preferred_element_type=jnp.float32))
    sel = scores + router_bias.astype(jnp.float32)
    gscore = jnp.sum(lax.top_k(sel.reshape(-1, N_GROUPS, EXPERTS_PER_GROUP), TOP_K)[0], axis=-1)
    best = jnp.argmax(gscore, axis=-1)
    gmask = jnp.arange(N_GROUPS)[None, :] == best[:, None]
    emask = jnp.repeat(gmask, EXPERTS_PER_GROUP, axis=-1)
    _, idx = lax.top_k(jnp.where(emask, sel, -jnp.inf), TOP_K)
    wts = jnp.take_along_axis(scores, idx, axis=-1)
    wts = wts / jnp.sum(wts, axis=-1, keepdims=True)
    comb = jnp.sum(jax.nn.one_hot(idx, N_EXPERTS, dtype=jnp.float32) * wts[..., None], axis=-2)
    out = jnp.zeros(xt.shape, jnp.float32)
    for e in range(N_EXPERTS):
        he = jax.nn.silu(xt @ w1[e]) * (xt @ w3[e])
        out = out + comb[:, e:e + 1] * (he @ w2[e]).astype(jnp.float32)
    return out.astype(h.dtype).reshape(shp)


def _open_layer(x, cvec, w_mod_l, b_mod_l, w_in_l, c_qnorm_l, c_knorm_l):
    mod = jax.nn.silu(cvec) @ w_mod_l + b_mod_l
    sh1, sc1, g1, sh2, sc2, g2 = jnp.split(mod, 6, axis=-1)
    h = x * (1.0 + sc1) + sh1
    b, t = x.shape[0], x.shape[1]
    qa, ka, va, qb, kb, vb, qc, kc, vc = jnp.split(h @ w_in_l, _in_offsets(), axis=-1)
    qa = qa.reshape(b, t, 2, H_A, HEAD_DIM)
    ka = ka.reshape(b, t, 2, H_A, HEAD_DIM)
    va = va.reshape(b, t, H_A, 2 * HEAD_DIM)
    qb = qb.reshape(b, t, H_B, HEAD_DIM)
    kb = kb.reshape(b, t, H_B, HEAD_DIM)
    vb = vb.reshape(b, t, H_B, HEAD_DIM)
    qc = _rms(qc.reshape(b, t, KV_C, H_C // KV_C, HEAD_DIM), c_qnorm_l)
    kc = _rms(kc.reshape(b, t, KV_C, HEAD_DIM), c_knorm_l)
    vc = vc.reshape(b, t, KV_C, HEAD_DIM)
    return h, (g1, sh2, sc2, g2), (qa, ka, va, qb, kb, vb, qc, kc, vc)


def _close_layer(x, h, mods, oa, ob, oc, alpha, w_gate_l, b_gate_l, w_br_l, w_o_l, ln_g_l, ln_b_l,
                 w_router, router_bias, moe_w1_l, moe_w3_l, moe_w2_l):
    g1, sh2, sc2, g2 = mods
    ga, gb, gc = jnp.split(jax.nn.sigmoid(h @ w_gate_l + b_gate_l), N_BRANCH, axis=-1)
    y = (ga * (oa @ w_br_l[0]) + gb * (ob @ w_br_l[1]) + gc * (oc @ w_br_l[2])) @ w_o_l
    x = _layernorm(alpha * x + g1 * y, ln_g_l[0], ln_b_l[0])
    h2 = x * (1.0 + sc2) + sh2
    m = _moe(h2, w_router, router_bias, moe_w1_l, moe_w3_l, moe_w2_l)
    return _layernorm(alpha * x + g2 * m, ln_g_l[1], ln_b_l[1])


def _context_layer(x, c_ctx, layer, alpha, lw, w_router, router_bias):
    (w_mod_l, b_mod_l, w_in_l, w_gate_l, b_gate_l, lam_l, a_subln_l, rpb_l, c_qnorm_l, c_knorm_l,
     w_br_l, w_o_l, ln_g_l, ln_b_l, moe_w1_l, moe_w3_l, moe_w2_l) = lw
    b, t = x.shape[0], x.shape[1]
    h, mods, (qa, ka, va, qb, kb, vb, qc, kc, vc) = _open_layer(
        x, c_ctx[None, None, :], w_mod_l, b_mod_l, w_in_l, c_qnorm_l, c_knorm_l)
    lam_v, lam_init = _diff_lambda(lam_l, layer)
    oa = _diff_out(_diff_attend(qa, ka, va, lam_v), a_subln_l, lam_init)
    ob = _mha_attend(qb, kb, vb).reshape(b, t, BRANCH_W)
    oc = _gqa_attend(qc, kc, vc).reshape(b, t, BRANCH_W)
    x = _close_layer(x, h, mods, oa, ob, oc, alpha, w_gate_l, b_gate_l, w_br_l, w_o_l, ln_g_l, ln_b_l,
                     w_router, router_bias, moe_w1_l, moe_w3_l, moe_w2_l)
    return x, (ka, va, kb, vb, kc, vc)


def _latent_layer(x, c, layer, alpha, ctx_kv, cos, sin, lw, w_router, router_bias):
    (w_mod_l, b_mod_l, w_in_l, w_gate_l, b_gate_l, lam_l, a_subln_l, rpb_l, c_qnorm_l, c_knorm_l,
     w_br_l, w_o_l, ln_g_l, ln_b_l, moe_w1_l, moe_w3_l, moe_w2_l) = lw
    ck_a, cv_a, ck_b, cv_b, ck_c, cv_c = ctx_kv
    b, t = x.shape[0], x.shape[1]
    h, mods, (qa, ka, va, qb, kb, vb, qc, kc, vc) = _open_layer(
        x, c[:, None, :], w_mod_l, b_mod_l, w_in_l, c_qnorm_l, c_knorm_l)
    qa = _apply_rope(qa, cos, sin)
    ka = _apply_rope(ka, cos, sin)
    qc = _apply_rope(qc, cos, sin)
    kc = _apply_rope(kc, cos, sin)
    lam_v, lam_init = _diff_lambda(lam_l, layer)
    k_all_a = jnp.concatenate([ck_a, ka], axis=1)
    v_all_a = jnp.concatenate([cv_a, va], axis=1)
    oa = _blockwise(lambda qq: _diff_attend(qq, k_all_a, v_all_a, lam_v), qa)
    oa = _diff_out(oa, a_subln_l, lam_init)
    ob = _neighbourhood_attend(qb, kb, vb, ck_b, cv_b, rpb_l).reshape(b, t, BRANCH_W)
    k_all_c = jnp.concatenate([ck_c, kc], axis=1)
    v_all_c = jnp.concatenate([cv_c, vc], axis=1)
    oc = _blockwise(lambda qq: _gqa_attend(qq, k_all_c, v_all_c), qc).reshape(b, t, BRANCH_W)
    return _close_layer(x, h, mods, oa, ob, oc, alpha, w_gate_l, b_gate_l, w_br_l, w_o_l, ln_g_l, ln_b_l,
                        w_router, router_bias, moe_w1_l, moe_w3_l, moe_w2_l)


def setup_inputs(seed: int = 0) -> dict:
    key = jax.random.key(seed)
    ks = jax.random.split(key, 32)
    f32 = jnp.float32
    beta = (8.0 * DEPTH) ** -0.25
    dm = D_MODEL ** -0.5

    def nrm(k, shape, s):
        return jax.random.normal(k, shape, f32) * s

    kin = jax.random.split(ks[12], len(IN_SPLITS))
    w_in = jnp.concatenate(
        [nrm(kin[i], (DEPTH, D_MODEL, s), dm * (beta if i in V_PIECES else 1.0)) for i, s in enumerate(IN_SPLITS)],
        axis=-1)
    return {
        'x_prompt': nrm(ks[0], (BATCH, SEQ, D_MODEL), 1.0),
        'x_sample': nrm(ks[1], (DEC_BATCH, DEC_SEQ, D_MODEL), 1.0),
        'c': nrm(ks[2], (DEC_BATCH, D_MODEL), 1.0),
        'cache_a_k': nrm(ks[3], (DEC_BATCH, DEPTH, PAST_LEN, 2, H_A, HEAD_DIM), 1.0),
        'cache_a_v': nrm(ks[4], (DEC_BATCH, DEPTH, PAST_LEN, H_A, 2 * HEAD_DIM), beta),
        'cache_b_k': nrm(ks[5], (DEC_BATCH, DEPTH, PAST_LEN, H_B, HEAD_DIM), 1.0),
        'cache_b_v': nrm(ks[6], (DEC_BATCH, DEPTH, PAST_LEN, H_B, HEAD_DIM), beta),
        'cache_c_k': nrm(ks[7], (DEC_BATCH, DEPTH, PAST_LEN, KV_C, HEAD_DIM), 1.0),
        'cache_c_v': nrm(ks[8], (DEC_BATCH, DEPTH, PAST_LEN, KV_C, HEAD_DIM), beta),
        'c_ctx': nrm(ks[9], (D_MODEL,), 1.0),
        'w_mod': nrm(ks[10], (DEPTH, D_MODEL, 6 * D_MODEL), 0.5 * dm),
        'b_mod': nrm(ks[11], (DEPTH, 6 * D_MODEL), 0.01),
        'w_in': w_in,
        'w_gate': nrm(ks[13], (DEPTH, D_MODEL, N_BRANCH * D_MODEL), dm),
        'b_gate': nrm(ks[14], (DEPTH, N_BRANCH * D_MODEL), 0.01),
        'lam': nrm(ks[15], (DEPTH, 4, HEAD_DIM), 0.1),
        'a_subln': 1.0 + nrm(ks[16], (DEPTH, 2 * HEAD_DIM), 0.01),
        'rpb': nrm(ks[17], (DEPTH, H_B, 2 * NA_KH - 1, 2 * NA_KW - 1), 0.02),
        'c_qnorm': 1.0 + nrm(ks[18], (DEPTH, HEAD_DIM), 0.01),
        'c_knorm': 1.0 + nrm(ks[19], (DEPTH, HEAD_DIM), 0.01),
        'w_br': nrm(ks[20], (DEPTH, N_BRANCH, BRANCH_W, D_MODEL), BRANCH_W ** -0.5 * beta),
        'w_o': nrm(ks[21], (DEPTH, D_MODEL, D_MODEL), dm * beta),
        'ln_g': 1.0 + nrm(ks[22], (DEPTH, 2, D_MODEL), 0.01),
        'ln_b': nrm(ks[23], (DEPTH, 2, D_MODEL), 0.01),
        'w_router': nrm(ks[24], (D_MODEL, N_EXPERTS), dm),
        'router_bias': nrm(ks[25], (N_EXPERTS,), 0.01),
        'moe_w1': nrm(ks[26], (DEPTH, N_EXPERTS, D_MODEL, D_EXPERT), dm),
        'moe_w3': nrm(ks[27], (DEPTH, N_EXPERTS, D_MODEL, D_EXPERT), dm * beta),
        'moe_w2': nrm(ks[28], (DEPTH, N_EXPERTS, D_EXPERT, D_MODEL), D_EXPERT ** -0.5 * beta),
    }


def reference(x_prompt, x_sample, c, cache_a_k, cache_a_v, cache_b_k, cache_b_v, cache_c_k, cache_c_v,
              c_ctx, w_mod, b_mod, w_in, w_gate, b_gate, lam, a_subln, rpb, c_qnorm, c_knorm,
              w_br, w_o, ln_g, ln_b, w_router, router_bias, moe_w1, moe_w3, moe_w2):
    alpha = (2.0 * DEPTH) ** 0.25

    def layer_weights(l):
        return (w_mod[l], b_mod[l], w_in[l], w_gate[l], b_gate[l], lam[l], a_subln[l], rpb[l],
                c_qnorm[l], c_knorm[l], w_br[l], w_o[l], ln_g[l], ln_b[l], moe_w1[l], moe_w3[l], moe_w2[l])

    y_prompt = x_prompt
    ctx = []
    for l in range(DEPTH):
        y_prompt, kv = _context_layer(y_prompt, c_ctx, l, alpha, layer_weights(l), w_router, router_bias)
        ctx.append(kv)
    new_a_k = jnp.stack([kv[0] for kv in ctx], axis=1)
    new_a_v = jnp.stack([kv[1] for kv in ctx], axis=1)
    new_b_k = jnp.stack([kv[2] for kv in ctx], axis=1)
    new_b_v = jnp.stack([kv[3] for kv in ctx], axis=1)
    new_c_k = jnp.stack([kv[4] for kv in ctx], axis=1)
    new_c_v = jnp.stack([kv[5] for kv in ctx], axis=1)

    cos, sin = _rope_2d(x_sample.shape[1])
    y_sample = x_sample
    for l in range(DEPTH):
        ctx_kv = (cache_a_k[:, l], cache_a_v[:, l], cache_b_k[:, l], cache_b_v[:, l],
                  cache_c_k[:, l], cache_c_v[:, l])
        y_sample = _latent_layer(y_sample, c, l, alpha, ctx_kv, cos, sin, layer_weights(l),
                                 w_router, router_bias)

    return (y_prompt, y_sample, new_a_k, new_a_v, new_b_k, new_b_v, new_c_k, new_c_v)
```

```python
import functools
import math
import operator

import jax
import jax.numpy as jnp
from jax import lax
from jax.experimental import pallas as pl
from jax.experimental.pallas import tpu as pltpu

F32 = jnp.float32
BF16 = jnp.bfloat16

D_MODEL = 1024
BATCH = 32
SEQ = 256
DEPTH = 2
DEC_BATCH = 4
DEC_SEQ = 4096
PAST_LEN = 256
GRID_W = 64
GRID_H = DEC_SEQ // GRID_W
HEAD_DIM = 64
H_A = 4
H_B = 8
H_C = 8
KV_C = 2
NA_KH = 8
NA_KW = 16
ROPE_THETA = 10000.0
N_EXPERTS = 16
N_GROUPS = 4
EXPERTS_PER_GROUP = N_EXPERTS // N_GROUPS
D_EXPERT = 256
BRANCH_W = 512
LN_EPS = 1e-5
RMS_EPS = 1e-6
QK_SCALE = HEAD_DIM ** -0.5
NEG_BIAS = -1e30

LANES = 128
N_COND = 8
VMEM_LIMIT = 56 * 1024 * 1024

OFF_QA, OFF_KA, OFF_VA, OFF_QB, OFF_KB, OFF_VB, OFF_QC, OFF_KC, OFF_VC = (
    0, 512, 1024, 1536, 2048, 2560, 3072, 3584, 3712)
IN_WIDTH = 3840
Q_PACK = 1536
KV_PACK = 1152

TM = 256
TQ_A = 128
TQ_C = 256


def _params(n_axes):
    return pltpu.CompilerParams(dimension_semantics=("parallel",) * n_axes,
                                vmem_limit_bytes=VMEM_LIMIT)


def _dot(a, b):
    return jnp.dot(a, b, preferred_element_type=F32)


def _dot_nt(a, b):
    return lax.dot_general(a, b, (((1,), (1,)), ((), ())), preferred_element_type=F32)


def _layernorm(z, g, b):
    mu = jnp.mean(z, axis=-1, keepdims=True)
    zc = z - mu
    var = jnp.mean(zc * zc, axis=-1, keepdims=True)
    return zc * lax.rsqrt(var + LN_EPS) * g + b


def _mod_body(c_ref, w_ref, b_ref, o_ref):
    c = c_ref[...]
    s = (c * jax.nn.sigmoid(c)).astype(BF16)
    o_ref[...] = _dot(s, w_ref[...].astype(BF16)) + b_ref[...]


def _modulation(cond, w_mod, b_mod):
    tn = 1536
    return pl.pallas_call(
        _mod_body,
        out_shape=jax.ShapeDtypeStruct((DEPTH, N_COND, 6 * D_MODEL), F32),
        grid=(DEPTH, 6 * D_MODEL // tn),
        in_specs=[
            pl.BlockSpec((N_COND, D_MODEL), lambda l, j: (0, 0)),
            pl.BlockSpec((None, D_MODEL, tn), lambda l, j: (l, 0, j)),
            pl.BlockSpec((None, 1, tn), lambda l, j: (l, 0, j)),
        ],
        out_specs=pl.BlockSpec((None, N_COND, tn), lambda l, j: (l, 0, j)),
        compiler_params=_params(2),
        name="modulation",
    )(cond, w_mod, b_mod.reshape(DEPTH, 1, 6 * D_MODEL))


def _in_proj_body(rope, x_ref, mod_ref, w_ref, qn_ref, kn_ref, g512_ref, g128_ref, *rest):
    if rope:
        ca_ref, cb_ref, cc_ref = rest[:3]
        outs = rest[3:]
    else:
        outs = rest
    x = x_ref[...]
    sh1 = mod_ref[:, 0:D_MODEL]
    sc1 = mod_ref[:, D_MODEL:2 * D_MODEL]
    h = (x * (1.0 + sc1) + sh1).astype(BF16)

    def proj(off, width):
        return _dot(h, w_ref[:, off:off + width])

    def rms(t, g_ref, wn_ref):
        t2 = t * t
        hi = t2.astype(BF16)
        lo = (t2 - hi.astype(F32)).astype(BF16)
        ms = _dot(hi, g_ref[...]) + _dot(lo, g_ref[...])
        return t * lax.rsqrt(ms + RMS_EPS) * wn_ref[...]

    def rot(t):
        if not rope:
            return t
        ca, cb, cc = ca_ref[...], cb_ref[...], cc_ref[...]
        pieces = []
        for j in range(t.shape[1] // LANES):
            blk = t[:, j * LANES:(j + 1) * LANES]
            pieces.append(blk * ca + pltpu.roll(blk, LANES - 1, 1) * cb + pltpu.roll(blk, 1, 1) * cc)
        return pieces[0] if len(pieces) == 1 else jnp.concatenate(pieces, axis=1)

    qa = rot(proj(OFF_QA, 512)) * QK_SCALE
    ka = rot(proj(OFF_KA, 512))
    va = proj(OFF_VA, 512)
    qb = proj(OFF_QB, 512) * QK_SCALE
    kb = proj(OFF_KB, 512)
    vb = proj(OFF_VB, 512)
    qc = rot(rms(proj(OFF_QC, 512), g512_ref, qn_ref)) * QK_SCALE
    kc = rot(rms(proj(OFF_KC, 128), g128_ref, kn_ref))
    vc = proj(OFF_VC, 128)

    q_ref = outs[0]
    q_ref[:, 0:512] = qa.astype(BF16)
    q_ref[:, 512:1024] = qb.astype(BF16)
    q_ref[:, 1024:1536] = qc.astype(BF16)
    if rope:
        k_ref, v_ref = outs[1:]
        k_ref[:, 0:512] = ka.astype(BF16)
        k_ref[:, 512:1024] = kb.astype(BF16)
        k_ref[:, 1024:1152] = kc.astype(BF16)
        v_ref[:, 0:512] = va.astype(BF16)
        v_ref[:, 512:1024] = vb.astype(BF16)
        v_ref[:, 1024:1152] = vc.astype(BF16)
    else:
        ka_ref, va_ref, kb_ref, vb_ref, kc_ref, vc_ref = outs[1:]
        ka_ref[...] = ka
        va_ref[...] = va
        kb_ref[...] = kb
        vb_ref[...] = vb
        kc_ref[...] = kc
        vc_ref[...] = vc


def _in_proj(x, mod_l, row_of_tile, w_bf, qn, kn, g512, g128, rope_tabs):
    n_tok = x.shape[0]
    n_tiles = n_tok // TM
    rope = rope_tabs is not None
    const = lambda i: (0, 0)
    in_specs = [
        pl.BlockSpec((TM, D_MODEL), lambda i: (i, 0)),
        pl.BlockSpec((None, 1, 6 * D_MODEL), lambda i: (row_of_tile(i), 0, 0)),
        pl.BlockSpec((D_MODEL, IN_WIDTH), const),
        pl.BlockSpec((1, 512), const),
        pl.BlockSpec((1, 128), const),
        pl.BlockSpec((512, 512), const),
        pl.BlockSpec((128, 128), const),
    ]
    args = [x, mod_l, w_bf, qn, kn, g512, g128]
    tile = lambda width: pl.BlockSpec((TM, width), lambda i: (i, 0))
    if rope:
        tiles_per_seq = DEC_SEQ // TM
        in_specs += [pl.BlockSpec((TM, LANES), lambda i: (i % tiles_per_seq, 0))] * 3
        args += list(rope_tabs)
        out_shape = [jax.ShapeDtypeStruct((n_tok, Q_PACK), BF16),
                     jax.ShapeDtypeStruct((n_tok, KV_PACK), BF16),
                     jax.ShapeDtypeStruct((n_tok, KV_PACK), BF16)]
        out_specs = [tile(Q_PACK), tile(KV_PACK), tile(KV_PACK)]
    else:
        widths = (512, 512, 512, 512, 128, 128)
        out_shape = [jax.ShapeDtypeStruct((n_tok, Q_PACK), BF16)] + [
            jax.ShapeDtypeStruct((n_tok, w), F32) for w in widths]
        out_specs = [tile(Q_PACK)] + [tile(w) for w in widths]
    return pl.pallas_call(
        functools.partial(_in_proj_body, rope),
        out_shape=out_shape,
        grid=(n_tiles,),
        in_specs=in_specs,
        out_specs=out_specs,
        compiler_params=_params(1),
        name="in_proj_latent" if rope else "in_proj_context",
    )(*args)


def _head_scores(q_pair, half, k_pair):
    lane = lax.broadcasted_iota(jnp.int32, q_pair.shape, 1)
    keep = (lane < HEAD_DIM) if half == 0 else (lane >= HEAD_DIM)
    qm = jnp.where(keep, q_pair, jnp.zeros_like(q_pair))
    return _dot_nt(qm, k_pair)


def _softmax_parts(s_list):
    mx = functools.reduce(jnp.maximum, [jnp.max(s, axis=-1, keepdims=True) for s in s_list])
    e_list = [jnp.exp(s - mx) for s in s_list]
    den = functools.reduce(operator.add, [jnp.sum(e, axis=-1, keepdims=True) for e in e_list])
    return e_list, 1.0 / den


def _pair_attention(q_pair, segs):
    outs = []
    for half in (0, 1):
        s_list = []
        for k_pair, _, bias in segs:
            s = _head_scores(q_pair, half, k_pair)
            if bias is not None:
                s = s + bias(half)
            s_list.append(s)
        e_list, inv = _softmax_parts(s_list)
        o = functools.reduce(operator.add, [_dot(e.astype(BF16), seg[1]) for e, seg in zip(e_list, segs)])
        outs.append(o * inv)
    lane = lax.broadcasted_iota(jnp.int32, outs[0].shape, 1)
    return jnp.where(lane < HEAD_DIM, outs[0], outs[1])


def _lambda(lam_ref, layer):
    lam_init = 0.8 - 0.6 * math.exp(-0.3 * layer)
    l = lam_ref[...]
    a = jnp.sum(l[0:1] * l[1:2], axis=-1, keepdims=True)
    b = jnp.sum(l[2:3] * l[3:4], axis=-1, keepdims=True)
    return jnp.exp(a) - jnp.exp(b) + lam_init, lam_init


def _diff_attention(q_blk, k_segs, v_segs, lam_v, lam_init, subln):
    outs = []
    for h in range(H_A):
        j, half = h // 2, h % 2
        parts = []
        for m in range(2):
            qb = q_blk(2 * m + j)
            s_list = [_head_scores(qb, half, ks(2 * m + j)) for ks in k_segs]
            parts.append(_softmax_parts(s_list))
        (e1, inv1), (e2, inv2) = parts
        c2 = lam_v * inv2
        o = functools.reduce(operator.add, [
            _dot((e1[s] * inv1 - e2[s] * c2).astype(BF16), v_segs[s](h)) for s in range(len(k_segs))])
        ms = jnp.mean(o * o, axis=-1, keepdims=True)
        outs.append(o * lax.rsqrt(ms + RMS_EPS) * subln * (1.0 - lam_init))
    return outs


def _tile(ref, j, dtype=None):
    t = ref[:, j * LANES:(j + 1) * LANES]
    return t if dtype is None else t.astype(dtype)


def _ctx_attn_body(layer, q_ref, ka_ref, va_ref, kb_ref, vb_ref, kc_ref, vc_ref, lam_ref, subln_ref,
                   oa_ref, ob_ref, oc_ref):
    lam_v, lam_init = _lambda(lam_ref, layer)
    oa = _diff_attention(lambda j: _tile(q_ref, j),
                         [lambda j: _tile(ka_ref, j, BF16)],
                         [lambda h: _tile(va_ref, h, BF16)],
                         lam_v, lam_init, subln_ref[...])
    for h in range(H_A):
        oa_ref[:, h * LANES:(h + 1) * LANES] = oa[h].astype(BF16)
    for j in range(4):
        o = _pair_attention(_tile(q_ref, 4 + j), [(_tile(kb_ref, j, BF16), _tile(vb_ref, j, BF16), None)])
        ob_ref[:, j * LANES:(j + 1) * LANES] = o.astype(BF16)
    kc = kc_ref[...].astype(BF16)
    vc = vc_ref[...].astype(BF16)
    for g in range(4):
        o = _pair_attention(_tile(q_ref, 8 + g), [(kc, vc, None)])
        oc_ref[:, g * LANES:(g + 1) * LANES] = o.astype(BF16)


def _ctx_attention(layer, q, ka, va, kb, vb, kc, vc, lam_l, subln_l):
    n_tok = q.shape[0]
    tile = lambda width: pl.BlockSpec((SEQ, width), lambda b: (b, 0))
    const = lambda b: (0, 0)
    return pl.pallas_call(
        functools.partial(_ctx_attn_body, layer),
        out_shape=[jax.ShapeDtypeStruct((n_tok, BRANCH_W), BF16)] * 3,
        grid=(n_tok // SEQ,),
        in_specs=[tile(Q_PACK), tile(512), tile(512), tile(512), tile(512), tile(128), tile(128),
                  pl.BlockSpec((4, HEAD_DIM), const), pl.BlockSpec((1, 2 * HEAD_DIM), const)],
        out_specs=[tile(BRANCH_W)] * 3,
        compiler_params=_params(1),
        name="context_attention",
    )(q, ka, va, kb, vb, kc, vc, lam_l, subln_l)


def _lat_diff_body(layer, q_ref, kl_ref, vl_ref, kc_ref, vc_ref, lam_ref, subln_ref, o_ref):
    lam_v, lam_init = _lambda(lam_ref, layer)
    oa = _diff_attention(lambda j: _tile(q_ref, j),
                         [lambda j: _tile(kc_ref, j, BF16), lambda j: _tile(kl_ref, j)],
                         [lambda h: _tile(vc_ref, h, BF16), lambda h: _tile(vl_ref, h)],
                         lam_v, lam_init, subln_ref[...])
    for h in range(H_A):
        o_ref[:, h * LANES:(h + 1) * LANES] = oa[h].astype(BF16)


def _lat_diff_attention(layer, q, k, v, cache_k, cache_v, lam_l, subln_l):
    nq = DEC_SEQ // TQ_A
    const = lambda b, i: (0, 0)
    return pl.pallas_call(
        functools.partial(_lat_diff_body, layer),
        out_shape=jax.ShapeDtypeStruct((DEC_BATCH * DEC_SEQ, BRANCH_W), BF16),
        grid=(DEC_BATCH, nq),
        in_specs=[
            pl.BlockSpec((TQ_A, 512), lambda b, i: (b * nq + i, 0)),
            pl.BlockSpec((DEC_SEQ, 512), lambda b, i: (b, 0)),
            pl.BlockSpec((DEC_SEQ, 512), lambda b, i: (b, 0)),
            pl.BlockSpec((None, None, PAST_LEN, 512), lambda b, i: (b, layer, 0, 0)),
            pl.BlockSpec((None, None, PAST_LEN, 512), lambda b, i: (b, layer, 0, 0)),
            pl.BlockSpec((4, HEAD_DIM), const),
            pl.BlockSpec((1, 2 * HEAD_DIM), const),
        ],
        out_specs=pl.BlockSpec((TQ_A, BRANCH_W), lambda b, i: (b * nq + i, 0)),
        compiler_params=_params(2),
        name="latent_diff_attention",
    )(q, k, v, cache_k, cache_v, lam_l, subln_l)


def _lat_gqa_body(q_ref, kl_ref, vl_ref, kc_ref, vc_ref, o_ref):
    segs = [(kc_ref[...].astype(BF16), vc_ref[...].astype(BF16), None),
            (kl_ref[...], vl_ref[...], None)]
    for g in range(4):
        o = _pair_attention(_tile(q_ref, g), segs)
        o_ref[:, g * LANES:(g + 1) * LANES] = o.astype(BF16)


def _lat_gqa_attention(layer, q, k, v, cache_k, cache_v):
    nq = DEC_SEQ // TQ_C
    return pl.pallas_call(
        _lat_gqa_body,
        out_shape=jax.ShapeDtypeStruct((DEC_BATCH * DEC_SEQ, BRANCH_W), BF16),
        grid=(DEC_BATCH, nq),
        in_specs=[
            pl.BlockSpec((TQ_C, 512), lambda b, i: (b * nq + i, 2)),
            pl.BlockSpec((DEC_SEQ, LANES), lambda b, i: (b, 1024 // LANES)),
            pl.BlockSpec((DEC_SEQ, LANES), lambda b, i: (b, 1024 // LANES)),
            pl.BlockSpec((None, None, PAST_LEN, LANES), lambda b, i: (b, layer, 0, 0)),
            pl.BlockSpec((None, None, PAST_LEN, LANES), lambda b, i: (b, layer, 0, 0)),
        ],
        out_specs=pl.BlockSpec((TQ_C, BRANCH_W), lambda b, i: (b * nq + i, 0)),
        compiler_params=_params(2),
        name="latent_gqa_attention",
    )(q, k, v, cache_k, cache_v)


def _lat_nbr_body(q_ref, kl_ref, vl_ref, kc_ref, vc_ref, bias_ref, o_ref):
    r = pl.program_id(1)
    row_start = jnp.clip(r - NA_KH // 2, 0, GRID_H - NA_KH)
    case = row_start - r + (NA_KH - 1)
    base = pl.multiple_of(row_start * GRID_W, GRID_W)
    n_loc = NA_KH * GRID_W
    for j in range(4):
        k_loc = kl_ref[pl.ds(base, n_loc), j * LANES:(j + 1) * LANES]
        v_loc = vl_ref[pl.ds(base, n_loc), j * LANES:(j + 1) * LANES]
        segs = [(k_loc, v_loc, lambda half, j=j: bias_ref[2 * j + half, case]),
                (_tile(kc_ref, j, BF16), _tile(vc_ref, j, BF16), None)]
        o = _pair_attention(_tile(q_ref, j), segs)
        o_ref[:, j * LANES:(j + 1) * LANES] = o.astype(BF16)


def _lat_nbr_attention(layer, q, k, v, cache_k, cache_v, bias_tab):
    return pl.pallas_call(
        _lat_nbr_body,
        out_shape=jax.ShapeDtypeStruct((DEC_BATCH * DEC_SEQ, BRANCH_W), BF16),
        grid=(DEC_BATCH, GRID_H),
        in_specs=[
            pl.BlockSpec((GRID_W, 512), lambda b, r: (b * GRID_H + r, 1)),
            pl.BlockSpec((DEC_SEQ, 512), lambda b, r: (b, 1)),
            pl.BlockSpec((DEC_SEQ, 512), lambda b, r: (b, 1)),
            pl.BlockSpec((None, None, PAST_LEN, 512), lambda b, r: (b, layer, 0, 0)),
            pl.BlockSpec((None, None, PAST_LEN, 512), lambda b, r: (b, layer, 0, 0)),
            pl.BlockSpec((H_B, NA_KH, GRID_W, NA_KH * GRID_W), lambda b, r: (0, 0, 0, 0)),
        ],
        out_specs=pl.BlockSpec((GRID_W, BRANCH_W), lambda b, r: (b * GRID_H + r, 0)),
        compiler_params=_params(2),
        name="latent_neighbourhood_attention",
    )(q, k, v, cache_k, cache_v, bias_tab)


def _merge_body(alpha, x_ref, oa_ref, ob_ref, oc_ref, mod_ref, wg_ref, bg_ref, wbr_ref, wo_ref,
                lng_ref, lnb_ref, out_ref):
    x = x_ref[...]
    sh1 = mod_ref[:, 0:D_MODEL]
    sc1 = mod_ref[:, D_MODEL:2 * D_MODEL]
    g1 = mod_ref[:, 2 * D_MODEL:3 * D_MODEL]
    h = (x * (1.0 + sc1) + sh1).astype(BF16)
    acc = None
    for i, o_ref in enumerate((oa_ref, ob_ref, oc_ref)):
        cols = slice(i * D_MODEL, (i + 1) * D_MODEL)
        gate = jax.nn.sigmoid(_dot(h, wg_ref[:, cols]) + bg_ref[:, cols])
        term = gate * _dot(o_ref[...], wbr_ref[i])
        acc = term if acc is None else acc + term
    y = _dot(acc.astype(BF16), wo_ref[...])
    out_ref[...] = _layernorm(alpha * x + g1 * y, lng_ref[...], lnb_ref[...])


def _merge(alpha, x, oa, ob, oc, mod_l, row_of_tile, wg, bg, wbr, wo, lng, lnb):
    n_tok = x.shape[0]
    const = lambda i: (0, 0)
    tile = lambda width: pl.BlockSpec((TM, width), lambda i: (i, 0))
    return pl.pallas_call(
        functools.partial(_merge_body, alpha),
        out_shape=jax.ShapeDtypeStruct((n_tok, D_MODEL), F32),
        grid=(n_tok // TM,),
        in_specs=[
            tile(D_MODEL), tile(BRANCH_W), tile(BRANCH_W), tile(BRANCH_W),
            pl.BlockSpec((None, 1, 6 * D_MODEL), lambda i: (row_of_tile(i), 0, 0)),
            pl.BlockSpec((D_MODEL, 3 * D_MODEL), const),
            pl.BlockSpec((1, 3 * D_MODEL), const),
            pl.BlockSpec((3, BRANCH_W, D_MODEL), lambda i: (0, 0, 0)),
            pl.BlockSpec((D_MODEL, D_MODEL), const),
            pl.BlockSpec((1, D_MODEL), const),
            pl.BlockSpec((1, D_MODEL), const),
        ],
        out_specs=tile(D_MODEL),
        compiler_params=_params(1),
        name="branch_merge",
    )(x, oa, ob, oc, mod_l, wg, bg, wbr, wo, lng, lnb)


def _routing(logits, bias):
    scores = jax.nn.sigmoid(logits)
    sel_all = scores + bias
    sel = [sel_all[e:e + 1] for e in range(N_EXPERTS)]
    sc = [scores[e:e + 1] for e in range(N_EXPERTS)]
    gscore = []
    for g in range(N_GROUPS):
        a, b, c, d = sel[4 * g:4 * g + 4]
        hi1, lo1 = jnp.maximum(a, b), jnp.minimum(a, b)
        hi2, lo2 = jnp.maximum(c, d), jnp.minimum(c, d)
        gscore.append(jnp.maximum(hi1, hi2) + jnp.maximum(jnp.minimum(hi1, hi2), jnp.maximum(lo1, lo2)))
    best = []
    taken = None
    for g in range(N_GROUPS):
        ok = functools.reduce(operator.and_, [gscore[g] >= gscore[k] for k in range(g + 1, N_GROUPS)],
                              jnp.ones_like(gscore[g], dtype=jnp.bool_))
        if taken is not None:
            ok = ok & jnp.logical_not(taken)
        best.append(ok)
        taken = ok if taken is None else (taken | ok)
    one = jnp.ones_like(sel[0])
    zero = jnp.zeros_like(sel[0])
    picked = []
    for e in range(N_EXPERTS):
        g, i = divmod(e, EXPERTS_PER_GROUP)
        rank = zero
        for k in range(EXPERTS_PER_GROUP):
            if k == i:
                continue
            other = sel[4 * g + k]
            ahead = (other >= sel[e]) if k < i else (other > sel[e])
            rank = rank + jnp.where(ahead, one, zero)
        picked.append(jnp.where((rank < 2.0) & best[g], sc[e], zero))
    inv = 1.0 / functools.reduce(operator.add, picked)
    return jnp.concatenate([p * inv for p in picked], axis=0)


def _moe_body(alpha, x_ref, mod_ref, wrt_ref, rb_ref, w13_ref, w2_ref, lng_ref, lnb_ref, out_ref):
    x = x_ref[...]
    sh2 = mod_ref[:, 3 * D_MODEL:4 * D_MODEL]
    sc2 = mod_ref[:, 4 * D_MODEL:5 * D_MODEL]
    g2 = mod_ref[:, 5 * D_MODEL:6 * D_MODEL]
    h2 = x * (1.0 + sc2) + sh2
    hi = h2.astype(BF16)
    lo = (h2 - hi.astype(F32)).astype(BF16)
    wr = wrt_ref[...]
    wr_hi = wr.astype(BF16)
    wr_lo = (wr - wr_hi.astype(F32)).astype(BF16)
    logits = _dot_nt(wr_hi, hi) + _dot_nt(wr_hi, lo) + _dot_nt(wr_lo, hi)
    comb_t = _routing(logits, rb_ref[...])
    tm = x.shape[0]
    comb = jnp.concatenate([comb_t, jnp.zeros((LANES - N_EXPERTS, tm), F32)], axis=0).T
    lane = lax.broadcasted_iota(jnp.int32, comb.shape, 1)

    def expert(e, acc):
        h13 = _dot(hi, w13_ref[e])
        a = h13[:, :D_EXPERT]
        b = h13[:, D_EXPERT:]
        he = (a * jax.nn.sigmoid(a) * b).astype(BF16)
        y = _dot(he, w2_ref[e])
        col = jnp.sum(jnp.where(lane == e, comb, 0.0), axis=1, keepdims=True)
        return acc + col * y

    m = lax.fori_loop(0, N_EXPERTS, expert, jnp.zeros((tm, D_MODEL), F32))
    out_ref[...] = _layernorm(alpha * x + g2 * m, lng_ref[...], lnb_ref[...])


def _moe(alpha, x, mod_l, row_of_tile, wrt, rb, w13, w2, lng, lnb):
    n_tok = x.shape[0]
    const = lambda i: (0, 0)
    tile = pl.BlockSpec((TM, D_MODEL), lambda i: (i, 0))
    resident = pl.Buffered(1)
    return pl.pallas_call(
        functools.partial(_moe_body, alpha),
        out_shape=jax.ShapeDtypeStruct((n_tok, D_MODEL), F32),
        grid=(n_tok // TM,),
        in_specs=[
            tile,
            pl.BlockSpec((None, 1, 6 * D_MODEL), lambda i: (row_of_tile(i), 0, 0)),
            pl.BlockSpec((N_EXPERTS, D_MODEL), const),
            pl.BlockSpec((N_EXPERTS, 1), const),
            pl.BlockSpec((N_EXPERTS, D_MODEL, 2 * D_EXPERT), lambda i: (0, 0, 0), pipeline_mode=resident),
            pl.BlockSpec((N_EXPERTS, D_EXPERT, D_MODEL), lambda i: (0, 0, 0), pipeline_mode=resident),
            pl.BlockSpec((1, D_MODEL), const),
            pl.BlockSpec((1, D_MODEL), const),
        ],
        out_specs=tile,
        compiler_params=_params(1),
        name="mixture_of_experts",
    )(x, mod_l, wrt, rb, w13, w2, lng, lnb)


def _rope_tables():
    t = jnp.arange(DEC_SEQ)
    row = (t // GRID_W).astype(F32)
    col = (t % GRID_W).astype(F32)
    n = HEAD_DIM // 4
    freqs = ROPE_THETA ** (-jnp.arange(n, dtype=F32) / n)
    ang = jnp.concatenate([row[:, None] * freqs, col[:, None] * freqs], axis=-1)
    cos = jnp.tile(jnp.repeat(jnp.cos(ang), 2, axis=-1), (1, LANES // HEAD_DIM))
    sin = jnp.tile(jnp.repeat(jnp.sin(ang), 2, axis=-1), (1, LANES // HEAD_DIM))
    even = (jnp.arange(LANES) % 2 == 0)[None, :]
    return cos, jnp.where(even, -sin, 0.0), jnp.where(even, 0.0, sin)


def _group_mean_matrix(width):
    g = jnp.arange(width) // HEAD_DIM
    return jnp.where(g[:, None] == g[None, :], 1.0 / HEAD_DIM, 0.0).astype(BF16)


def _neighbourhood_bias(rpb_l):
    cx = jnp.arange(GRID_W)
    col_start = jnp.clip(cx - NA_KW // 2, 0, GRID_W - NA_KW)
    kx = jnp.arange(GRID_W)
    inside = (kx[None, :] >= col_start[:, None]) & (kx[None, :] < col_start[:, None] + NA_KW)
    dx = jnp.clip(kx[None, :] - cx[:, None] + (NA_KW - 1), 0, 2 * NA_KW - 2)
    tab = jnp.where(inside[None, None], rpb_l[:, :, dx], NEG_BIAS)
    cat = jnp.stack([tab[:, c:c + NA_KH] for c in range(NA_KH)], axis=1)
    return cat.transpose(0, 1, 3, 2, 4).reshape(H_B, NA_KH, GRID_W, NA_KH * GRID_W)


def _gqa_column_order():
    g, n, d = jnp.meshgrid(jnp.arange(H_C // KV_C), jnp.arange(KV_C), jnp.arange(HEAD_DIM), indexing="ij")
    return (n * (H_C // KV_C) * HEAD_DIM + g * HEAD_DIM + d).reshape(-1)


def kernel(x_prompt, x_sample, c, cache_a_k, cache_a_v, cache_b_k, cache_b_v, cache_c_k, cache_c_v,
           c_ctx, w_mod, b_mod, w_in, w_gate, b_gate, lam, a_subln, rpb, c_qnorm, c_knorm,
           w_br, w_o, ln_g, ln_b, w_router, router_bias, moe_w1, moe_w3, moe_w2):
    alpha = (2.0 * DEPTH) ** 0.25
    cond = jnp.concatenate([c_ctx[None, :], c, jnp.zeros((N_COND - 1 - DEC_BATCH, D_MODEL), F32)], axis=0)
    mods = _modulation(cond, w_mod, b_mod).reshape(DEPTH, N_COND, 1, 6 * D_MODEL)

    rope_tabs = _rope_tables()
    g512 = _group_mean_matrix(512)
    g128 = _group_mean_matrix(128)
    perm = _gqa_column_order()
    wrt = w_router.T
    rb = router_bias.reshape(N_EXPERTS, 1)

    ck_a = cache_a_k.reshape(DEC_BATCH, DEPTH, PAST_LEN, 512)
    cv_a = cache_a_v.reshape(DEC_BATCH, DEPTH, PAST_LEN, 512)
    ck_b = cache_b_k.reshape(DEC_BATCH, DEPTH, PAST_LEN, 512)
    cv_b = cache_b_v.reshape(DEC_BATCH, DEPTH, PAST_LEN, 512)
    ck_c = cache_c_k.reshape(DEC_BATCH, DEPTH, PAST_LEN, 128)
    cv_c = cache_c_v.reshape(DEC_BATCH, DEPTH, PAST_LEN, 128)

    ctx_row = lambda i: 0
    lat_tiles = DEC_SEQ // TM
    lat_row = lambda i: 1 + i // lat_tiles

    xp = x_prompt.reshape(BATCH * SEQ, D_MODEL)
    xs = x_sample.reshape(DEC_BATCH * DEC_SEQ, D_MODEL)
    new_kv = []
    for l in range(DEPTH):
        w_in_l = w_in[l]
        w_in_l = jnp.concatenate([w_in_l[:, :OFF_QC], w_in_l[:, OFF_QC:OFF_KC][:, perm], w_in_l[:, OFF_KC:]],
                                 axis=1).astype(BF16)
        qn = jnp.tile(c_qnorm[l], 512 // HEAD_DIM)[None, :]
        kn = jnp.tile(c_knorm[l], 128 // HEAD_DIM)[None, :]
        wg = w_gate[l].astype(BF16)
        bg = b_gate[l][None, :]
        wbr = jnp.stack([w_br[l, 0], w_br[l, 1], w_br[l, 2][perm]], axis=0).astype(BF16)
        wo = w_o[l].astype(BF16)
        w13 = jnp.concatenate([moe_w1[l], moe_w3[l]], axis=-1).astype(BF16)
        w2 = moe_w2[l].astype(BF16)
        lam_l = lam[l]
        subln_l = a_subln[l][None, :]
        bias_tab = _neighbourhood_bias(rpb[l])
        mod_l = mods[l]

        q, ka, va, kb, vb, kc, vc = _in_proj(xp, mod_l, ctx_row, w_in_l, qn, kn, g512, g128, None)
        new_kv.append((ka, va, kb, vb, kc, vc))
        oa, ob, oc = _ctx_attention(l, q, ka, va, kb, vb, kc, vc, lam_l, subln_l)
        xp = _merge(alpha, xp, oa, ob, oc, mod_l, ctx_row, wg, bg, wbr, wo, ln_g[l, 0:1], ln_b[l, 0:1])
        xp = _moe(alpha, xp, mod_l, ctx_row, wrt, rb, w13, w2, ln_g[l, 1:2], ln_b[l, 1:2])

        q, k, v = _in_proj(xs, mod_l, lat_row, w_in_l, qn, kn, g512, g128, rope_tabs)
        oa = _lat_diff_attention(l, q, k, v, ck_a, cv_a, lam_l, subln_l)
        ob = _lat_nbr_attention(l, q, k, v, ck_b, cv_b, bias_tab)
        oc = _lat_gqa_attention(l, q, k, v, ck_c, cv_c)
        xs = _merge(alpha, xs, oa, ob, oc, mod_l, lat_row, wg, bg, wbr, wo, ln_g[l, 0:1], ln_b[l, 0:1])
        xs = _moe(alpha, xs, mod_l, lat_row, wrt, rb, w13, w2, ln_g[l, 1:2], ln_b[l, 1:2])

    def cache(idx, shape):
        return jnp.stack([new_kv[l][idx].reshape(BATCH, SEQ, *shape) for l in range(DEPTH)], axis=1)

    return (xp.reshape(BATCH, SEQ, D_MODEL), xs.reshape(DEC_BATCH, DEC_SEQ, D_MODEL),
            cache(0, (2, H_A, HEAD_DIM)), cache(1, (H_A, 2 * HEAD_DIM)),
            cache(2, (H_B, HEAD_DIM)), cache(3, (H_B, HEAD_DIM)),
            cache(4, (KV_C, HEAD_DIM)), cache(5, (KV_C, HEAD_DIM)))
```

```python
import functools
import math
import operator

import jax
import jax.numpy as jnp
from jax import lax
from jax.experimental import pallas as pl
from jax.experimental.pallas import tpu as pltpu

F32 = jnp.float32
BF16 = jnp.bfloat16

D_MODEL = 1024
BATCH = 32
SEQ = 256
DEPTH = 2
DEC_BATCH = 4
DEC_SEQ = 4096
PAST_LEN = 256
GRID_W = 64
GRID_H = DEC_SEQ // GRID_W
HEAD_DIM = 64
H_A = 4
H_B = 8
H_C = 8
KV_C = 2
NA_KH = 8
NA_KW = 16
ROPE_THETA = 10000.0
N_EXPERTS = 16
N_GROUPS = 4
EXPERTS_PER_GROUP = N_EXPERTS // N_GROUPS
D_EXPERT = 256
BRANCH_W = 512
LN_EPS = 1e-5
RMS_EPS = 1e-6
QK_SCALE = HEAD_DIM ** -0.5
NEG_BIAS = -1e30

LANES = 128
N_COND = 8
VMEM_LIMIT = 56 * 1024 * 1024

OFF_QA, OFF_KA, OFF_VA, OFF_QB, OFF_KB, OFF_VB, OFF_QC, OFF_KC, OFF_VC = (
    0, 512, 1024, 1536, 2048, 2560, 3072, 3584, 3712)
IN_WIDTH = 3840
Q_PACK = 1536
KV_PACK = 1152

TM = 256
TR = 256
GROUP_PAIRS = tuple((i, j) for i in range(EXPERTS_PER_GROUP) for j in range(i + 1, EXPERTS_PER_GROUP))
N_BUCKETS = N_GROUPS * len(GROUP_PAIRS)
BUCKET_ROWS = 32
N_TOKENS = BATCH * SEQ + DEC_BATCH * DEC_SEQ
SORTED_ROWS = N_TOKENS + N_BUCKETS * TR
TQ_A = 128
TQ_C = 256


def _params(n_axes):
    return pltpu.CompilerParams(dimension_semantics=("parallel",) * n_axes,
                                vmem_limit_bytes=VMEM_LIMIT)


def _dot(a, b):
    return jnp.dot(a, b, preferred_element_type=F32)


def _dot_nt(a, b):
    return lax.dot_general(a, b, (((1,), (1,)), ((), ())), preferred_element_type=F32)


def _layernorm(z, g, b):
    mu = jnp.mean(z, axis=-1, keepdims=True)
    zc = z - mu
    var = jnp.mean(zc * zc, axis=-1, keepdims=True)
    return zc * lax.rsqrt(var + LN_EPS) * g + b


def _mod_body(c_ref, w_ref, b_ref, o_ref):
    c = c_ref[...]
    s = (c * jax.nn.sigmoid(c)).astype(BF16)
    o_ref[...] = _dot(s, w_ref[...].astype(BF16)) + b_ref[...]


def _modulation(cond, w_mod, b_mod):
    tn = 1536
    return pl.pallas_call(
        _mod_body,
        out_shape=jax.ShapeDtypeStruct((DEPTH, N_COND, 6 * D_MODEL), F32),
        grid=(DEPTH, 6 * D_MODEL // tn),
        in_specs=[
            pl.BlockSpec((N_COND, D_MODEL), lambda l, j: (0, 0)),
            pl.BlockSpec((None, D_MODEL, tn), lambda l, j: (l, 0, j)),
            pl.BlockSpec((None, 1, tn), lambda l, j: (l, 0, j)),
        ],
        out_specs=pl.BlockSpec((None, N_COND, tn), lambda l, j: (l, 0, j)),
        compiler_params=_params(2),
        name="modulation",
    )(cond, w_mod, b_mod.reshape(DEPTH, 1, 6 * D_MODEL))


def _in_proj_body(rope, x_ref, mod_ref, w_ref, qn_ref, kn_ref, g512_ref, g128_ref, *rest):
    if rope:
        ca_ref, cb_ref, cc_ref = rest[:3]
        outs = rest[3:]
    else:
        outs = rest
    x = x_ref[...]
    sh1 = mod_ref[:, 0:D_MODEL]
    sc1 = mod_ref[:, D_MODEL:2 * D_MODEL]
    h = (x * (1.0 + sc1) + sh1).astype(BF16)

    def proj(off, width):
        return _dot(h, w_ref[:, off:off + width])

    def rms(t, g_ref, wn_ref):
        t2 = t * t
        hi = t2.astype(BF16)
        lo = (t2 - hi.astype(F32)).astype(BF16)
        ms = _dot(hi, g_ref[...]) + _dot(lo, g_ref[...])
        return t * lax.rsqrt(ms + RMS_EPS) * wn_ref[...]

    def rot(t):
        if not rope:
            return t
        ca, cb, cc = ca_ref[...], cb_ref[...], cc_ref[...]
        pieces = []
        for j in range(t.shape[1] // LANES):
            blk = t[:, j * LANES:(j + 1) * LANES]
            pieces.append(blk * ca + pltpu.roll(blk, LANES - 1, 1) * cb + pltpu.roll(blk, 1, 1) * cc)
        return pieces[0] if len(pieces) == 1 else jnp.concatenate(pieces, axis=1)

    qa = rot(proj(OFF_QA, 512)) * QK_SCALE
    ka = rot(proj(OFF_KA, 512))
    va = proj(OFF_VA, 512)
    qb = proj(OFF_QB, 512) * QK_SCALE
    kb = proj(OFF_KB, 512)
    vb = proj(OFF_VB, 512)
    qc = rot(rms(proj(OFF_QC, 512), g512_ref, qn_ref)) * QK_SCALE
    kc = rot(rms(proj(OFF_KC, 128), g128_ref, kn_ref))
    vc = proj(OFF_VC, 128)

    q_ref = outs[0]
    q_ref[:, 0:512] = qa.astype(BF16)
    q_ref[:, 512:1024] = qb.astype(BF16)
    q_ref[:, 1024:1536] = qc.astype(BF16)
    if rope:
        k_ref, v_ref = outs[1:]
        k_ref[:, 0:512] = ka.astype(BF16)
        k_ref[:, 512:1024] = kb.astype(BF16)
        k_ref[:, 1024:1152] = kc.astype(BF16)
        v_ref[:, 0:512] = va.astype(BF16)
        v_ref[:, 512:1024] = vb.astype(BF16)
        v_ref[:, 1024:1152] = vc.astype(BF16)
    else:
        ka_ref, va_ref, kb_ref, vb_ref, kc_ref, vc_ref = outs[1:]
        ka_ref[...] = ka
        va_ref[...] = va
        kb_ref[...] = kb
        vb_ref[...] = vb
        kc_ref[...] = kc
        vc_ref[...] = vc


def _in_proj(x, mod_l, row_of_tile, w_bf, qn, kn, g512, g128, rope_tabs):
    n_tok = x.shape[0]
    n_tiles = n_tok // TM
    rope = rope_tabs is not None
    const = lambda i: (0, 0)
    in_specs = [
        pl.BlockSpec((TM, D_MODEL), lambda i: (i, 0)),
        pl.BlockSpec((None, 1, 6 * D_MODEL), lambda i: (row_of_tile(i), 0, 0)),
        pl.BlockSpec((D_MODEL, IN_WIDTH), const),
        pl.BlockSpec((1, 512), const),
        pl.BlockSpec((1, 128), const),
        pl.BlockSpec((512, 512), const),
        pl.BlockSpec((128, 128), const),
    ]
    args = [x, mod_l, w_bf, qn, kn, g512, g128]
    tile = lambda width: pl.BlockSpec((TM, width), lambda i: (i, 0))
    if rope:
        tiles_per_seq = DEC_SEQ // TM
        in_specs += [pl.BlockSpec((TM, LANES), lambda i: (i % tiles_per_seq, 0))] * 3
        args += list(rope_tabs)
        out_shape = [jax.ShapeDtypeStruct((n_tok, Q_PACK), BF16),
                     jax.ShapeDtypeStruct((n_tok, KV_PACK), BF16),
                     jax.ShapeDtypeStruct((n_tok, KV_PACK), BF16)]
        out_specs = [tile(Q_PACK), tile(KV_PACK), tile(KV_PACK)]
    else:
        widths = (512, 512, 512, 512, 128, 128)
        out_shape = [jax.ShapeDtypeStruct((n_tok, Q_PACK), BF16)] + [
            jax.ShapeDtypeStruct((n_tok, w), F32) for w in widths]
        out_specs = [tile(Q_PACK)] + [tile(w) for w in widths]
    return pl.pallas_call(
        functools.partial(_in_proj_body, rope),
        out_shape=out_shape,
        grid=(n_tiles,),
        in_specs=in_specs,
        out_specs=out_specs,
        compiler_params=_params(1),
        name="in_proj_latent" if rope else "in_proj_context",
    )(*args)


def _head_scores(q_pair, half, k_pair):
    lane = lax.broadcasted_iota(jnp.int32, q_pair.shape, 1)
    keep = (lane < HEAD_DIM) if half == 0 else (lane >= HEAD_DIM)
    qm = jnp.where(keep, q_pair, jnp.zeros_like(q_pair))
    return _dot_nt(qm, k_pair)


def _softmax_parts(s_list):
    mx = functools.reduce(jnp.maximum, [jnp.max(s, axis=-1, keepdims=True) for s in s_list])
    e_list = [jnp.exp(s - mx) for s in s_list]
    den = functools.reduce(operator.add, [jnp.sum(e, axis=-1, keepdims=True) for e in e_list])
    return e_list, 1.0 / den


def _pair_attention(q_pair, segs):
    outs = []
    for half in (0, 1):
        s_list = []
        for k_pair, _, bias in segs:
            s = _head_scores(q_pair, half, k_pair)
            if bias is not None:
                s = s + bias(half)
            s_list.append(s)
        e_list, inv = _softmax_parts(s_list)
        o = functools.reduce(operator.add, [_dot(e.astype(BF16), seg[1]) for e, seg in zip(e_list, segs)])
        outs.append(o * inv)
    lane = lax.broadcasted_iota(jnp.int32, outs[0].shape, 1)
    return jnp.where(lane < HEAD_DIM, outs[0], outs[1])


def _lambda(lam_ref, layer):
    lam_init = 0.8 - 0.6 * math.exp(-0.3 * layer)
    l = lam_ref[...]
    a = jnp.sum(l[0:1] * l[1:2], axis=-1, keepdims=True)
    b = jnp.sum(l[2:3] * l[3:4], axis=-1, keepdims=True)
    return jnp.exp(a) - jnp.exp(b) + lam_init, lam_init


def _diff_attention(q_blk, k_segs, v_segs, lam_v, lam_init, subln):
    outs = []
    for h in range(H_A):
        j, half = h // 2, h % 2
        parts = []
        for m in range(2):
            qb = q_blk(2 * m + j)
            s_list = [_head_scores(qb, half, ks(2 * m + j)) for ks in k_segs]
            parts.append(_softmax_parts(s_list))
        (e1, inv1), (e2, inv2) = parts
        c2 = lam_v * inv2
        o = functools.reduce(operator.add, [
            _dot((e1[s] * inv1 - e2[s] * c2).astype(BF16), v_segs[s](h)) for s in range(len(k_segs))])
        ms = jnp.mean(o * o, axis=-1, keepdims=True)
        outs.append(o * lax.rsqrt(ms + RMS_EPS) * subln * (1.0 - lam_init))
    return outs


def _tile(ref, j, dtype=None):
    t = ref[:, j * LANES:(j + 1) * LANES]
    return t if dtype is None else t.astype(dtype)


def _ctx_attn_body(layer, q_ref, ka_ref, va_ref, kb_ref, vb_ref, kc_ref, vc_ref, lam_ref, subln_ref,
                   oa_ref, ob_ref, oc_ref):
    lam_v, lam_init = _lambda(lam_ref, layer)
    oa = _diff_attention(lambda j: _tile(q_ref, j),
                         [lambda j: _tile(ka_ref, j, BF16)],
                         [lambda h: _tile(va_ref, h, BF16)],
                         lam_v, lam_init, subln_ref[...])
    for h in range(H_A):
        oa_ref[:, h * LANES:(h + 1) * LANES] = oa[h].astype(BF16)
    for j in range(4):
        o = _pair_attention(_tile(q_ref, 4 + j), [(_tile(kb_ref, j, BF16), _tile(vb_ref, j, BF16), None)])
        ob_ref[:, j * LANES:(j + 1) * LANES] = o.astype(BF16)
    kc = kc_ref[...].astype(BF16)
    vc = vc_ref[...].astype(BF16)
    for g in range(4):
        o = _pair_attention(_tile(q_ref, 8 + g), [(kc, vc, None)])
        oc_ref[:, g * LANES:(g + 1) * LANES] = o.astype(BF16)


def _ctx_attention(layer, q, ka, va, kb, vb, kc, vc, lam_l, subln_l):
    n_tok = q.shape[0]
    tile = lambda width: pl.BlockSpec((SEQ, width), lambda b: (b, 0))
    const = lambda b: (0, 0)
    return pl.pallas_call(
        functools.partial(_ctx_attn_body, layer),
        out_shape=[jax.ShapeDtypeStruct((n_tok, BRANCH_W), BF16)] * 3,
        grid=(n_tok // SEQ,),
        in_specs=[tile(Q_PACK), tile(512), tile(512), tile(512), tile(512), tile(128), tile(128),
                  pl.BlockSpec((4, HEAD_DIM), const), pl.BlockSpec((1, 2 * HEAD_DIM), const)],
        out_specs=[tile(BRANCH_W)] * 3,
        compiler_params=_params(1),
        name="context_attention",
    )(q, ka, va, kb, vb, kc, vc, lam_l, subln_l)


def _lat_diff_body(layer, q_ref, kl_ref, vl_ref, kc_ref, vc_ref, lam_ref, subln_ref, o_ref):
    lam_v, lam_init = _lambda(lam_ref, layer)
    oa = _diff_attention(lambda j: _tile(q_ref, j),
                         [lambda j: _tile(kc_ref, j, BF16), lambda j: _tile(kl_ref, j)],
                         [lambda h: _tile(vc_ref, h, BF16), lambda h: _tile(vl_ref, h)],
                         lam_v, lam_init, subln_ref[...])
    for h in range(H_A):
        o_ref[:, h * LANES:(h + 1) * LANES] = oa[h].astype(BF16)


def _lat_diff_attention(layer, q, k, v, cache_k, cache_v, lam_l, subln_l):
    nq = DEC_SEQ // TQ_A
    const = lambda b, i: (0, 0)
    return pl.pallas_call(
        functools.partial(_lat_diff_body, layer),
        out_shape=jax.ShapeDtypeStruct((DEC_BATCH * DEC_SEQ, BRANCH_W), BF16),
        grid=(DEC_BATCH, nq),
        in_specs=[
            pl.BlockSpec((TQ_A, 512), lambda b, i: (b * nq + i, 0)),
            pl.BlockSpec((DEC_SEQ, 512), lambda b, i: (b, 0)),
            pl.BlockSpec((DEC_SEQ, 512), lambda b, i: (b, 0)),
            pl.BlockSpec((None, None, PAST_LEN, 512), lambda b, i: (b, layer, 0, 0)),
            pl.BlockSpec((None, None, PAST_LEN, 512), lambda b, i: (b, layer, 0, 0)),
            pl.BlockSpec((4, HEAD_DIM), const),
            pl.BlockSpec((1, 2 * HEAD_DIM), const),
        ],
        out_specs=pl.BlockSpec((TQ_A, BRANCH_W), lambda b, i: (b * nq + i, 0)),
        compiler_params=_params(2),
        name="latent_diff_attention",
    )(q, k, v, cache_k, cache_v, lam_l, subln_l)


def _lat_gqa_body(q_ref, kl_ref, vl_ref, kc_ref, vc_ref, o_ref):
    segs = [(kc_ref[...].astype(BF16), vc_ref[...].astype(BF16), None),
            (kl_ref[...], vl_ref[...], None)]
    for g in range(4):
        o = _pair_attention(_tile(q_ref, g), segs)
        o_ref[:, g * LANES:(g + 1) * LANES] = o.astype(BF16)


def _lat_gqa_attention(layer, q, k, v, cache_k, cache_v):
    nq = DEC_SEQ // TQ_C
    return pl.pallas_call(
        _lat_gqa_body,
        out_shape=jax.ShapeDtypeStruct((DEC_BATCH * DEC_SEQ, BRANCH_W), BF16),
        grid=(DEC_BATCH, nq),
        in_specs=[
            pl.BlockSpec((TQ_C, 512), lambda b, i: (b * nq + i, 2)),
            pl.BlockSpec((DEC_SEQ, LANES), lambda b, i: (b, 1024 // LANES)),
            pl.BlockSpec((DEC_SEQ, LANES), lambda b, i: (b, 1024 // LANES)),
            pl.BlockSpec((None, None, PAST_LEN, LANES), lambda b, i: (b, layer, 0, 0)),
            pl.BlockSpec((None, None, PAST_LEN, LANES), lambda b, i: (b, layer, 0, 0)),
        ],
        out_specs=pl.BlockSpec((TQ_C, BRANCH_W), lambda b, i: (b * nq + i, 0)),
        compiler_params=_params(2),
        name="latent_gqa_attention",
    )(q, k, v, cache_k, cache_v)


def _lat_nbr_body(q_ref, kl_ref, vl_ref, kc_ref, vc_ref, bias_ref, o_ref):
    r = pl.program_id(1)
    row_start = jnp.clip(r - NA_KH // 2, 0, GRID_H - NA_KH)
    case = row_start - r + (NA_KH - 1)
    base = pl.multiple_of(row_start * GRID_W, GRID_W)
    n_loc = NA_KH * GRID_W
    for j in range(4):
        k_loc = kl_ref[pl.ds(base, n_loc), j * LANES:(j + 1) * LANES]
        v_loc = vl_ref[pl.ds(base, n_loc), j * LANES:(j + 1) * LANES]
        segs = [(k_loc, v_loc, lambda half, j=j: bias_ref[2 * j + half, case]),
                (_tile(kc_ref, j, BF16), _tile(vc_ref, j, BF16), None)]
        o = _pair_attention(_tile(q_ref, j), segs)
        o_ref[:, j * LANES:(j + 1) * LANES] = o.astype(BF16)


def _lat_nbr_attention(layer, q, k, v, cache_k, cache_v, bias_tab):
    return pl.pallas_call(
        _lat_nbr_body,
        out_shape=jax.ShapeDtypeStruct((DEC_BATCH * DEC_SEQ, BRANCH_W), BF16),
        grid=(DEC_BATCH, GRID_H),
        in_specs=[
            pl.BlockSpec((GRID_W, 512), lambda b, r: (b * GRID_H + r, 1)),
            pl.BlockSpec((DEC_SEQ, 512), lambda b, r: (b, 1)),
            pl.BlockSpec((DEC_SEQ, 512), lambda b, r: (b, 1)),
            pl.BlockSpec((None, None, PAST_LEN, 512), lambda b, r: (b, layer, 0, 0)),
            pl.BlockSpec((None, None, PAST_LEN, 512), lambda b, r: (b, layer, 0, 0)),
            pl.BlockSpec((H_B, NA_KH, GRID_W, NA_KH * GRID_W), lambda b, r: (0, 0, 0, 0)),
        ],
        out_specs=pl.BlockSpec((GRID_W, BRANCH_W), lambda b, r: (b * GRID_H + r, 0)),
        compiler_params=_params(2),
        name="latent_neighbourhood_attention",
    )(q, k, v, cache_k, cache_v, bias_tab)


def _merge_body(alpha, x_ref, oa_ref, ob_ref, oc_ref, mod_ref, wg_ref, bg_ref, wbr_ref, wo_ref,
                lng_ref, lnb_ref, out_ref):
    x = x_ref[...]
    sh1 = mod_ref[:, 0:D_MODEL]
    sc1 = mod_ref[:, D_MODEL:2 * D_MODEL]
    g1 = mod_ref[:, 2 * D_MODEL:3 * D_MODEL]
    h = (x * (1.0 + sc1) + sh1).astype(BF16)
    acc = None
    for i, o_ref in enumerate((oa_ref, ob_ref, oc_ref)):
        cols = slice(i * D_MODEL, (i + 1) * D_MODEL)
        gate = jax.nn.sigmoid(_dot(h, wg_ref[:, cols]) + bg_ref[:, cols])
        term = gate * _dot(o_ref[...], wbr_ref[i])
        acc = term if acc is None else acc + term
    y = _dot(acc.astype(BF16), wo_ref[...])
    out_ref[...] = _layernorm(alpha * x + g1 * y, lng_ref[...], lnb_ref[...])


def _merge(alpha, x, oa, ob, oc, mod_l, row_of_tile, wg, bg, wbr, wo, lng, lnb):
    n_tok = x.shape[0]
    const = lambda i: (0, 0)
    tile = lambda width: pl.BlockSpec((TM, width), lambda i: (i, 0))
    return pl.pallas_call(
        functools.partial(_merge_body, alpha),
        out_shape=jax.ShapeDtypeStruct((n_tok, D_MODEL), F32),
        grid=(n_tok // TM,),
        in_specs=[
            tile(D_MODEL), tile(BRANCH_W), tile(BRANCH_W), tile(BRANCH_W),
            pl.BlockSpec((None, 1, 6 * D_MODEL), lambda i: (row_of_tile(i), 0, 0)),
            pl.BlockSpec((D_MODEL, 3 * D_MODEL), const),
            pl.BlockSpec((1, 3 * D_MODEL), const),
            pl.BlockSpec((3, BRANCH_W, D_MODEL), lambda i: (0, 0, 0)),
            pl.BlockSpec((D_MODEL, D_MODEL), const),
            pl.BlockSpec((1, D_MODEL), const),
            pl.BlockSpec((1, D_MODEL), const),
        ],
        out_specs=tile(D_MODEL),
        compiler_params=_params(1),
        name="branch_merge",
    )(x, oa, ob, oc, mod_l, wg, bg, wbr, wo, lng, lnb)


def _routing(logits, bias):
    scores = jax.nn.sigmoid(logits)
    sel_all = scores + bias
    sel = [sel_all[e:e + 1] for e in range(N_EXPERTS)]
    gscore = []
    for g in range(N_GROUPS):
        a, b, c, d = sel[4 * g:4 * g + 4]
        hi1, lo1 = jnp.maximum(a, b), jnp.minimum(a, b)
        hi2, lo2 = jnp.maximum(c, d), jnp.minimum(c, d)
        gscore.append(jnp.maximum(hi1, hi2) + jnp.maximum(jnp.minimum(hi1, hi2), jnp.maximum(lo1, lo2)))
    best = []
    taken = None
    for g in range(N_GROUPS):
        ok = functools.reduce(operator.and_, [gscore[g] >= gscore[k] for k in range(g + 1, N_GROUPS)],
                              jnp.ones_like(gscore[g], dtype=jnp.bool_))
        if taken is not None:
            ok = ok & jnp.logical_not(taken)
        best.append(ok)
        taken = ok if taken is None else (taken | ok)
    picked = []
    for e in range(N_EXPERTS):
        g, i = divmod(e, EXPERTS_PER_GROUP)
        rank = jnp.zeros_like(sel[0])
        for k in range(EXPERTS_PER_GROUP):
            if k == i:
                continue
            other = sel[4 * g + k]
            ahead = (other >= sel[e]) if k < i else (other > sel[e])
            rank = rank + jnp.where(ahead, 1.0, 0.0)
        picked.append((rank < 2.0) & best[g])
    return [picked[4 * g + i] & picked[4 * g + j] for g in range(N_GROUPS) for i, j in GROUP_PAIRS]


def _split_bf16(v):
    hi = v.astype(BF16)
    return hi, (v - hi.astype(F32)).astype(BF16)


def _route_body(x_ref, mod_ref, wrt_ref, rb_ref, tri_ref, cnt_in_ref, h2_ref, meta_ref, cnt_ref):
    @pl.when(pl.program_id(0) == 0)
    def _():
        cnt_ref[...] = cnt_in_ref[...]

    x = x_ref[...]
    sh2 = mod_ref[:, 3 * D_MODEL:4 * D_MODEL]
    sc2 = mod_ref[:, 4 * D_MODEL:5 * D_MODEL]
    h2 = x * (1.0 + sc2) + sh2
    h2_ref[...] = h2
    hi, lo = _split_bf16(h2)
    wr_hi, wr_lo = _split_bf16(wrt_ref[...])
    logits = _dot_nt(wr_hi, hi) + _dot_nt(wr_hi, lo) + _dot_nt(wr_lo, hi)
    masks = _routing(logits, rb_ref[...])
    tm = x.shape[0]
    onehot = jnp.concatenate([jnp.where(m, 1.0, 0.0) for m in masks]
                             + [jnp.zeros((BUCKET_ROWS - N_BUCKETS, tm), F32)], axis=0)
    ahead = _dot(onehot.astype(BF16), tri_ref[...]) + cnt_ref[:, 0:1]
    rank = jnp.sum(onehot * ahead, axis=0, keepdims=True)
    bucket_id = lax.broadcasted_iota(jnp.int32, onehot.shape, 0).astype(F32)
    bucket = jnp.sum(onehot * bucket_id, axis=0, keepdims=True)
    meta_ref[...] = jnp.concatenate([bucket, rank, jnp.zeros((6, tm), F32)], axis=0).astype(jnp.int32)
    cnt_ref[...] = cnt_ref[...] + jnp.sum(onehot, axis=1, keepdims=True)


def _route(x, mod_l, row_of_tile, wrt, rb, tri, cnt_in):
    n_tok = x.shape[0]
    const = lambda i: (0, 0)
    tile = pl.BlockSpec((TM, D_MODEL), lambda i: (i, 0))
    return pl.pallas_call(
        _route_body,
        out_shape=[jax.ShapeDtypeStruct((n_tok, D_MODEL), F32),
                   jax.ShapeDtypeStruct((8, n_tok), jnp.int32),
                   jax.ShapeDtypeStruct((BUCKET_ROWS, LANES), F32)],
        grid=(n_tok // TM,),
        in_specs=[
            tile,
            pl.BlockSpec((None, 1, 6 * D_MODEL), lambda i: (row_of_tile(i), 0, 0)),
            pl.BlockSpec((N_EXPERTS, D_MODEL), const),
            pl.BlockSpec((N_EXPERTS, 1), const),
            pl.BlockSpec((TM, TM), const),
            pl.BlockSpec((BUCKET_ROWS, LANES), const),
        ],
        out_specs=[tile, pl.BlockSpec((8, TM), lambda i: (0, i)), pl.BlockSpec((BUCKET_ROWS, LANES), const)],
        compiler_params=pltpu.CompilerParams(dimension_semantics=("arbitrary",), vmem_limit_bytes=VMEM_LIMIT),
        name="moe_route",
    )(x, mod_l, wrt, rb, tri, cnt_in)


def _row_copy(src_ref, src_row, dst_ref, dst_row, sem):
    return pltpu.make_async_copy(src_ref.at[pl.ds(src_row, 1)], dst_ref.at[pl.ds(dst_row, 1)], sem)


def _dest_row(base_ref, bucket_ref, rank_ref, token):
    return base_ref[bucket_ref[token]] + rank_ref[token]


def _scatter_body(base_ref, bucket_ref, rank_ref, h_ref, xs_in_ref, xs_ref, sem):
    del xs_in_ref
    first = pl.program_id(0) * TM

    def issue(t, carry):
        _row_copy(h_ref, t, xs_ref, _dest_row(base_ref, bucket_ref, rank_ref, first + t), sem).start()
        return carry

    def drain(t, carry):
        _row_copy(h_ref, t, xs_ref, 0, sem).wait()
        return carry

    lax.fori_loop(0, TM, issue, 0, unroll=8)
    lax.fori_loop(0, TM, drain, 0, unroll=8)


def _scatter(base, bucket, rank, h2, xs):
    n_tok = h2.shape[0]
    return pl.pallas_call(
        _scatter_body,
        out_shape=jax.ShapeDtypeStruct(xs.shape, xs.dtype),
        grid_spec=pltpu.PrefetchScalarGridSpec(
            num_scalar_prefetch=3,
            grid=(n_tok // TM,),
            in_specs=[pl.BlockSpec((TM, D_MODEL), lambda i, *_: (i, 0)),
                      pl.BlockSpec(memory_space=pl.ANY)],
            out_specs=pl.BlockSpec(memory_space=pl.ANY),
            scratch_shapes=[pltpu.SemaphoreType.DMA(())],
        ),
        input_output_aliases={4: 0},
        compiler_params=pltpu.CompilerParams(dimension_semantics=("arbitrary",), vmem_limit_bytes=VMEM_LIMIT),
        name="moe_scatter",
    )(base, bucket, rank, h2, xs)


def _experts_body(e0_ref, e1_ref, used_ref, xs_ref, wrt_ref, w13a_ref, w2a_ref, w13b_ref, w2b_ref, y_ref):
    j = pl.program_id(0)

    @pl.when(j < used_ref[0])
    def _():
        hi, lo = _split_bf16(xs_ref[...])
        wr = jnp.concatenate([wrt_ref[pl.ds(e0_ref[j], 1), :], wrt_ref[pl.ds(e1_ref[j], 1), :],
                              jnp.zeros((6, D_MODEL), F32)], axis=0)
        wr_hi, wr_lo = _split_bf16(wr)
        logits = _dot_nt(hi, wr_hi) + _dot_nt(lo, wr_hi) + _dot_nt(hi, wr_lo)
        s = jax.nn.sigmoid(logits)
        s0 = s[:, 0:1]
        s1 = s[:, 1:2]
        inv = 1.0 / (s0 + s1)

        def ffn(w13_ref, w2_ref):
            h13 = _dot(hi, w13_ref[...])
            a = h13[:, :D_EXPERT]
            b = h13[:, D_EXPERT:]
            return _dot((a * jax.nn.sigmoid(a) * b).astype(BF16), w2_ref[...])

        y_ref[...] = (s0 * inv) * ffn(w13a_ref, w2a_ref) + (s1 * inv) * ffn(w13b_ref, w2b_ref)

    @pl.when(j >= used_ref[0])
    def _():
        y_ref[...] = jnp.zeros_like(y_ref)


def _experts(tile_e0, tile_e1, n_used, xs, wrt, w13, w2):
    n_rows = xs.shape[0]
    tile = pl.BlockSpec((TR, D_MODEL), lambda j, *_: (j, 0))
    return pl.pallas_call(
        _experts_body,
        out_shape=jax.ShapeDtypeStruct((n_rows, D_MODEL), F32),
        grid_spec=pltpu.PrefetchScalarGridSpec(
            num_scalar_prefetch=3,
            grid=(n_rows // TR,),
            in_specs=[
                tile,
                pl.BlockSpec((N_EXPERTS, D_MODEL), lambda j, *_: (0, 0)),
                pl.BlockSpec((None, D_MODEL, 2 * D_EXPERT), lambda j, e0, e1, u: (e0[j], 0, 0)),
                pl.BlockSpec((None, D_EXPERT, D_MODEL), lambda j, e0, e1, u: (e0[j], 0, 0)),
                pl.BlockSpec((None, D_MODEL, 2 * D_EXPERT), lambda j, e0, e1, u: (e1[j], 0, 0)),
                pl.BlockSpec((None, D_EXPERT, D_MODEL), lambda j, e0, e1, u: (e1[j], 0, 0)),
            ],
            out_specs=tile,
        ),
        compiler_params=pltpu.CompilerParams(dimension_semantics=("arbitrary",), vmem_limit_bytes=VMEM_LIMIT),
        name="moe_experts",
    )(tile_e0, tile_e1, n_used, xs, wrt, w13, w2, w13, w2)


def _gather_norm_body(alpha, base_ref, bucket_ref, rank_ref, x_ref, mod_ref, y_ref, lng_ref, lnb_ref,
                      out_ref, m_ref, sem):
    first = pl.program_id(0) * TM

    def issue(t, carry):
        _row_copy(y_ref, _dest_row(base_ref, bucket_ref, rank_ref, first + t), m_ref, t, sem).start()
        return carry

    def drain(t, carry):
        _row_copy(y_ref, 0, m_ref, t, sem).wait()
        return carry

    lax.fori_loop(0, TM, issue, 0, unroll=8)
    lax.fori_loop(0, TM, drain, 0, unroll=8)
    g2 = mod_ref[:, 5 * D_MODEL:6 * D_MODEL]
    out_ref[...] = _layernorm(alpha * x_ref[...] + g2 * m_ref[...], lng_ref[...], lnb_ref[...])


def _gather_norm(alpha, base, bucket, rank, x, mod_l, row_of_tile, y, lng, lnb):
    n_tok = x.shape[0]
    const = lambda i, *_: (0, 0)
    tile = pl.BlockSpec((TM, D_MODEL), lambda i, *_: (i, 0))
    return pl.pallas_call(
        functools.partial(_gather_norm_body, alpha),
        out_shape=jax.ShapeDtypeStruct((n_tok, D_MODEL), F32),
        grid_spec=pltpu.PrefetchScalarGridSpec(
            num_scalar_prefetch=3,
            grid=(n_tok // TM,),
            in_specs=[
                tile,
                pl.BlockSpec((None, 1, 6 * D_MODEL), lambda i, *_: (row_of_tile(i), 0, 0)),
                pl.BlockSpec(memory_space=pl.ANY),
                pl.BlockSpec((1, D_MODEL), const),
                pl.BlockSpec((1, D_MODEL), const),
            ],
            out_specs=tile,
            scratch_shapes=[pltpu.VMEM((TM, D_MODEL), F32), pltpu.SemaphoreType.DMA(())],
        ),
        compiler_params=pltpu.CompilerParams(dimension_semantics=("arbitrary",), vmem_limit_bytes=VMEM_LIMIT),
        name="moe_gather_norm",
    )(base, bucket, rank, x, mod_l, y, lng, lnb)


def _bucket_layout(counts):
    cnt = counts[:N_BUCKETS, 0].astype(jnp.int32)
    padded = (cnt + TR - 1) // TR * TR
    ends = jnp.cumsum(padded)
    base = jnp.zeros((BUCKET_ROWS,), jnp.int32).at[:N_BUCKETS].set(ends - padded)
    tile_start = jnp.arange(SORTED_ROWS // TR, dtype=jnp.int32) * TR
    tile_bucket = jnp.minimum(jnp.sum(tile_start[:, None] >= ends[None, :], axis=1), N_BUCKETS - 1)
    first = jnp.array([4 * g + i for g in range(N_GROUPS) for i, _ in GROUP_PAIRS], jnp.int32)
    second = jnp.array([4 * g + j for g in range(N_GROUPS) for _, j in GROUP_PAIRS], jnp.int32)
    return base, first[tile_bucket], second[tile_bucket], (ends[-1:] // TR).astype(jnp.int32)


def _rope_tables():
    t = jnp.arange(DEC_SEQ)
    row = (t // GRID_W).astype(F32)
    col = (t % GRID_W).astype(F32)
    n = HEAD_DIM // 4
    freqs = ROPE_THETA ** (-jnp.arange(n, dtype=F32) / n)
    ang = jnp.concatenate([row[:, None] * freqs, col[:, None] * freqs], axis=-1)
    cos = jnp.tile(jnp.repeat(jnp.cos(ang), 2, axis=-1), (1, LANES // HEAD_DIM))
    sin = jnp.tile(jnp.repeat(jnp.sin(ang), 2, axis=-1), (1, LANES // HEAD_DIM))
    even = (jnp.arange(LANES) % 2 == 0)[None, :]
    return cos, jnp.where(even, -sin, 0.0), jnp.where(even, 0.0, sin)


def _group_mean_matrix(width):
    g = jnp.arange(width) // HEAD_DIM
    return jnp.where(g[:, None] == g[None, :], 1.0 / HEAD_DIM, 0.0).astype(BF16)


def _neighbourhood_bias(rpb_l):
    cx = jnp.arange(GRID_W)
    col_start = jnp.clip(cx - NA_KW // 2, 0, GRID_W - NA_KW)
    kx = jnp.arange(GRID_W)
    inside = (kx[None, :] >= col_start[:, None]) & (kx[None, :] < col_start[:, None] + NA_KW)
    dx = jnp.clip(kx[None, :] - cx[:, None] + (NA_KW - 1), 0, 2 * NA_KW - 2)
    tab = jnp.where(inside[None, None], rpb_l[:, :, dx], NEG_BIAS)
    cat = jnp.stack([tab[:, c:c + NA_KH] for c in range(NA_KH)], axis=1)
    return cat.transpose(0, 1, 3, 2, 4).reshape(H_B, NA_KH, GRID_W, NA_KH * GRID_W)


def _gqa_column_order():
    g, n, d = jnp.meshgrid(jnp.arange(H_C // KV_C), jnp.arange(KV_C), jnp.arange(HEAD_DIM), indexing="ij")
    return (n * (H_C // KV_C) * HEAD_DIM + g * HEAD_DIM + d).reshape(-1)


def kernel(x_prompt, x_sample, c, cache_a_k, cache_a_v, cache_b_k, cache_b_v, cache_c_k, cache_c_v,
           c_ctx, w_mod, b_mod, w_in, w_gate, b_gate, lam, a_subln, rpb, c_qnorm, c_knorm,
           w_br, w_o, ln_g, ln_b, w_router, router_bias, moe_w1, moe_w3, moe_w2):
    alpha = (2.0 * DEPTH) ** 0.25
    cond = jnp.concatenate([c_ctx[None, :], c, jnp.zeros((N_COND - 1 - DEC_BATCH, D_MODEL), F32)], axis=0)
    mods = _modulation(cond, w_mod, b_mod).reshape(DEPTH, N_COND, 1, 6 * D_MODEL)

    rope_tabs = _rope_tables()
    g512 = _group_mean_matrix(512)
    g128 = _group_mean_matrix(128)
    perm = _gqa_column_order()
    wrt = w_router.T
    rb = router_bias.reshape(N_EXPERTS, 1)
    tok = jnp.arange(TM)
    tri = (tok[:, None] < tok[None, :]).astype(BF16)

    ck_a = cache_a_k.reshape(DEC_BATCH, DEPTH, PAST_LEN, 512)
    cv_a = cache_a_v.reshape(DEC_BATCH, DEPTH, PAST_LEN, 512)
    ck_b = cache_b_k.reshape(DEC_BATCH, DEPTH, PAST_LEN, 512)
    cv_b = cache_b_v.reshape(DEC_BATCH, DEPTH, PAST_LEN, 512)
    ck_c = cache_c_k.reshape(DEC_BATCH, DEPTH, PAST_LEN, 128)
    cv_c = cache_c_v.reshape(DEC_BATCH, DEPTH, PAST_LEN, 128)

    ctx_row = lambda i: 0
    lat_tiles = DEC_SEQ // TM
    lat_row = lambda i: 1 + i // lat_tiles

    xp = x_prompt.reshape(BATCH * SEQ, D_MODEL)
    xs = x_sample.reshape(DEC_BATCH * DEC_SEQ, D_MODEL)
    new_kv = []
    for l in range(DEPTH):
        w_in_l = w_in[l]
        w_in_l = jnp.concatenate([w_in_l[:, :OFF_QC], w_in_l[:, OFF_QC:OFF_KC][:, perm], w_in_l[:, OFF_KC:]],
                                 axis=1).astype(BF16)
        qn = jnp.tile(c_qnorm[l], 512 // HEAD_DIM)[None, :]
        kn = jnp.tile(c_knorm[l], 128 // HEAD_DIM)[None, :]
        wg = w_gate[l].astype(BF16)
        bg = b_gate[l][None, :]
        wbr = jnp.stack([w_br[l, 0], w_br[l, 1], w_br[l, 2][perm]], axis=0).astype(BF16)
        wo = w_o[l].astype(BF16)
        w13 = jnp.concatenate([moe_w1[l], moe_w3[l]], axis=-1).astype(BF16)
        w2 = moe_w2[l].astype(BF16)
        lam_l = lam[l]
        subln_l = a_subln[l][None, :]
        bias_tab = _neighbourhood_bias(rpb[l])
        mod_l = mods[l]

        q, ka, va, kb, vb, kc, vc = _in_proj(xp, mod_l, ctx_row, w_in_l, qn, kn, g512, g128, None)
        new_kv.append((ka, va, kb, vb, kc, vc))
        oa, ob, oc = _ctx_attention(l, q, ka, va, kb, vb, kc, vc, lam_l, subln_l)
        xp = _merge(alpha, xp, oa, ob, oc, mod_l, ctx_row, wg, bg, wbr, wo, ln_g[l, 0:1], ln_b[l, 0:1])

        q, k, v = _in_proj(xs, mod_l, lat_row, w_in_l, qn, kn, g512, g128, rope_tabs)
        oa = _lat_diff_attention(l, q, k, v, ck_a, cv_a, lam_l, subln_l)
        ob = _lat_nbr_attention(l, q, k, v, ck_b, cv_b, bias_tab)
        oc = _lat_gqa_attention(l, q, k, v, ck_c, cv_c)
        xs = _merge(alpha, xs, oa, ob, oc, mod_l, lat_row, wg, bg, wbr, wo, ln_g[l, 0:1], ln_b[l, 0:1])

        h2p, meta_p, counts = _route(xp, mod_l, ctx_row, wrt, rb, tri, jnp.zeros((BUCKET_ROWS, LANES), F32))
        h2s, meta_s, counts = _route(xs, mod_l, lat_row, wrt, rb, tri, counts)
        base, tile_e0, tile_e1, n_used = _bucket_layout(counts)
        rows = jnp.zeros((SORTED_ROWS, D_MODEL), F32)
        rows = _scatter(base, meta_p[0], meta_p[1], h2p, rows)
        rows = _scatter(base, meta_s[0], meta_s[1], h2s, rows)
        y = _experts(tile_e0, tile_e1, n_used, rows, wrt, w13, w2)
        xp = _gather_norm(alpha, base, meta_p[0], meta_p[1], xp, mod_l, ctx_row, y, ln_g[l, 1:2], ln_b[l, 1:2])
        xs = _gather_norm(alpha, base, meta_s[0], meta_s[1], xs, mod_l, lat_row, y, ln_g[l, 1:2], ln_b[l, 1:2])

    def cache(idx, shape):
        return jnp.stack([new_kv[l][idx].reshape(BATCH, SEQ, *shape) for l in range(DEPTH)], axis=1)

    return (xp.reshape(BATCH, SEQ, D_MODEL), xs.reshape(DEC_BATCH, DEC_SEQ, D_MODEL),
            cache(0, (2, H_A, HEAD_DIM)), cache(1, (H_A, 2 * HEAD_DIM)),
            cache(2, (H_B, HEAD_DIM)), cache(3, (H_B, HEAD_DIM)),
            cache(4, (KV_C, HEAD_DIM)), cache(5, (KV_C, HEAD_DIM)))
```

```python
import functools
import math
import operator

import jax
import jax.numpy as jnp
from jax import lax
from jax.experimental import pallas as pl
from jax.experimental.pallas import tpu as pltpu

F32 = jnp.float32
BF16 = jnp.bfloat16

D_MODEL = 1024
BATCH = 32
SEQ = 256
DEPTH = 2
DEC_BATCH = 4
DEC_SEQ = 4096
PAST_LEN = 256
GRID_W = 64
GRID_H = DEC_SEQ // GRID_W
HEAD_DIM = 64
H_A = 4
H_B = 8
H_C = 8
KV_C = 2
NA_KH = 8
NA_KW = 16
ROPE_THETA = 10000.0
N_EXPERTS = 16
N_GROUPS = 4
EXPERTS_PER_GROUP = N_EXPERTS // N_GROUPS
D_EXPERT = 256
BRANCH_W = 512
LN_EPS = 1e-5
RMS_EPS = 1e-6
LOG2E = math.log2(math.e)
QK_SCALE = HEAD_DIM ** -0.5 * LOG2E
NEG_BIAS = -1e30
NBR_ROWS = 4
NBR_KEY_ROWS = 12

LANES = 128
N_COND = 8
VMEM_LIMIT = 56 * 1024 * 1024

OFF_QA, OFF_KA, OFF_VA, OFF_QB, OFF_KB, OFF_VB, OFF_QC, OFF_KC, OFF_VC = (
    0, 512, 1024, 1536, 2048, 2560, 3072, 3584, 3712)
IN_WIDTH = 3840
Q_PACK = 1536
KV_PACK = 1152

TM = 256
TR = 256
GROUP_PAIRS = tuple((i, j) for i in range(EXPERTS_PER_GROUP) for j in range(i + 1, EXPERTS_PER_GROUP))
N_BUCKETS = N_GROUPS * len(GROUP_PAIRS)
BUCKET_ROWS = 32
N_TOKENS = BATCH * SEQ + DEC_BATCH * DEC_SEQ
SORTED_ROWS = N_TOKENS + N_BUCKETS * TR
TQ_A = 128
TQ_C = 256


def _params(n_axes):
    return pltpu.CompilerParams(dimension_semantics=("parallel",) * n_axes,
                                vmem_limit_bytes=VMEM_LIMIT)


def _dot(a, b):
    return jnp.dot(a, b, preferred_element_type=F32)


def _dot_nt(a, b):
    return lax.dot_general(a, b, (((1,), (1,)), ((), ())), preferred_element_type=F32)


def _layernorm(z, g, b):
    mu = jnp.mean(z, axis=-1, keepdims=True)
    zc = z - mu
    var = jnp.mean(zc * zc, axis=-1, keepdims=True)
    return zc * lax.rsqrt(var + LN_EPS) * g + b


def _mod_body(c_ref, w_ref, b_ref, o_ref):
    c = c_ref[...]
    s = (c * jax.nn.sigmoid(c)).astype(BF16)
    o_ref[...] = _dot(s, w_ref[...].astype(BF16)) + b_ref[...]


def _modulation(cond, w_mod, b_mod):
    tn = 1536
    return pl.pallas_call(
        _mod_body,
        out_shape=jax.ShapeDtypeStruct((DEPTH, N_COND, 6 * D_MODEL), F32),
        grid=(DEPTH, 6 * D_MODEL // tn),
        in_specs=[
            pl.BlockSpec((N_COND, D_MODEL), lambda l, j: (0, 0)),
            pl.BlockSpec((None, D_MODEL, tn), lambda l, j: (l, 0, j)),
            pl.BlockSpec((None, 1, tn), lambda l, j: (l, 0, j)),
        ],
        out_specs=pl.BlockSpec((None, N_COND, tn), lambda l, j: (l, 0, j)),
        compiler_params=_params(2),
        name="modulation",
    )(cond, w_mod, b_mod.reshape(DEPTH, 1, 6 * D_MODEL))


def _in_proj_body(rope, x_ref, mod_ref, w_ref, qn_ref, kn_ref, g512_ref, g128_ref, *rest):
    if rope:
        ca_ref, cb_ref, cc_ref = rest[:3]
        outs = rest[3:]
    else:
        outs = rest
    x = x_ref[...]
    sh1 = mod_ref[:, 0:D_MODEL]
    sc1 = mod_ref[:, D_MODEL:2 * D_MODEL]
    h = (x * (1.0 + sc1) + sh1).astype(BF16)

    def proj(off, width):
        return _dot(h, w_ref[:, off:off + width])

    def rms(t, g_ref, wn_ref):
        t2 = t * t
        hi = t2.astype(BF16)
        lo = (t2 - hi.astype(F32)).astype(BF16)
        ms = _dot(hi, g_ref[...]) + _dot(lo, g_ref[...])
        return t * lax.rsqrt(ms + RMS_EPS) * wn_ref[...]

    def rot(t):
        if not rope:
            return t
        ca, cb, cc = ca_ref[...], cb_ref[...], cc_ref[...]
        pieces = []
        for j in range(t.shape[1] // LANES):
            blk = t[:, j * LANES:(j + 1) * LANES]
            pieces.append(blk * ca + pltpu.roll(blk, LANES - 1, 1) * cb + pltpu.roll(blk, 1, 1) * cc)
        return pieces[0] if len(pieces) == 1 else jnp.concatenate(pieces, axis=1)

    qa = rot(proj(OFF_QA, 512)) * QK_SCALE
    ka = rot(proj(OFF_KA, 512))
    va = proj(OFF_VA, 512)
    qb = proj(OFF_QB, 512) * QK_SCALE
    kb = proj(OFF_KB, 512)
    vb = proj(OFF_VB, 512)
    qc = rot(rms(proj(OFF_QC, 512), g512_ref, qn_ref)) * QK_SCALE
    kc = rot(rms(proj(OFF_KC, 128), g128_ref, kn_ref))
    vc = proj(OFF_VC, 128)

    q_ref = outs[0]
    q_ref[:, 0:512] = qa.astype(BF16)
    q_ref[:, 512:1024] = qb.astype(BF16)
    q_ref[:, 1024:1536] = qc.astype(BF16)
    if rope:
        k_ref, v_ref = outs[1:]
        k_ref[:, 0:512] = ka.astype(BF16)
        k_ref[:, 512:1024] = kb.astype(BF16)
        k_ref[:, 1024:1152] = kc.astype(BF16)
        v_ref[:, 0:512] = va.astype(BF16)
        v_ref[:, 512:1024] = vb.astype(BF16)
        v_ref[:, 1024:1152] = vc.astype(BF16)
    else:
        ka_ref, va_ref, kb_ref, vb_ref, kc_ref, vc_ref = outs[1:]
        ka_ref[...] = ka
        va_ref[...] = va
        kb_ref[...] = kb
        vb_ref[...] = vb
        kc_ref[...] = kc
        vc_ref[...] = vc


def _in_proj(x, mod_l, row_of_tile, w_bf, qn, kn, g512, g128, rope_tabs):
    n_tok = x.shape[0]
    n_tiles = n_tok // TM
    rope = rope_tabs is not None
    const = lambda i: (0, 0)
    in_specs = [
        pl.BlockSpec((TM, D_MODEL), lambda i: (i, 0)),
        pl.BlockSpec((None, 1, 6 * D_MODEL), lambda i: (row_of_tile(i), 0, 0)),
        pl.BlockSpec((D_MODEL, IN_WIDTH), const),
        pl.BlockSpec((1, 512), const),
        pl.BlockSpec((1, 128), const),
        pl.BlockSpec((512, 512), const),
        pl.BlockSpec((128, 128), const),
    ]
    args = [x, mod_l, w_bf, qn, kn, g512, g128]
    tile = lambda width: pl.BlockSpec((TM, width), lambda i: (i, 0))
    if rope:
        tiles_per_seq = DEC_SEQ // TM
        in_specs += [pl.BlockSpec((TM, LANES), lambda i: (i % tiles_per_seq, 0))] * 3
        args += list(rope_tabs)
        out_shape = [jax.ShapeDtypeStruct((n_tok, Q_PACK), BF16),
                     jax.ShapeDtypeStruct((n_tok, KV_PACK), BF16),
                     jax.ShapeDtypeStruct((n_tok, KV_PACK), BF16)]
        out_specs = [tile(Q_PACK), tile(KV_PACK), tile(KV_PACK)]
    else:
        widths = (512, 512, 512, 512, 128, 128)
        out_shape = [jax.ShapeDtypeStruct((n_tok, Q_PACK), BF16)] + [
            jax.ShapeDtypeStruct((n_tok, w), F32) for w in widths]
        out_specs = [tile(Q_PACK)] + [tile(w) for w in widths]
    return pl.pallas_call(
        functools.partial(_in_proj_body, rope),
        out_shape=out_shape,
        grid=(n_tiles,),
        in_specs=in_specs,
        out_specs=out_specs,
        compiler_params=_params(1),
        name="in_proj_latent" if rope else "in_proj_context",
    )(*args)


def _head_scores(q_pair, half, k_pair):
    lane = lax.broadcasted_iota(jnp.int32, q_pair.shape, 1)
    keep = (lane < HEAD_DIM) if half == 0 else (lane >= HEAD_DIM)
    qm = jnp.where(keep, q_pair, jnp.zeros_like(q_pair))
    return _dot_nt(qm, k_pair)


def _softmax_parts(s_list):
    mx = functools.reduce(jnp.maximum, [jnp.max(s, axis=-1, keepdims=True) for s in s_list])
    e_list = [jnp.exp2(s - mx) for s in s_list]
    den = functools.reduce(operator.add, [jnp.sum(e, axis=-1, keepdims=True) for e in e_list])
    return e_list, 1.0 / den


def _pair_attention(q_pair, segs):
    outs = []
    for half in (0, 1):
        s_list = []
        for k_pair, _, bias in segs:
            s = _head_scores(q_pair, half, k_pair)
            if bias is not None:
                s = s + bias(half)
            s_list.append(s)
        e_list, inv = _softmax_parts(s_list)
        o = functools.reduce(operator.add, [_dot(e.astype(BF16), seg[1]) for e, seg in zip(e_list, segs)])
        outs.append(o * inv)
    lane = lax.broadcasted_iota(jnp.int32, outs[0].shape, 1)
    return jnp.where(lane < HEAD_DIM, outs[0], outs[1])


def _lambda(lam_ref, layer):
    lam_init = 0.8 - 0.6 * math.exp(-0.3 * layer)
    l = lam_ref[...]
    a = jnp.sum(l[0:1] * l[1:2], axis=-1, keepdims=True)
    b = jnp.sum(l[2:3] * l[3:4], axis=-1, keepdims=True)
    return jnp.exp(a) - jnp.exp(b) + lam_init, lam_init


def _diff_attention(q_blk, k_segs, v_segs, lam_v, lam_init, subln):
    outs = []
    for h in range(H_A):
        j, half = h // 2, h % 2
        parts = []
        for m in range(2):
            qb = q_blk(2 * m + j)
            s_list = [_head_scores(qb, half, ks(2 * m + j)) for ks in k_segs]
            parts.append(_softmax_parts(s_list))
        (e1, inv1), (e2, inv2) = parts
        c2 = lam_v * inv2
        o = functools.reduce(operator.add, [
            _dot((e1[s] * inv1 - e2[s] * c2).astype(BF16), v_segs[s](h)) for s in range(len(k_segs))])
        ms = jnp.mean(o * o, axis=-1, keepdims=True)
        outs.append(o * lax.rsqrt(ms + RMS_EPS) * subln * (1.0 - lam_init))
    return outs


def _tile(ref, j, dtype=None):
    t = ref[:, j * LANES:(j + 1) * LANES]
    return t if dtype is None else t.astype(dtype)


def _ctx_attn_body(layer, q_ref, ka_ref, va_ref, kb_ref, vb_ref, kc_ref, vc_ref, lam_ref, subln_ref,
                   oa_ref, ob_ref, oc_ref):
    lam_v, lam_init = _lambda(lam_ref, layer)
    oa = _diff_attention(lambda j: _tile(q_ref, j),
                         [lambda j: _tile(ka_ref, j, BF16)],
                         [lambda h: _tile(va_ref, h, BF16)],
                         lam_v, lam_init, subln_ref[...])
    for h in range(H_A):
        oa_ref[:, h * LANES:(h + 1) * LANES] = oa[h].astype(BF16)
    for j in range(4):
        o = _pair_attention(_tile(q_ref, 4 + j), [(_tile(kb_ref, j, BF16), _tile(vb_ref, j, BF16), None)])
        ob_ref[:, j * LANES:(j + 1) * LANES] = o.astype(BF16)
    kc = kc_ref[...].astype(BF16)
    vc = vc_ref[...].astype(BF16)
    for g in range(4):
        o = _pair_attention(_tile(q_ref, 8 + g), [(kc, vc, None)])
        oc_ref[:, g * LANES:(g + 1) * LANES] = o.astype(BF16)


def _ctx_attention(layer, q, ka, va, kb, vb, kc, vc, lam_l, subln_l):
    n_tok = q.shape[0]
    tile = lambda width: pl.BlockSpec((SEQ, width), lambda b: (b, 0))
    const = lambda b: (0, 0)
    return pl.pallas_call(
        functools.partial(_ctx_attn_body, layer),
        out_shape=[jax.ShapeDtypeStruct((n_tok, BRANCH_W), BF16)] * 3,
        grid=(n_tok // SEQ,),
        in_specs=[tile(Q_PACK), tile(512), tile(512), tile(512), tile(512), tile(128), tile(128),
                  pl.BlockSpec((4, HEAD_DIM), const), pl.BlockSpec((1, 2 * HEAD_DIM), const)],
        out_specs=[tile(BRANCH_W)] * 3,
        compiler_params=_params(1),
        name="context_attention",
    )(q, ka, va, kb, vb, kc, vc, lam_l, subln_l)


def _lat_diff_body(layer, q_ref, kl_ref, vl_ref, kc_ref, vc_ref, lam_ref, subln_ref, o_ref):
    lam_v, lam_init = _lambda(lam_ref, layer)
    oa = _diff_attention(lambda j: _tile(q_ref, j),
                         [lambda j: _tile(kc_ref, j, BF16), lambda j: _tile(kl_ref, j)],
                         [lambda h: _tile(vc_ref, h, BF16), lambda h: _tile(vl_ref, h)],
                         lam_v, lam_init, subln_ref[...])
    for h in range(H_A):
        o_ref[:, h * LANES:(h + 1) * LANES] = oa[h].astype(BF16)


def _lat_diff_attention(layer, q, k, v, cache_k, cache_v, lam_l, subln_l):
    nq = DEC_SEQ // TQ_A
    const = lambda b, i: (0, 0)
    return pl.pallas_call(
        functools.partial(_lat_diff_body, layer),
        out_shape=jax.ShapeDtypeStruct((DEC_BATCH * DEC_SEQ, BRANCH_W), BF16),
        grid=(DEC_BATCH, nq),
        in_specs=[
            pl.BlockSpec((TQ_A, 512), lambda b, i: (b * nq + i, 0)),
            pl.BlockSpec((DEC_SEQ, 512), lambda b, i: (b, 0)),
            pl.BlockSpec((DEC_SEQ, 512), lambda b, i: (b, 0)),
            pl.BlockSpec((None, None, PAST_LEN, 512), lambda b, i: (b, layer, 0, 0)),
            pl.BlockSpec((None, None, PAST_LEN, 512), lambda b, i: (b, layer, 0, 0)),
            pl.BlockSpec((4, HEAD_DIM), const),
            pl.BlockSpec((1, 2 * HEAD_DIM), const),
        ],
        out_specs=pl.BlockSpec((TQ_A, BRANCH_W), lambda b, i: (b * nq + i, 0)),
        compiler_params=_params(2),
        name="latent_diff_attention",
    )(q, k, v, cache_k, cache_v, lam_l, subln_l)


def _lat_gqa_body(q_ref, kl_ref, vl_ref, kc_ref, vc_ref, o_ref):
    segs = [(kc_ref[...].astype(BF16), vc_ref[...].astype(BF16), None),
            (kl_ref[...], vl_ref[...], None)]
    for g in range(4):
        o = _pair_attention(_tile(q_ref, g), segs)
        o_ref[:, g * LANES:(g + 1) * LANES] = o.astype(BF16)


def _lat_gqa_attention(layer, q, k, v, cache_k, cache_v):
    nq = DEC_SEQ // TQ_C
    return pl.pallas_call(
        _lat_gqa_body,
        out_shape=jax.ShapeDtypeStruct((DEC_BATCH * DEC_SEQ, BRANCH_W), BF16),
        grid=(DEC_BATCH, nq),
        in_specs=[
            pl.BlockSpec((TQ_C, 512), lambda b, i: (b * nq + i, 2)),
            pl.BlockSpec((DEC_SEQ, LANES), lambda b, i: (b, 1024 // LANES)),
            pl.BlockSpec((DEC_SEQ, LANES), lambda b, i: (b, 1024 // LANES)),
            pl.BlockSpec((None, None, PAST_LEN, LANES), lambda b, i: (b, layer, 0, 0)),
            pl.BlockSpec((None, None, PAST_LEN, LANES), lambda b, i: (b, layer, 0, 0)),
        ],
        out_specs=pl.BlockSpec((TQ_C, BRANCH_W), lambda b, i: (b * nq + i, 0)),
        compiler_params=_params(2),
        name="latent_gqa_attention",
    )(q, k, v, cache_k, cache_v)


def _nbr_key_start(g):
    return jnp.clip(g * NBR_ROWS - NA_KH // 2, 0, GRID_H - NBR_KEY_ROWS)


def _lat_nbr_body(q_ref, kl_ref, vl_ref, kc_ref, vc_ref, bias_ref, o_ref):
    base = pl.multiple_of(_nbr_key_start(pl.program_id(1)) * GRID_W, GRID_W)
    n_loc = NBR_KEY_ROWS * GRID_W
    for j in range(4):
        k_loc = kl_ref[pl.ds(base, n_loc), j * LANES:(j + 1) * LANES]
        v_loc = vl_ref[pl.ds(base, n_loc), j * LANES:(j + 1) * LANES]
        segs = [(k_loc, v_loc, lambda half, j=j: bias_ref[2 * j + half]),
                (_tile(kc_ref, j, BF16), _tile(vc_ref, j, BF16), None)]
        o = _pair_attention(_tile(q_ref, j), segs)
        o_ref[:, j * LANES:(j + 1) * LANES] = o.astype(BF16)


def _lat_nbr_attention(layer, q, k, v, cache_k, cache_v, bias_tab):
    n_groups = GRID_H // NBR_ROWS
    tq = NBR_ROWS * GRID_W
    case = lambda g: jnp.where(g == 0, 0, jnp.where(g == n_groups - 1, 2, 1))
    return pl.pallas_call(
        _lat_nbr_body,
        out_shape=jax.ShapeDtypeStruct((DEC_BATCH * DEC_SEQ, BRANCH_W), BF16),
        grid=(DEC_BATCH, n_groups),
        in_specs=[
            pl.BlockSpec((tq, 512), lambda b, g: (b * n_groups + g, 1)),
            pl.BlockSpec((DEC_SEQ, 512), lambda b, g: (b, 1)),
            pl.BlockSpec((DEC_SEQ, 512), lambda b, g: (b, 1)),
            pl.BlockSpec((None, None, PAST_LEN, 512), lambda b, g: (b, layer, 0, 0)),
            pl.BlockSpec((None, None, PAST_LEN, 512), lambda b, g: (b, layer, 0, 0)),
            pl.BlockSpec((None, H_B, tq, NBR_KEY_ROWS * GRID_W), lambda b, g: (case(g), 0, 0, 0)),
        ],
        out_specs=pl.BlockSpec((tq, BRANCH_W), lambda b, g: (b * n_groups + g, 0)),
        compiler_params=_params(2),
        name="latent_neighbourhood_attention",
    )(q, k, v, cache_k, cache_v, bias_tab)


def _merge_body(alpha, x_ref, oa_ref, ob_ref, oc_ref, mod_ref, wg_ref, bg_ref, wbr_ref, wo_ref,
                lng_ref, lnb_ref, out_ref):
    x = x_ref[...]
    sh1 = mod_ref[:, 0:D_MODEL]
    sc1 = mod_ref[:, D_MODEL:2 * D_MODEL]
    g1 = mod_ref[:, 2 * D_MODEL:3 * D_MODEL]
    h = (x * (1.0 + sc1) + sh1).astype(BF16)
    acc = None
    for i, o_ref in enumerate((oa_ref, ob_ref, oc_ref)):
        cols = slice(i * D_MODEL, (i + 1) * D_MODEL)
        gate = jax.nn.sigmoid(_dot(h, wg_ref[:, cols]) + bg_ref[:, cols])
        term = gate * _dot(o_ref[...], wbr_ref[i])
        acc = term if acc is None else acc + term
    y = _dot(acc.astype(BF16), wo_ref[...])
    out_ref[...] = _layernorm(alpha * x + g1 * y, lng_ref[...], lnb_ref[...])


def _merge(alpha, x, oa, ob, oc, mod_l, row_of_tile, wg, bg, wbr, wo, lng, lnb):
    n_tok = x.shape[0]
    const = lambda i: (0, 0)
    tile = lambda width: pl.BlockSpec((TM, width), lambda i: (i, 0))
    return pl.pallas_call(
        functools.partial(_merge_body, alpha),
        out_shape=jax.ShapeDtypeStruct((n_tok, D_MODEL), F32),
        grid=(n_tok // TM,),
        in_specs=[
            tile(D_MODEL), tile(BRANCH_W), tile(BRANCH_W), tile(BRANCH_W),
            pl.BlockSpec((None, 1, 6 * D_MODEL), lambda i: (row_of_tile(i), 0, 0)),
            pl.BlockSpec((D_MODEL, 3 * D_MODEL), const),
            pl.BlockSpec((1, 3 * D_MODEL), const),
            pl.BlockSpec((3, BRANCH_W, D_MODEL), lambda i: (0, 0, 0)),
            pl.BlockSpec((D_MODEL, D_MODEL), const),
            pl.BlockSpec((1, D_MODEL), const),
            pl.BlockSpec((1, D_MODEL), const),
        ],
        out_specs=tile(D_MODEL),
        compiler_params=_params(1),
        name="branch_merge",
    )(x, oa, ob, oc, mod_l, wg, bg, wbr, wo, lng, lnb)


def _routing(logits, bias):
    scores = jax.nn.sigmoid(logits)
    sel_all = scores + bias
    sel = [sel_all[e:e + 1] for e in range(N_EXPERTS)]
    gscore = []
    for g in range(N_GROUPS):
        a, b, c, d = sel[4 * g:4 * g + 4]
        hi1, lo1 = jnp.maximum(a, b), jnp.minimum(a, b)
        hi2, lo2 = jnp.maximum(c, d), jnp.minimum(c, d)
        gscore.append(jnp.maximum(hi1, hi2) + jnp.maximum(jnp.minimum(hi1, hi2), jnp.maximum(lo1, lo2)))
    best = []
    taken = None
    for g in range(N_GROUPS):
        ok = functools.reduce(operator.and_, [gscore[g] >= gscore[k] for k in range(g + 1, N_GROUPS)],
                              jnp.ones_like(gscore[g], dtype=jnp.bool_))
        if taken is not None:
            ok = ok & jnp.logical_not(taken)
        best.append(ok)
        taken = ok if taken is None else (taken | ok)
    picked = []
    for e in range(N_EXPERTS):
        g, i = divmod(e, EXPERTS_PER_GROUP)
        rank = jnp.zeros_like(sel[0])
        for k in range(EXPERTS_PER_GROUP):
            if k == i:
                continue
            other = sel[4 * g + k]
            ahead = (other >= sel[e]) if k < i else (other > sel[e])
            rank = rank + jnp.where(ahead, 1.0, 0.0)
        picked.append((rank < 2.0) & best[g])
    return [picked[4 * g + i] & picked[4 * g + j] for g in range(N_GROUPS) for i, j in GROUP_PAIRS]


def _split_bf16(v):
    hi = v.astype(BF16)
    return hi, (v - hi.astype(F32)).astype(BF16)


def _route_body(x_ref, mod_ref, wrt_ref, rb_ref, tri_ref, cnt_in_ref, h2_ref, meta_ref, cnt_ref):
    @pl.when(pl.program_id(0) == 0)
    def _():
        cnt_ref[...] = cnt_in_ref[...]

    x = x_ref[...]
    sh2 = mod_ref[:, 3 * D_MODEL:4 * D_MODEL]
    sc2 = mod_ref[:, 4 * D_MODEL:5 * D_MODEL]
    h2 = x * (1.0 + sc2) + sh2
    h2_ref[...] = h2
    hi, lo = _split_bf16(h2)
    wr_hi, wr_lo = _split_bf16(wrt_ref[...])
    logits = _dot_nt(wr_hi, hi) + _dot_nt(wr_hi, lo) + _dot_nt(wr_lo, hi)
    masks = _routing(logits, rb_ref[...])
    tm = x.shape[0]
    onehot = jnp.concatenate([jnp.where(m, 1.0, 0.0) for m in masks]
                             + [jnp.zeros((BUCKET_ROWS - N_BUCKETS, tm), F32)], axis=0)
    ahead = _dot(onehot.astype(BF16), tri_ref[...]) + cnt_ref[:, 0:1]
    rank = jnp.sum(onehot * ahead, axis=0, keepdims=True)
    bucket_id = lax.broadcasted_iota(jnp.int32, onehot.shape, 0).astype(F32)
    bucket = jnp.sum(onehot * bucket_id, axis=0, keepdims=True)
    meta_ref[...] = jnp.concatenate([bucket, rank, jnp.zeros((6, tm), F32)], axis=0).astype(jnp.int32)
    cnt_ref[...] = cnt_ref[...] + jnp.sum(onehot, axis=1, keepdims=True)


def _route(x, mod_l, row_of_tile, wrt, rb, tri, cnt_in):
    n_tok = x.shape[0]
    const = lambda i: (0, 0)
    tile = pl.BlockSpec((TM, D_MODEL), lambda i: (i, 0))
    return pl.pallas_call(
        _route_body,
        out_shape=[jax.ShapeDtypeStruct((n_tok, D_MODEL), F32),
                   jax.ShapeDtypeStruct((8, n_tok), jnp.int32),
                   jax.ShapeDtypeStruct((BUCKET_ROWS, LANES), F32)],
        grid=(n_tok // TM,),
        in_specs=[
            tile,
            pl.BlockSpec((None, 1, 6 * D_MODEL), lambda i: (row_of_tile(i), 0, 0)),
            pl.BlockSpec((N_EXPERTS, D_MODEL), const),
            pl.BlockSpec((N_EXPERTS, 1), const),
            pl.BlockSpec((TM, TM), const),
            pl.BlockSpec((BUCKET_ROWS, LANES), const),
        ],
        out_specs=[tile, pl.BlockSpec((8, TM), lambda i: (0, i)), pl.BlockSpec((BUCKET_ROWS, LANES), const)],
        compiler_params=pltpu.CompilerParams(dimension_semantics=("arbitrary",), vmem_limit_bytes=VMEM_LIMIT),
        name="moe_route",
    )(x, mod_l, wrt, rb, tri, cnt_in)


def _row_copy(src_ref, src_row, dst_ref, dst_row, sem):
    return pltpu.make_async_copy(src_ref.at[pl.ds(src_row, 1)], dst_ref.at[pl.ds(dst_row, 1)], sem)


def _dest_row(base_ref, bucket_ref, rank_ref, token):
    return base_ref[bucket_ref[token]] + rank_ref[token]


def _scatter_body(base_ref, bucket_ref, rank_ref, h_ref, xs_in_ref, xs_ref, sem):
    del xs_in_ref
    first = pl.program_id(0) * TM

    def issue(t, carry):
        _row_copy(h_ref, t, xs_ref, _dest_row(base_ref, bucket_ref, rank_ref, first + t), sem).start()
        return carry

    def drain(t, carry):
        _row_copy(h_ref, t, xs_ref, 0, sem).wait()
        return carry

    lax.fori_loop(0, TM, issue, 0, unroll=8)
    lax.fori_loop(0, TM, drain, 0, unroll=8)


def _scatter(base, bucket, rank, h2, xs):
    n_tok = h2.shape[0]
    return pl.pallas_call(
        _scatter_body,
        out_shape=jax.ShapeDtypeStruct(xs.shape, xs.dtype),
        grid_spec=pltpu.PrefetchScalarGridSpec(
            num_scalar_prefetch=3,
            grid=(n_tok // TM,),
            in_specs=[pl.BlockSpec((TM, D_MODEL), lambda i, *_: (i, 0)),
                      pl.BlockSpec(memory_space=pl.ANY)],
            out_specs=pl.BlockSpec(memory_space=pl.ANY),
            scratch_shapes=[pltpu.SemaphoreType.DMA(())],
        ),
        input_output_aliases={4: 0},
        compiler_params=pltpu.CompilerParams(dimension_semantics=("arbitrary",), vmem_limit_bytes=VMEM_LIMIT),
        name="moe_scatter",
    )(base, bucket, rank, h2, xs)


def _experts_body(e0_ref, e1_ref, used_ref, xs_ref, wrt_ref, w13a_ref, w2a_ref, w13b_ref, w2b_ref, y_ref):
    j = pl.program_id(0)

    @pl.when(j < used_ref[0])
    def _():
        hi, lo = _split_bf16(xs_ref[...])
        wr = jnp.concatenate([wrt_ref[pl.ds(e0_ref[j], 1), :], wrt_ref[pl.ds(e1_ref[j], 1), :],
                              jnp.zeros((6, D_MODEL), F32)], axis=0)
        wr_hi, wr_lo = _split_bf16(wr)
        logits = _dot_nt(hi, wr_hi) + _dot_nt(lo, wr_hi) + _dot_nt(hi, wr_lo)
        s = jax.nn.sigmoid(logits)
        s0 = s[:, 0:1]
        s1 = s[:, 1:2]
        inv = 1.0 / (s0 + s1)

        def ffn(w13_ref, w2_ref):
            h13 = _dot(hi, w13_ref[...])
            a = h13[:, :D_EXPERT]
            b = h13[:, D_EXPERT:]
            return _dot((a * jax.nn.sigmoid(a) * b).astype(BF16), w2_ref[...])

        y_ref[...] = (s0 * inv) * ffn(w13a_ref, w2a_ref) + (s1 * inv) * ffn(w13b_ref, w2b_ref)

    @pl.when(j >= used_ref[0])
    def _():
        y_ref[...] = jnp.zeros_like(y_ref)


def _experts(tile_e0, tile_e1, n_used, xs, wrt, w13, w2):
    n_rows = xs.shape[0]
    tile = pl.BlockSpec((TR, D_MODEL), lambda j, *_: (j, 0))
    return pl.pallas_call(
        _experts_body,
        out_shape=jax.ShapeDtypeStruct((n_rows, D_MODEL), F32),
        grid_spec=pltpu.PrefetchScalarGridSpec(
            num_scalar_prefetch=3,
            grid=(n_rows // TR,),
            in_specs=[
                tile,
                pl.BlockSpec((N_EXPERTS, D_MODEL), lambda j, *_: (0, 0)),
                pl.BlockSpec((None, D_MODEL, 2 * D_EXPERT), lambda j, e0, e1, u: (e0[j], 0, 0)),
                pl.BlockSpec((None, D_EXPERT, D_MODEL), lambda j, e0, e1, u: (e0[j], 0, 0)),
                pl.BlockSpec((None, D_MODEL, 2 * D_EXPERT), lambda j, e0, e1, u: (e1[j], 0, 0)),
                pl.BlockSpec((None, D_EXPERT, D_MODEL), lambda j, e0, e1, u: (e1[j], 0, 0)),
            ],
            out_specs=tile,
        ),
        compiler_params=pltpu.CompilerParams(dimension_semantics=("arbitrary",), vmem_limit_bytes=VMEM_LIMIT),
        name="moe_experts",
    )(tile_e0, tile_e1, n_used, xs, wrt, w13, w2, w13, w2)


def _gather_norm_body(alpha, base_ref, bucket_ref, rank_ref, x_ref, mod_ref, y_ref, lng_ref, lnb_ref,
                      out_ref, m_ref, sem):
    first = pl.program_id(0) * TM

    def issue(t, carry):
        _row_copy(y_ref, _dest_row(base_ref, bucket_ref, rank_ref, first + t), m_ref, t, sem).start()
        return carry

    def drain(t, carry):
        _row_copy(y_ref, 0, m_ref, t, sem).wait()
        return carry

    lax.fori_loop(0, TM, issue, 0, unroll=8)
    lax.fori_loop(0, TM, drain, 0, unroll=8)
    g2 = mod_ref[:, 5 * D_MODEL:6 * D_MODEL]
    out_ref[...] = _layernorm(alpha * x_ref[...] + g2 * m_ref[...], lng_ref[...], lnb_ref[...])


def _gather_norm(alpha, base, bucket, rank, x, mod_l, row_of_tile, y, lng, lnb):
    n_tok = x.shape[0]
    const = lambda i, *_: (0, 0)
    tile = pl.BlockSpec((TM, D_MODEL), lambda i, *_: (i, 0))
    return pl.pallas_call(
        functools.partial(_gather_norm_body, alpha),
        out_shape=jax.ShapeDtypeStruct((n_tok, D_MODEL), F32),
        grid_spec=pltpu.PrefetchScalarGridSpec(
            num_scalar_prefetch=3,
            grid=(n_tok // TM,),
            in_specs=[
                tile,
                pl.BlockSpec((None, 1, 6 * D_MODEL), lambda i, *_: (row_of_tile(i), 0, 0)),
                pl.BlockSpec(memory_space=pl.ANY),
                pl.BlockSpec((1, D_MODEL), const),
                pl.BlockSpec((1, D_MODEL), const),
            ],
            out_specs=tile,
            scratch_shapes=[pltpu.VMEM((TM, D_MODEL), F32), pltpu.SemaphoreType.DMA(())],
        ),
        compiler_params=pltpu.CompilerParams(dimension_semantics=("arbitrary",), vmem_limit_bytes=VMEM_LIMIT),
        name="moe_gather_norm",
    )(base, bucket, rank, x, mod_l, y, lng, lnb)


def _bucket_layout(counts):
    cnt = counts[:N_BUCKETS, 0].astype(jnp.int32)
    padded = (cnt + TR - 1) // TR * TR
    ends = jnp.cumsum(padded)
    base = jnp.zeros((BUCKET_ROWS,), jnp.int32).at[:N_BUCKETS].set(ends - padded)
    tile_start = jnp.arange(SORTED_ROWS // TR, dtype=jnp.int32) * TR
    tile_bucket = jnp.minimum(jnp.sum(tile_start[:, None] >= ends[None, :], axis=1), N_BUCKETS - 1)
    first = jnp.array([4 * g + i for g in range(N_GROUPS) for i, _ in GROUP_PAIRS], jnp.int32)
    second = jnp.array([4 * g + j for g in range(N_GROUPS) for _, j in GROUP_PAIRS], jnp.int32)
    return base, first[tile_bucket], second[tile_bucket], (ends[-1:] // TR).astype(jnp.int32)


def _rope_tables():
    t = jnp.arange(DEC_SEQ)
    row = (t // GRID_W).astype(F32)
    col = (t % GRID_W).astype(F32)
    n = HEAD_DIM // 4
    freqs = ROPE_THETA ** (-jnp.arange(n, dtype=F32) / n)
    ang = jnp.concatenate([row[:, None] * freqs, col[:, None] * freqs], axis=-1)
    cos = jnp.tile(jnp.repeat(jnp.cos(ang), 2, axis=-1), (1, LANES // HEAD_DIM))
    sin = jnp.tile(jnp.repeat(jnp.sin(ang), 2, axis=-1), (1, LANES // HEAD_DIM))
    even = (jnp.arange(LANES) % 2 == 0)[None, :]
    return cos, jnp.where(even, -sin, 0.0), jnp.where(even, 0.0, sin)


def _group_mean_matrix(width):
    g = jnp.arange(width) // HEAD_DIM
    return jnp.where(g[:, None] == g[None, :], 1.0 / HEAD_DIM, 0.0).astype(BF16)


def _neighbourhood_bias(rpb_l):
    cx = jnp.arange(GRID_W)
    col_start = jnp.clip(cx - NA_KW // 2, 0, GRID_W - NA_KW)
    kx = jnp.arange(GRID_W)
    inside_col = (kx[None, :] >= col_start[:, None]) & (kx[None, :] < col_start[:, None] + NA_KW)
    dx = jnp.clip(kx[None, :] - cx[:, None] + (NA_KW - 1), 0, 2 * NA_KW - 2)
    n_groups = GRID_H // NBR_ROWS
    i = jnp.arange(NBR_KEY_ROWS)
    tabs = []
    for g in (0, 1, n_groups - 1):
        key_start = min(max(g * NBR_ROWS - NA_KH // 2, 0), GRID_H - NBR_KEY_ROWS)
        r = g * NBR_ROWS + jnp.arange(NBR_ROWS)
        row_start = jnp.clip(r - NA_KH // 2, 0, GRID_H - NA_KH)
        ky = key_start + i
        inside_row = (ky[None, :] >= row_start[:, None]) & (ky[None, :] < row_start[:, None] + NA_KH)
        dy = jnp.clip(ky[None, :] - r[:, None] + (NA_KH - 1), 0, 2 * NA_KH - 2)
        vals = rpb_l[:, dy[:, None, :, None], dx[None, :, None, :]]
        ok = inside_row[:, None, :, None] & inside_col[None, :, None, :]
        tabs.append(jnp.where(ok[None], vals * LOG2E, NEG_BIAS)
                    .reshape(H_B, NBR_ROWS * GRID_W, NBR_KEY_ROWS * GRID_W))
    return jnp.stack(tabs, axis=0)


def _gqa_regroup(w, axis):
    shape = w.shape
    w = w.reshape(shape[:axis] + (KV_C, H_C // KV_C, HEAD_DIM) + shape[axis + 1:])
    return jnp.swapaxes(w, axis, axis + 1).reshape(shape)


def kernel(x_prompt, x_sample, c, cache_a_k, cache_a_v, cache_b_k, cache_b_v, cache_c_k, cache_c_v,
           c_ctx, w_mod, b_mod, w_in, w_gate, b_gate, lam, a_subln, rpb, c_qnorm, c_knorm,
           w_br, w_o, ln_g, ln_b, w_router, router_bias, moe_w1, moe_w3, moe_w2):
    alpha = (2.0 * DEPTH) ** 0.25
    cond = jnp.concatenate([c_ctx[None, :], c, jnp.zeros((N_COND - 1 - DEC_BATCH, D_MODEL), F32)], axis=0)
    mods = _modulation(cond, w_mod, b_mod).reshape(DEPTH, N_COND, 1, 6 * D_MODEL)

    rope_tabs = _rope_tables()
    g512 = _group_mean_matrix(512)
    g128 = _group_mean_matrix(128)
    wrt = w_router.T
    rb = router_bias.reshape(N_EXPERTS, 1)
    tok = jnp.arange(TM)
    tri = (tok[:, None] < tok[None, :]).astype(BF16)

    ck_a = cache_a_k.reshape(DEC_BATCH, DEPTH, PAST_LEN, 512)
    cv_a = cache_a_v.reshape(DEC_BATCH, DEPTH, PAST_LEN, 512)
    ck_b = cache_b_k.reshape(DEC_BATCH, DEPTH, PAST_LEN, 512)
    cv_b = cache_b_v.reshape(DEC_BATCH, DEPTH, PAST_LEN, 512)
    ck_c = cache_c_k.reshape(DEC_BATCH, DEPTH, PAST_LEN, 128)
    cv_c = cache_c_v.reshape(DEC_BATCH, DEPTH, PAST_LEN, 128)

    ctx_row = lambda i: 0
    lat_tiles = DEC_SEQ // TM
    lat_row = lambda i: 1 + i // lat_tiles

    xp = x_prompt.reshape(BATCH * SEQ, D_MODEL)
    xs = x_sample.reshape(DEC_BATCH * DEC_SEQ, D_MODEL)
    new_kv = []
    for l in range(DEPTH):
        w_in_l = w_in[l]
        w_in_l = jnp.concatenate([w_in_l[:, :OFF_QC], _gqa_regroup(w_in_l[:, OFF_QC:OFF_KC], 1),
                                  w_in_l[:, OFF_KC:]], axis=1).astype(BF16)
        qn = jnp.tile(c_qnorm[l], 512 // HEAD_DIM)[None, :]
        kn = jnp.tile(c_knorm[l], 128 // HEAD_DIM)[None, :]
        wg = w_gate[l].astype(BF16)
        bg = b_gate[l][None, :]
        wbr = jnp.stack([w_br[l, 0], w_br[l, 1], _gqa_regroup(w_br[l, 2], 0)], axis=0).astype(BF16)
        wo = w_o[l].astype(BF16)
        w13 = jnp.concatenate([moe_w1[l], moe_w3[l]], axis=-1).astype(BF16)
        w2 = moe_w2[l].astype(BF16)
        lam_l = lam[l]
        subln_l = a_subln[l][None, :]
        bias_tab = _neighbourhood_bias(rpb[l])
        mod_l = mods[l]

        q, ka, va, kb, vb, kc, vc = _in_proj(xp, mod_l, ctx_row, w_in_l, qn, kn, g512, g128, None)
        new_kv.append((ka, va, kb, vb, kc, vc))
        oa, ob, oc = _ctx_attention(l, q, ka, va, kb, vb, kc, vc, lam_l, subln_l)
        xp = _merge(alpha, xp, oa, ob, oc, mod_l, ctx_row, wg, bg, wbr, wo, ln_g[l, 0:1], ln_b[l, 0:1])

        q, k, v = _in_proj(xs, mod_l, lat_row, w_in_l, qn, kn, g512, g128, rope_tabs)
        oa = _lat_diff_attention(l, q, k, v, ck_a, cv_a, lam_l, subln_l)
        ob = _lat_nbr_attention(l, q, k, v, ck_b, cv_b, bias_tab)
        oc = _lat_gqa_attention(l, q, k, v, ck_c, cv_c)
        xs = _merge(alpha, xs, oa, ob, oc, mod_l, lat_row, wg, bg, wbr, wo, ln_g[l, 0:1], ln_b[l, 0:1])

        h2p, meta_p, counts = _route(xp, mod_l, ctx_row, wrt, rb, tri, jnp.zeros((BUCKET_ROWS, LANES), F32))
        h2s, meta_s, counts = _route(xs, mod_l, lat_row, wrt, rb, tri, counts)
        base, tile_e0, tile_e1, n_used = _bucket_layout(counts)
        rows = jnp.zeros((SORTED_ROWS, D_MODEL), F32)
        rows = _scatter(base, meta_p[0], meta_p[1], h2p, rows)
        rows = _scatter(base, meta_s[0], meta_s[1], h2s, rows)
        y = _experts(tile_e0, tile_e1, n_used, rows, wrt, w13, w2)
        xp = _gather_norm(alpha, base, meta_p[0], meta_p[1], xp, mod_l, ctx_row, y, ln_g[l, 1:2], ln_b[l, 1:2])
        xs = _gather_norm(alpha, base, meta_s[0], meta_s[1], xs, mod_l, lat_row, y, ln_g[l, 1:2], ln_b[l, 1:2])

    def cache(idx, shape):
        return jnp.stack([new_kv[l][idx].reshape(BATCH, SEQ, *shape) for l in range(DEPTH)], axis=1)

    return (xp.reshape(BATCH, SEQ, D_MODEL), xs.reshape(DEC_BATCH, DEC_SEQ, D_MODEL),
            cache(0, (2, H_A, HEAD_DIM)), cache(1, (H_A, 2 * HEAD_DIM)),
            cache(2, (H_B, HEAD_DIM)), cache(3, (H_B, HEAD_DIM)),
            cache(4, (KV_C, HEAD_DIM)), cache(5, (KV_C, HEAD_DIM)))
```

```python
import functools
import math
import operator

import jax
import jax.numpy as jnp
from jax import lax
from jax.experimental import pallas as pl
from jax.experimental.pallas import tpu as pltpu

F32 = jnp.float32
BF16 = jnp.bfloat16

D_MODEL = 1024
BATCH = 32
SEQ = 256
DEPTH = 2
DEC_BATCH = 4
DEC_SEQ = 4096
PAST_LEN = 256
GRID_W = 64
GRID_H = DEC_SEQ // GRID_W
HEAD_DIM = 64
H_A = 4
H_B = 8
H_C = 8
KV_C = 2
NA_KH = 8
NA_KW = 16
ROPE_THETA = 10000.0
N_EXPERTS = 16
N_GROUPS = 4
EXPERTS_PER_GROUP = N_EXPERTS // N_GROUPS
D_EXPERT = 256
BRANCH_W = 512
LN_EPS = 1e-5
RMS_EPS = 1e-6
LOG2E = math.log2(math.e)
QK_SCALE = HEAD_DIM ** -0.5 * LOG2E
NEG_BIAS = -1e30
NBR_ROWS = 4
NBR_KEY_ROWS = 12

LANES = 128
N_COND = 8
VMEM_LIMIT = 56 * 1024 * 1024

OFF_QA, OFF_KA, OFF_VA, OFF_QB, OFF_KB, OFF_VB, OFF_QC, OFF_KC, OFF_VC = (
    0, 512, 1024, 1536, 2048, 2560, 3072, 3584, 3712)
IN_WIDTH = 3840
Q_PACK = 1536
KV_PACK = 1152

TM = 256
TR = 256
GROUP_PAIRS = tuple((i, j) for i in range(EXPERTS_PER_GROUP) for j in range(i + 1, EXPERTS_PER_GROUP))
N_BUCKETS = N_GROUPS * len(GROUP_PAIRS)
BUCKET_ROWS = 32
N_TOKENS = BATCH * SEQ + DEC_BATCH * DEC_SEQ
SORTED_ROWS = N_TOKENS + N_BUCKETS * TR
TQ_A = 256
TQ_C = 512


def _params(n_axes):
    return pltpu.CompilerParams(dimension_semantics=("parallel",) * n_axes,
                                vmem_limit_bytes=VMEM_LIMIT)


def _dot(a, b):
    return jnp.dot(a, b, preferred_element_type=F32)


def _dot_nt(a, b):
    return lax.dot_general(a, b, (((1,), (1,)), ((), ())), preferred_element_type=F32)


def _layernorm(z, g, b):
    mu = jnp.mean(z, axis=-1, keepdims=True)
    zc = z - mu
    var = jnp.mean(zc * zc, axis=-1, keepdims=True)
    return zc * lax.rsqrt(var + LN_EPS) * g + b


def _mod_body(c_ref, w_ref, b_ref, o_ref):
    c = c_ref[...]
    s = (c * jax.nn.sigmoid(c)).astype(BF16)
    o_ref[...] = _dot(s, w_ref[...].astype(BF16)) + b_ref[...]


def _modulation(cond, w_mod, b_mod):
    tn = 1536
    return pl.pallas_call(
        _mod_body,
        out_shape=jax.ShapeDtypeStruct((DEPTH, N_COND, 6 * D_MODEL), F32),
        grid=(DEPTH, 6 * D_MODEL // tn),
        in_specs=[
            pl.BlockSpec((N_COND, D_MODEL), lambda l, j: (0, 0)),
            pl.BlockSpec((None, D_MODEL, tn), lambda l, j: (l, 0, j)),
            pl.BlockSpec((None, 1, tn), lambda l, j: (l, 0, j)),
        ],
        out_specs=pl.BlockSpec((None, N_COND, tn), lambda l, j: (l, 0, j)),
        compiler_params=_params(2),
        name="modulation",
    )(cond, w_mod, b_mod.reshape(DEPTH, 1, 6 * D_MODEL))


def _in_proj_body(rope, x_ref, mod_ref, w_ref, qn_ref, kn_ref, g512_ref, g128_ref, *rest):
    if rope:
        ca_ref, cb_ref, cc_ref = rest[:3]
        outs = rest[3:]
    else:
        outs = rest
    x = x_ref[...]
    sh1 = mod_ref[:, 0:D_MODEL]
    sc1 = mod_ref[:, D_MODEL:2 * D_MODEL]
    h = (x * (1.0 + sc1) + sh1).astype(BF16)

    def proj(off, width):
        return _dot(h, w_ref[:, off:off + width])

    def rms(t, g_ref, wn_ref):
        t2 = t * t
        hi = t2.astype(BF16)
        lo = (t2 - hi.astype(F32)).astype(BF16)
        ms = _dot(hi, g_ref[...]) + _dot(lo, g_ref[...])
        return t * lax.rsqrt(ms + RMS_EPS) * wn_ref[...]

    def rot(t):
        if not rope:
            return t
        ca, cb, cc = ca_ref[...], cb_ref[...], cc_ref[...]
        pieces = []
        for j in range(t.shape[1] // LANES):
            blk = t[:, j * LANES:(j + 1) * LANES]
            pieces.append(blk * ca + pltpu.roll(blk, LANES - 1, 1) * cb + pltpu.roll(blk, 1, 1) * cc)
        return pieces[0] if len(pieces) == 1 else jnp.concatenate(pieces, axis=1)

    qa = rot(proj(OFF_QA, 512)) * QK_SCALE
    ka = rot(proj(OFF_KA, 512))
    va = proj(OFF_VA, 512)
    qb = proj(OFF_QB, 512) * QK_SCALE
    kb = proj(OFF_KB, 512)
    vb = proj(OFF_VB, 512)
    qc = rot(rms(proj(OFF_QC, 512), g512_ref, qn_ref)) * QK_SCALE
    kc = rot(rms(proj(OFF_KC, 128), g128_ref, kn_ref))
    vc = proj(OFF_VC, 128)

    q_ref = outs[0]
    q_ref[:, 0:512] = qa.astype(BF16)
    q_ref[:, 512:1024] = qb.astype(BF16)
    q_ref[:, 1024:1536] = qc.astype(BF16)
    if rope:
        k_ref, v_ref = outs[1:]
        k_ref[:, 0:512] = ka.astype(BF16)
        k_ref[:, 512:1024] = kb.astype(BF16)
        k_ref[:, 1024:1152] = kc.astype(BF16)
        v_ref[:, 0:512] = va.astype(BF16)
        v_ref[:, 512:1024] = vb.astype(BF16)
        v_ref[:, 1024:1152] = vc.astype(BF16)
    else:
        ka_ref, va_ref, kb_ref, vb_ref, kc_ref, vc_ref = outs[1:]
        ka_ref[...] = ka
        va_ref[...] = va
        kb_ref[...] = kb
        vb_ref[...] = vb
        kc_ref[...] = kc
        vc_ref[...] = vc


def _in_proj(x, mod_l, row_of_tile, w_bf, qn, kn, g512, g128, rope_tabs):
    n_tok = x.shape[0]
    n_tiles = n_tok // TM
    rope = rope_tabs is not None
    const = lambda i: (0, 0)
    in_specs = [
        pl.BlockSpec((TM, D_MODEL), lambda i: (i, 0)),
        pl.BlockSpec((None, 1, 6 * D_MODEL), lambda i: (row_of_tile(i), 0, 0)),
        pl.BlockSpec((D_MODEL, IN_WIDTH), const),
        pl.BlockSpec((1, 512), const),
        pl.BlockSpec((1, 128), const),
        pl.BlockSpec((512, 512), const),
        pl.BlockSpec((128, 128), const),
    ]
    args = [x, mod_l, w_bf, qn, kn, g512, g128]
    tile = lambda width: pl.BlockSpec((TM, width), lambda i: (i, 0))
    if rope:
        tiles_per_seq = DEC_SEQ // TM
        in_specs += [pl.BlockSpec((TM, LANES), lambda i: (i % tiles_per_seq, 0))] * 3
        args += list(rope_tabs)
        out_shape = [jax.ShapeDtypeStruct((n_tok, Q_PACK), BF16),
                     jax.ShapeDtypeStruct((n_tok, KV_PACK), BF16),
                     jax.ShapeDtypeStruct((n_tok, KV_PACK), BF16)]
        out_specs = [tile(Q_PACK), tile(KV_PACK), tile(KV_PACK)]
    else:
        widths = (512, 512, 512, 512, 128, 128)
        out_shape = [jax.ShapeDtypeStruct((n_tok, Q_PACK), BF16)] + [
            jax.ShapeDtypeStruct((n_tok, w), F32) for w in widths]
        out_specs = [tile(Q_PACK)] + [tile(w) for w in widths]
    return pl.pallas_call(
        functools.partial(_in_proj_body, rope),
        out_shape=out_shape,
        grid=(n_tiles,),
        in_specs=in_specs,
        out_specs=out_specs,
        compiler_params=_params(1),
        name="in_proj_latent" if rope else "in_proj_context",
    )(*args)


def _head_scores(q_pair, half, k_pair):
    lane = lax.broadcasted_iota(jnp.int32, q_pair.shape, 1)
    keep = (lane < HEAD_DIM) if half == 0 else (lane >= HEAD_DIM)
    qm = jnp.where(keep, q_pair, jnp.zeros_like(q_pair))
    return _dot_nt(qm, k_pair)


def _softmax_parts(s_list):
    mx = functools.reduce(jnp.maximum, [jnp.max(s, axis=-1, keepdims=True) for s in s_list])
    e_list = [jnp.exp2(s - mx) for s in s_list]
    den = functools.reduce(operator.add, [jnp.sum(e, axis=-1, keepdims=True) for e in e_list])
    return e_list, 1.0 / den


def _pair_attention(q_pair, segs):
    outs = []
    for half in (0, 1):
        s_list = []
        for k_pair, _, bias in segs:
            s = _head_scores(q_pair, half, k_pair)
            if bias is not None:
                s = s + bias(half)
            s_list.append(s)
        e_list, inv = _softmax_parts(s_list)
        o = functools.reduce(operator.add, [_dot(e.astype(BF16), seg[1]) for e, seg in zip(e_list, segs)])
        outs.append(o * inv)
    lane = lax.broadcasted_iota(jnp.int32, outs[0].shape, 1)
    return jnp.where(lane < HEAD_DIM, outs[0], outs[1])


def _lambda(lam_ref, layer):
    lam_init = 0.8 - 0.6 * math.exp(-0.3 * layer)
    l = lam_ref[...]
    a = jnp.sum(l[0:1] * l[1:2], axis=-1, keepdims=True)
    b = jnp.sum(l[2:3] * l[3:4], axis=-1, keepdims=True)
    return jnp.exp(a) - jnp.exp(b) + lam_init, lam_init


def _diff_attention(q_blk, k_segs, v_segs, lam_v, lam_init, subln):
    outs = []
    for h in range(H_A):
        j, half = h // 2, h % 2
        parts = []
        for m in range(2):
            qb = q_blk(2 * m + j)
            s_list = [_head_scores(qb, half, ks(2 * m + j)) for ks in k_segs]
            parts.append(_softmax_parts(s_list))
        (e1, inv1), (e2, inv2) = parts
        c2 = lam_v * inv2
        o = functools.reduce(operator.add, [
            _dot((e1[s] * inv1 - e2[s] * c2).astype(BF16), v_segs[s](h)) for s in range(len(k_segs))])
        ms = jnp.mean(o * o, axis=-1, keepdims=True)
        outs.append(o * lax.rsqrt(ms + RMS_EPS) * subln * (1.0 - lam_init))
    return outs


def _tile(ref, j, dtype=None):
    t = ref[:, j * LANES:(j + 1) * LANES]
    return t if dtype is None else t.astype(dtype)


def _ctx_attn_body(layer, q_ref, ka_ref, va_ref, kb_ref, vb_ref, kc_ref, vc_ref, lam_ref, subln_ref,
                   oa_ref, ob_ref, oc_ref):
    lam_v, lam_init = _lambda(lam_ref, layer)
    oa = _diff_attention(lambda j: _tile(q_ref, j),
                         [lambda j: _tile(ka_ref, j, BF16)],
                         [lambda h: _tile(va_ref, h, BF16)],
                         lam_v, lam_init, subln_ref[...])
    for h in range(H_A):
        oa_ref[:, h * LANES:(h + 1) * LANES] = oa[h].astype(BF16)
    for j in range(4):
        o = _pair_attention(_tile(q_ref, 4 + j), [(_tile(kb_ref, j, BF16), _tile(vb_ref, j, BF16), None)])
        ob_ref[:, j * LANES:(j + 1) * LANES] = o.astype(BF16)
    kc = kc_ref[...].astype(BF16)
    vc = vc_ref[...].astype(BF16)
    for g in range(4):
        o = _pair_attention(_tile(q_ref, 8 + g), [(kc, vc, None)])
        oc_ref[:, g * LANES:(g + 1) * LANES] = o.astype(BF16)


def _ctx_attention(layer, q, ka, va, kb, vb, kc, vc, lam_l, subln_l):
    n_tok = q.shape[0]
    tile = lambda width: pl.BlockSpec((SEQ, width), lambda b: (b, 0))
    const = lambda b: (0, 0)
    return pl.pallas_call(
        functools.partial(_ctx_attn_body, layer),
        out_shape=[jax.ShapeDtypeStruct((n_tok, BRANCH_W), BF16)] * 3,
        grid=(n_tok // SEQ,),
        in_specs=[tile(Q_PACK), tile(512), tile(512), tile(512), tile(512), tile(128), tile(128),
                  pl.BlockSpec((4, HEAD_DIM), const), pl.BlockSpec((1, 2 * HEAD_DIM), const)],
        out_specs=[tile(BRANCH_W)] * 3,
        compiler_params=_params(1),
        name="context_attention",
    )(q, ka, va, kb, vb, kc, vc, lam_l, subln_l)


def _lat_diff_body(layer, q_ref, kl_ref, vl_ref, kc_ref, vc_ref, lam_ref, subln_ref, o_ref):
    lam_v, lam_init = _lambda(lam_ref, layer)
    oa = _diff_attention(lambda j: _tile(q_ref, j),
                         [lambda j: _tile(kc_ref, j, BF16), lambda j: _tile(kl_ref, j)],
                         [lambda h: _tile(vc_ref, h, BF16), lambda h: _tile(vl_ref, h)],
                         lam_v, lam_init, subln_ref[...])
    for h in range(H_A):
        o_ref[:, h * LANES:(h + 1) * LANES] = oa[h].astype(BF16)


def _lat_diff_attention(layer, q, k, v, cache_k, cache_v, lam_l, subln_l):
    nq = DEC_SEQ // TQ_A
    const = lambda b, i: (0, 0)
    return pl.pallas_call(
        functools.partial(_lat_diff_body, layer),
        out_shape=jax.ShapeDtypeStruct((DEC_BATCH * DEC_SEQ, BRANCH_W), BF16),
        grid=(DEC_BATCH, nq),
        in_specs=[
            pl.BlockSpec((TQ_A, 512), lambda b, i: (b * nq + i, 0)),
            pl.BlockSpec((DEC_SEQ, 512), lambda b, i: (b, 0)),
            pl.BlockSpec((DEC_SEQ, 512), lambda b, i: (b, 0)),
            pl.BlockSpec((None, None, PAST_LEN, 512), lambda b, i: (b, layer, 0, 0)),
            pl.BlockSpec((None, None, PAST_LEN, 512), lambda b, i: (b, layer, 0, 0)),
            pl.BlockSpec((4, HEAD_DIM), const),
            pl.BlockSpec((1, 2 * HEAD_DIM), const),
        ],
        out_specs=pl.BlockSpec((TQ_A, BRANCH_W), lambda b, i: (b * nq + i, 0)),
        compiler_params=_params(2),
        name="latent_diff_attention",
    )(q, k, v, cache_k, cache_v, lam_l, subln_l)


def _lat_gqa_body(q_ref, kl_ref, vl_ref, kc_ref, vc_ref, o_ref):
    segs = [(kc_ref[...].astype(BF16), vc_ref[...].astype(BF16), None),
            (kl_ref[...], vl_ref[...], None)]
    for g in range(4):
        o = _pair_attention(_tile(q_ref, g), segs)
        o_ref[:, g * LANES:(g + 1) * LANES] = o.astype(BF16)


def _lat_gqa_attention(layer, q, k, v, cache_k, cache_v):
    nq = DEC_SEQ // TQ_C
    return pl.pallas_call(
        _lat_gqa_body,
        out_shape=jax.ShapeDtypeStruct((DEC_BATCH * DEC_SEQ, BRANCH_W), BF16),
        grid=(DEC_BATCH, nq),
        in_specs=[
            pl.BlockSpec((TQ_C, 512), lambda b, i: (b * nq + i, 2)),
            pl.BlockSpec((DEC_SEQ, LANES), lambda b, i: (b, 1024 // LANES)),
            pl.BlockSpec((DEC_SEQ, LANES), lambda b, i: (b, 1024 // LANES)),
            pl.BlockSpec((None, None, PAST_LEN, LANES), lambda b, i: (b, layer, 0, 0)),
            pl.BlockSpec((None, None, PAST_LEN, LANES), lambda b, i: (b, layer, 0, 0)),
        ],
        out_specs=pl.BlockSpec((TQ_C, BRANCH_W), lambda b, i: (b * nq + i, 0)),
        compiler_params=_params(2),
        name="latent_gqa_attention",
    )(q, k, v, cache_k, cache_v)


def _nbr_key_start(g):
    return jnp.clip(g * NBR_ROWS - NA_KH // 2, 0, GRID_H - NBR_KEY_ROWS)


def _lat_nbr_body(q_ref, kl_ref, vl_ref, kc_ref, vc_ref, bias_ref, o_ref):
    base = pl.multiple_of(_nbr_key_start(pl.program_id(1)) * GRID_W, GRID_W)
    n_loc = NBR_KEY_ROWS * GRID_W
    for j in range(4):
        k_loc = kl_ref[pl.ds(base, n_loc), j * LANES:(j + 1) * LANES]
        v_loc = vl_ref[pl.ds(base, n_loc), j * LANES:(j + 1) * LANES]
        segs = [(k_loc, v_loc, lambda half, j=j: bias_ref[2 * j + half]),
                (_tile(kc_ref, j, BF16), _tile(vc_ref, j, BF16), None)]
        o = _pair_attention(_tile(q_ref, j), segs)
        o_ref[:, j * LANES:(j + 1) * LANES] = o.astype(BF16)


def _lat_nbr_attention(layer, q, k, v, cache_k, cache_v, bias_tab):
    n_groups = GRID_H // NBR_ROWS
    tq = NBR_ROWS * GRID_W
    case = lambda g: jnp.where(g == 0, 0, jnp.where(g == n_groups - 1, 2, 1))
    return pl.pallas_call(
        _lat_nbr_body,
        out_shape=jax.ShapeDtypeStruct((DEC_BATCH * DEC_SEQ, BRANCH_W), BF16),
        grid=(DEC_BATCH, n_groups),
        in_specs=[
            pl.BlockSpec((tq, 512), lambda b, g: (b * n_groups + g, 1)),
            pl.BlockSpec((DEC_SEQ, 512), lambda b, g: (b, 1)),
            pl.BlockSpec((DEC_SEQ, 512), lambda b, g: (b, 1)),
            pl.BlockSpec((None, None, PAST_LEN, 512), lambda b, g: (b, layer, 0, 0)),
            pl.BlockSpec((None, None, PAST_LEN, 512), lambda b, g: (b, layer, 0, 0)),
            pl.BlockSpec((None, H_B, tq, NBR_KEY_ROWS * GRID_W), lambda b, g: (case(g), 0, 0, 0)),
        ],
        out_specs=pl.BlockSpec((tq, BRANCH_W), lambda b, g: (b * n_groups + g, 0)),
        compiler_params=_params(2),
        name="latent_neighbourhood_attention",
    )(q, k, v, cache_k, cache_v, bias_tab)


def _merge_body(alpha, x_ref, oa_ref, ob_ref, oc_ref, mod_ref, wg_ref, bg_ref, wbr_ref, wo_ref,
                lng_ref, lnb_ref, out_ref):
    x = x_ref[...]
    sh1 = mod_ref[:, 0:D_MODEL]
    sc1 = mod_ref[:, D_MODEL:2 * D_MODEL]
    g1 = mod_ref[:, 2 * D_MODEL:3 * D_MODEL]
    h = (x * (1.0 + sc1) + sh1).astype(BF16)
    acc = None
    for i, o_ref in enumerate((oa_ref, ob_ref, oc_ref)):
        cols = slice(i * D_MODEL, (i + 1) * D_MODEL)
        gate = jax.nn.sigmoid(_dot(h, wg_ref[:, cols]) + bg_ref[:, cols])
        term = gate * _dot(o_ref[...], wbr_ref[i])
        acc = term if acc is None else acc + term
    y = _dot(acc.astype(BF16), wo_ref[...])
    out_ref[...] = _layernorm(alpha * x + g1 * y, lng_ref[...], lnb_ref[...])


def _merge(alpha, x, oa, ob, oc, mod_l, row_of_tile, wg, bg, wbr, wo, lng, lnb):
    n_tok = x.shape[0]
    const = lambda i: (0, 0)
    tile = lambda width: pl.BlockSpec((TM, width), lambda i: (i, 0))
    return pl.pallas_call(
        functools.partial(_merge_body, alpha),
        out_shape=jax.ShapeDtypeStruct((n_tok, D_MODEL), F32),
        grid=(n_tok // TM,),
        in_specs=[
            tile(D_MODEL), tile(BRANCH_W), tile(BRANCH_W), tile(BRANCH_W),
            pl.BlockSpec((None, 1, 6 * D_MODEL), lambda i: (row_of_tile(i), 0, 0)),
            pl.BlockSpec((D_MODEL, 3 * D_MODEL), const),
            pl.BlockSpec((1, 3 * D_MODEL), const),
            pl.BlockSpec((3, BRANCH_W, D_MODEL), lambda i: (0, 0, 0)),
            pl.BlockSpec((D_MODEL, D_MODEL), const),
            pl.BlockSpec((1, D_MODEL), const),
            pl.BlockSpec((1, D_MODEL), const),
        ],
        out_specs=tile(D_MODEL),
        compiler_params=_params(1),
        name="branch_merge",
    )(x, oa, ob, oc, mod_l, wg, bg, wbr, wo, lng, lnb)


def _routing(logits, bias):
    scores = jax.nn.sigmoid(logits)
    sel_all = scores + bias
    sel = [sel_all[e:e + 1] for e in range(N_EXPERTS)]
    gscore = []
    for g in range(N_GROUPS):
        a, b, c, d = sel[4 * g:4 * g + 4]
        hi1, lo1 = jnp.maximum(a, b), jnp.minimum(a, b)
        hi2, lo2 = jnp.maximum(c, d), jnp.minimum(c, d)
        gscore.append(jnp.maximum(hi1, hi2) + jnp.maximum(jnp.minimum(hi1, hi2), jnp.maximum(lo1, lo2)))
    best = []
    taken = None
    for g in range(N_GROUPS):
        ok = functools.reduce(operator.and_, [gscore[g] >= gscore[k] for k in range(g + 1, N_GROUPS)],
                              jnp.ones_like(gscore[g], dtype=jnp.bool_))
        if taken is not None:
            ok = ok & jnp.logical_not(taken)
        best.append(ok)
        taken = ok if taken is None else (taken | ok)
    picked = []
    for e in range(N_EXPERTS):
        g, i = divmod(e, EXPERTS_PER_GROUP)
        rank = jnp.zeros_like(sel[0])
        for k in range(EXPERTS_PER_GROUP):
            if k == i:
                continue
            other = sel[4 * g + k]
            ahead = (other >= sel[e]) if k < i else (other > sel[e])
            rank = rank + jnp.where(ahead, 1.0, 0.0)
        picked.append((rank < 2.0) & best[g])
    return [picked[4 * g + i] & picked[4 * g + j] for g in range(N_GROUPS) for i, j in GROUP_PAIRS]


def _split_bf16(v):
    hi = v.astype(BF16)
    return hi, (v - hi.astype(F32)).astype(BF16)


TILE_ROWS = D_MODEL // LANES


def _store_token_tiles(ref, val):
    for t0 in range(0, val.shape[0], 8):
        for j in range(TILE_ROWS):
            ref[pl.ds(t0 * TILE_ROWS + j, 8, stride=TILE_ROWS), :] = val[t0:t0 + 8, j * LANES:(j + 1) * LANES]


def _load_token_tiles(ref, n):
    rows = [jnp.concatenate([ref[pl.ds(t0 * TILE_ROWS + j, 8, stride=TILE_ROWS), :] for j in range(TILE_ROWS)],
                            axis=1) for t0 in range(0, n, 8)]
    return jnp.concatenate(rows, axis=0)


def _route_body(x_ref, mod_ref, wrt_ref, rb_ref, tri_ref, cnt_in_ref, h2_ref, meta_ref, cnt_ref):
    @pl.when(pl.program_id(0) == 0)
    def _():
        cnt_ref[...] = cnt_in_ref[...]

    x = x_ref[...]
    sh2 = mod_ref[:, 3 * D_MODEL:4 * D_MODEL]
    sc2 = mod_ref[:, 4 * D_MODEL:5 * D_MODEL]
    h2 = x * (1.0 + sc2) + sh2
    _store_token_tiles(h2_ref, h2)
    hi, lo = _split_bf16(h2)
    wr_hi, wr_lo = _split_bf16(wrt_ref[...])
    logits = _dot_nt(wr_hi, hi) + _dot_nt(wr_hi, lo) + _dot_nt(wr_lo, hi)
    masks = _routing(logits, rb_ref[...])
    tm = x.shape[0]
    onehot = jnp.concatenate([jnp.where(m, 1.0, 0.0) for m in masks]
                             + [jnp.zeros((BUCKET_ROWS - N_BUCKETS, tm), F32)], axis=0)
    ahead = _dot(onehot.astype(BF16), tri_ref[...]) + cnt_ref[:, 0:1]
    rank = jnp.sum(onehot * ahead, axis=0, keepdims=True)
    bucket_id = lax.broadcasted_iota(jnp.int32, onehot.shape, 0).astype(F32)
    bucket = jnp.sum(onehot * bucket_id, axis=0, keepdims=True)
    meta_ref[...] = jnp.concatenate([bucket, rank, jnp.zeros((6, tm), F32)], axis=0).astype(jnp.int32)
    cnt_ref[...] = cnt_ref[...] + jnp.sum(onehot, axis=1, keepdims=True)


def _route(x, mod_l, row_of_tile, wrt, rb, tri, cnt_in):
    n_tok = x.shape[0]
    const = lambda i: (0, 0)
    tile = pl.BlockSpec((TM, D_MODEL), lambda i: (i, 0))
    return pl.pallas_call(
        _route_body,
        out_shape=[jax.ShapeDtypeStruct((n_tok * TILE_ROWS, LANES), F32),
                   jax.ShapeDtypeStruct((8, n_tok), jnp.int32),
                   jax.ShapeDtypeStruct((BUCKET_ROWS, LANES), F32)],
        grid=(n_tok // TM,),
        in_specs=[
            tile,
            pl.BlockSpec((None, 1, 6 * D_MODEL), lambda i: (row_of_tile(i), 0, 0)),
            pl.BlockSpec((N_EXPERTS, D_MODEL), const),
            pl.BlockSpec((N_EXPERTS, 1), const),
            pl.BlockSpec((TM, TM), const),
            pl.BlockSpec((BUCKET_ROWS, LANES), const),
        ],
        out_specs=[pl.BlockSpec((TM * TILE_ROWS, LANES), lambda i: (i, 0)),
                   pl.BlockSpec((8, TM), lambda i: (0, i)), pl.BlockSpec((BUCKET_ROWS, LANES), const)],
        compiler_params=pltpu.CompilerParams(dimension_semantics=("arbitrary",), vmem_limit_bytes=VMEM_LIMIT),
        name="moe_route",
    )(x, mod_l, wrt, rb, tri, cnt_in)


def _row_copy(src_ref, src_row, dst_ref, dst_row, sem):
    src = src_ref.at[pl.ds(pl.multiple_of(src_row * TILE_ROWS, TILE_ROWS), TILE_ROWS)]
    dst = dst_ref.at[pl.ds(pl.multiple_of(dst_row * TILE_ROWS, TILE_ROWS), TILE_ROWS)]
    return pltpu.make_async_copy(src, dst, sem)


def _dest_row(base_ref, bucket_ref, rank_ref, token):
    return base_ref[bucket_ref[token]] + rank_ref[token]


def _scatter_body(base_ref, bucket_ref, rank_ref, h_ref, xs_in_ref, xs_ref, sem):
    del xs_in_ref
    first = pl.program_id(0) * TM

    def issue(t, carry):
        _row_copy(h_ref, t, xs_ref, _dest_row(base_ref, bucket_ref, rank_ref, first + t), sem).start()
        return carry

    def drain(t, carry):
        _row_copy(h_ref, t, xs_ref, 0, sem).wait()
        return carry

    lax.fori_loop(0, TM, issue, 0, unroll=8)
    lax.fori_loop(0, TM, drain, 0, unroll=8)


def _scatter(base, bucket, rank, h2, xs):
    n_tok = h2.shape[0] // TILE_ROWS
    return pl.pallas_call(
        _scatter_body,
        out_shape=jax.ShapeDtypeStruct(xs.shape, xs.dtype),
        grid_spec=pltpu.PrefetchScalarGridSpec(
            num_scalar_prefetch=3,
            grid=(n_tok // TM,),
            in_specs=[pl.BlockSpec((TM * TILE_ROWS, LANES), lambda i, *_: (i, 0)),
                      pl.BlockSpec(memory_space=pl.ANY)],
            out_specs=pl.BlockSpec(memory_space=pl.ANY),
            scratch_shapes=[pltpu.SemaphoreType.DMA(())],
        ),
        input_output_aliases={4: 0},
        compiler_params=pltpu.CompilerParams(dimension_semantics=("arbitrary",), vmem_limit_bytes=VMEM_LIMIT),
        name="moe_scatter",
    )(base, bucket, rank, h2, xs)


def _experts_body(e0_ref, e1_ref, used_ref, xs_ref, wrt_ref, w13a_ref, w2a_ref, w13b_ref, w2b_ref, y_ref):
    j = pl.program_id(0)

    @pl.when(j < used_ref[0])
    def _():
        hi, lo = _split_bf16(_load_token_tiles(xs_ref, TR))
        wr = jnp.concatenate([wrt_ref[pl.ds(e0_ref[j], 1), :], wrt_ref[pl.ds(e1_ref[j], 1), :],
                              jnp.zeros((6, D_MODEL), F32)], axis=0)
        wr_hi, wr_lo = _split_bf16(wr)
        logits = _dot_nt(hi, wr_hi) + _dot_nt(lo, wr_hi) + _dot_nt(hi, wr_lo)
        s = jax.nn.sigmoid(logits)
        s0 = s[:, 0:1]
        s1 = s[:, 1:2]
        inv = 1.0 / (s0 + s1)

        def ffn(w13_ref, w2_ref):
            h13 = _dot(hi, w13_ref[...])
            a = h13[:, :D_EXPERT]
            b = h13[:, D_EXPERT:]
            return _dot((a * jax.nn.sigmoid(a) * b).astype(BF16), w2_ref[...])

        _store_token_tiles(y_ref, (s0 * inv) * ffn(w13a_ref, w2a_ref) + (s1 * inv) * ffn(w13b_ref, w2b_ref))

    @pl.when(j >= used_ref[0])
    def _():
        y_ref[...] = jnp.zeros_like(y_ref)


def _experts(tile_e0, tile_e1, n_used, xs, wrt, w13, w2):
    n_rows = xs.shape[0] // TILE_ROWS
    tile = pl.BlockSpec((TR * TILE_ROWS, LANES), lambda j, *_: (j, 0))
    return pl.pallas_call(
        _experts_body,
        out_shape=jax.ShapeDtypeStruct(xs.shape, F32),
        grid_spec=pltpu.PrefetchScalarGridSpec(
            num_scalar_prefetch=3,
            grid=(n_rows // TR,),
            in_specs=[
                tile,
                pl.BlockSpec((N_EXPERTS, D_MODEL), lambda j, *_: (0, 0)),
                pl.BlockSpec((None, D_MODEL, 2 * D_EXPERT), lambda j, e0, e1, u: (e0[j], 0, 0)),
                pl.BlockSpec((None, D_EXPERT, D_MODEL), lambda j, e0, e1, u: (e0[j], 0, 0)),
                pl.BlockSpec((None, D_MODEL, 2 * D_EXPERT), lambda j, e0, e1, u: (e1[j], 0, 0)),
                pl.BlockSpec((None, D_EXPERT, D_MODEL), lambda j, e0, e1, u: (e1[j], 0, 0)),
            ],
            out_specs=tile,
        ),
        compiler_params=pltpu.CompilerParams(dimension_semantics=("arbitrary",), vmem_limit_bytes=VMEM_LIMIT),
        name="moe_experts",
    )(tile_e0, tile_e1, n_used, xs, wrt, w13, w2, w13, w2)


def _gather_norm_body(alpha, base_ref, bucket_ref, rank_ref, x_ref, mod_ref, y_ref, lng_ref, lnb_ref,
                      out_ref, m_ref, sem):
    first = pl.program_id(0) * TM

    def issue(t, carry):
        _row_copy(y_ref, _dest_row(base_ref, bucket_ref, rank_ref, first + t), m_ref, t, sem).start()
        return carry

    def drain(t, carry):
        _row_copy(y_ref, 0, m_ref, t, sem).wait()
        return carry

    lax.fori_loop(0, TM, issue, 0, unroll=8)
    lax.fori_loop(0, TM, drain, 0, unroll=8)
    g2 = mod_ref[:, 5 * D_MODEL:6 * D_MODEL]
    m = _load_token_tiles(m_ref, TM)
    out_ref[...] = _layernorm(alpha * x_ref[...] + g2 * m, lng_ref[...], lnb_ref[...])


def _gather_norm(alpha, base, bucket, rank, x, mod_l, row_of_tile, y, lng, lnb):
    n_tok = x.shape[0]
    const = lambda i, *_: (0, 0)
    tile = pl.BlockSpec((TM, D_MODEL), lambda i, *_: (i, 0))
    return pl.pallas_call(
        functools.partial(_gather_norm_body, alpha),
        out_shape=jax.ShapeDtypeStruct((n_tok, D_MODEL), F32),
        grid_spec=pltpu.PrefetchScalarGridSpec(
            num_scalar_prefetch=3,
            grid=(n_tok // TM,),
            in_specs=[
                tile,
                pl.BlockSpec((None, 1, 6 * D_MODEL), lambda i, *_: (row_of_tile(i), 0, 0)),
                pl.BlockSpec(memory_space=pl.ANY),
                pl.BlockSpec((1, D_MODEL), const),
                pl.BlockSpec((1, D_MODEL), const),
            ],
            out_specs=tile,
            scratch_shapes=[pltpu.VMEM((TM * TILE_ROWS, LANES), F32), pltpu.SemaphoreType.DMA(())],
        ),
        compiler_params=pltpu.CompilerParams(dimension_semantics=("arbitrary",), vmem_limit_bytes=VMEM_LIMIT),
        name="moe_gather_norm",
    )(base, bucket, rank, x, mod_l, y, lng, lnb)


def _bucket_layout(counts):
    cnt = counts[:N_BUCKETS, 0].astype(jnp.int32)
    padded = (cnt + TR - 1) // TR * TR
    ends = jnp.cumsum(padded)
    base = jnp.zeros((BUCKET_ROWS,), jnp.int32).at[:N_BUCKETS].set(ends - padded)
    tile_start = jnp.arange(SORTED_ROWS // TR, dtype=jnp.int32) * TR
    tile_bucket = jnp.minimum(jnp.sum(tile_start[:, None] >= ends[None, :], axis=1), N_BUCKETS - 1)
    first = jnp.array([4 * g + i for g in range(N_GROUPS) for i, _ in GROUP_PAIRS], jnp.int32)
    second = jnp.array([4 * g + j for g in range(N_GROUPS) for _, j in GROUP_PAIRS], jnp.int32)
    return base, first[tile_bucket], second[tile_bucket], (ends[-1:] // TR).astype(jnp.int32)


def _rope_tables():
    t = jnp.arange(DEC_SEQ)
    row = (t // GRID_W).astype(F32)
    col = (t % GRID_W).astype(F32)
    n = HEAD_DIM // 4
    freqs = ROPE_THETA ** (-jnp.arange(n, dtype=F32) / n)
    ang = jnp.concatenate([row[:, None] * freqs, col[:, None] * freqs], axis=-1)
    cos = jnp.tile(jnp.repeat(jnp.cos(ang), 2, axis=-1), (1, LANES // HEAD_DIM))
    sin = jnp.tile(jnp.repeat(jnp.sin(ang), 2, axis=-1), (1, LANES // HEAD_DIM))
    even = (jnp.arange(LANES) % 2 == 0)[None, :]
    return cos, jnp.where(even, -sin, 0.0), jnp.where(even, 0.0, sin)


def _group_mean_matrix(width):
    g = jnp.arange(width) // HEAD_DIM
    return jnp.where(g[:, None] == g[None, :], 1.0 / HEAD_DIM, 0.0).astype(BF16)


def _neighbourhood_bias(rpb_l):
    cx = jnp.arange(GRID_W)
    col_start = jnp.clip(cx - NA_KW // 2, 0, GRID_W - NA_KW)
    kx = jnp.arange(GRID_W)
    inside_col = (kx[None, :] >= col_start[:, None]) & (kx[None, :] < col_start[:, None] + NA_KW)
    dx = jnp.clip(kx[None, :] - cx[:, None] + (NA_KW - 1), 0, 2 * NA_KW - 2)
    exact = lax.Precision.HIGHEST
    pick_dx = ((dx[:, :, None] == jnp.arange(2 * NA_KW - 1)) & inside_col[:, :, None]).astype(F32)
    n_groups = GRID_H // NBR_ROWS
    i = jnp.arange(NBR_KEY_ROWS)
    tabs = []
    for g in (0, 1, n_groups - 1):
        key_start = min(max(g * NBR_ROWS - NA_KH // 2, 0), GRID_H - NBR_KEY_ROWS)
        r = g * NBR_ROWS + jnp.arange(NBR_ROWS)
        row_start = jnp.clip(r - NA_KH // 2, 0, GRID_H - NA_KH)
        ky = key_start + i
        inside_row = (ky[None, :] >= row_start[:, None]) & (ky[None, :] < row_start[:, None] + NA_KH)
        dy = jnp.clip(ky[None, :] - r[:, None] + (NA_KH - 1), 0, 2 * NA_KH - 2)
        pick_dy = ((dy[:, :, None] == jnp.arange(2 * NA_KH - 1)) & inside_row[:, :, None]).astype(F32)
        by_row = jnp.einsum("hyd,jiy->hjid", rpb_l, pick_dy, precision=exact)
        vals = jnp.einsum("hjid,ckd->hjcik", by_row, pick_dx, precision=exact)
        ok = inside_row[:, None, :, None] & inside_col[None, :, None, :]
        tabs.append(jnp.where(ok[None], vals * LOG2E, NEG_BIAS)
                    .reshape(H_B, NBR_ROWS * GRID_W, NBR_KEY_ROWS * GRID_W))
    return jnp.stack(tabs, axis=0)


def _gqa_regroup(w, axis):
    shape = w.shape
    w = w.reshape(shape[:axis] + (KV_C, H_C // KV_C, HEAD_DIM) + shape[axis + 1:])
    return jnp.swapaxes(w, axis, axis + 1).reshape(shape)


def kernel(x_prompt, x_sample, c, cache_a_k, cache_a_v, cache_b_k, cache_b_v, cache_c_k, cache_c_v,
           c_ctx, w_mod, b_mod, w_in, w_gate, b_gate, lam, a_subln, rpb, c_qnorm, c_knorm,
           w_br, w_o, ln_g, ln_b, w_router, router_bias, moe_w1, moe_w3, moe_w2):
    alpha = (2.0 * DEPTH) ** 0.25
    cond = jnp.concatenate([c_ctx[None, :], c, jnp.zeros((N_COND - 1 - DEC_BATCH, D_MODEL), F32)], axis=0)
    mods = _modulation(cond, w_mod, b_mod).reshape(DEPTH, N_COND, 1, 6 * D_MODEL)

    rope_tabs = _rope_tables()
    g512 = _group_mean_matrix(512)
    g128 = _group_mean_matrix(128)
    wrt = w_router.T
    rb = router_bias.reshape(N_EXPERTS, 1)
    tok = jnp.arange(TM)
    tri = (tok[:, None] < tok[None, :]).astype(BF16)

    ck_a = cache_a_k.reshape(DEC_BATCH, DEPTH, PAST_LEN, 512)
    cv_a = cache_a_v.reshape(DEC_BATCH, DEPTH, PAST_LEN, 512)
    ck_b = cache_b_k.reshape(DEC_BATCH, DEPTH, PAST_LEN, 512)
    cv_b = cache_b_v.reshape(DEC_BATCH, DEPTH, PAST_LEN, 512)
    ck_c = cache_c_k.reshape(DEC_BATCH, DEPTH, PAST_LEN, 128)
    cv_c = cache_c_v.reshape(DEC_BATCH, DEPTH, PAST_LEN, 128)

    ctx_row = lambda i: 0
    lat_tiles = DEC_SEQ // TM
    lat_row = lambda i: 1 + i // lat_tiles

    xp = x_prompt.reshape(BATCH * SEQ, D_MODEL)
    xs = x_sample.reshape(DEC_BATCH * DEC_SEQ, D_MODEL)
    new_kv = []
    for l in range(DEPTH):
        w_in_l = w_in[l]
        w_in_l = jnp.concatenate([w_in_l[:, :OFF_QC], _gqa_regroup(w_in_l[:, OFF_QC:OFF_KC], 1),
                                  w_in_l[:, OFF_KC:]], axis=1).astype(BF16)
        qn = jnp.tile(c_qnorm[l], 512 // HEAD_DIM)[None, :]
        kn = jnp.tile(c_knorm[l], 128 // HEAD_DIM)[None, :]
        wg = w_gate[l].astype(BF16)
        bg = b_gate[l][None, :]
        wbr = jnp.stack([w_br[l, 0], w_br[l, 1], _gqa_regroup(w_br[l, 2], 0)], axis=0).astype(BF16)
        wo = w_o[l].astype(BF16)
        w13 = jnp.concatenate([moe_w1[l], moe_w3[l]], axis=-1).astype(BF16)
        w2 = moe_w2[l].astype(BF16)
        lam_l = lam[l]
        subln_l = a_subln[l][None, :]
        bias_tab = _neighbourhood_bias(rpb[l])
        mod_l = mods[l]

        q, ka, va, kb, vb, kc, vc = _in_proj(xp, mod_l, ctx_row, w_in_l, qn, kn, g512, g128, None)
        new_kv.append((ka, va, kb, vb, kc, vc))
        oa, ob, oc = _ctx_attention(l, q, ka, va, kb, vb, kc, vc, lam_l, subln_l)
        xp = _merge(alpha, xp, oa, ob, oc, mod_l, ctx_row, wg, bg, wbr, wo, ln_g[l, 0:1], ln_b[l, 0:1])

        q, k, v = _in_proj(xs, mod_l, lat_row, w_in_l, qn, kn, g512, g128, rope_tabs)
        oa = _lat_diff_attention(l, q, k, v, ck_a, cv_a, lam_l, subln_l)
        ob = _lat_nbr_attention(l, q, k, v, ck_b, cv_b, bias_tab)
        oc = _lat_gqa_attention(l, q, k, v, ck_c, cv_c)
        xs = _merge(alpha, xs, oa, ob, oc, mod_l, lat_row, wg, bg, wbr, wo, ln_g[l, 0:1], ln_b[l, 0:1])

        h2p, meta_p, counts = _route(xp, mod_l, ctx_row, wrt, rb, tri, jnp.zeros((BUCKET_ROWS, LANES), F32))
        h2s, meta_s, counts = _route(xs, mod_l, lat_row, wrt, rb, tri, counts)
        base, tile_e0, tile_e1, n_used = _bucket_layout(counts)
        rows = jnp.zeros((SORTED_ROWS * TILE_ROWS, LANES), F32)
        rows = _scatter(base, meta_p[0], meta_p[1], h2p, rows)
        rows = _scatter(base, meta_s[0], meta_s[1], h2s, rows)
        y = _experts(tile_e0, tile_e1, n_used, rows, wrt, w13, w2)
        xp = _gather_norm(alpha, base, meta_p[0], meta_p[1], xp, mod_l, ctx_row, y, ln_g[l, 1:2], ln_b[l, 1:2])
        xs = _gather_norm(alpha, base, meta_s[0], meta_s[1], xs, mod_l, lat_row, y, ln_g[l, 1:2], ln_b[l, 1:2])

    def cache(idx, shape):
        return jnp.stack([new_kv[l][idx].reshape(BATCH, SEQ, *shape) for l in range(DEPTH)], axis=1)

    return (xp.reshape(BATCH, SEQ, D_MODEL), xs.reshape(DEC_BATCH, DEC_SEQ, D_MODEL),
            cache(0, (2, H_A, HEAD_DIM)), cache(1, (H_A, 2 * HEAD_DIM)),
            cache(2, (H_B, HEAD_DIM)), cache(3, (H_B, HEAD_DIM)),
            cache(4, (KV_C, HEAD_DIM)), cache(5, (KV_C, HEAD_DIM)))
```

```python
import functools
import math
import operator

import jax
import jax.numpy as jnp
from jax import lax
from jax.experimental import pallas as pl
from jax.experimental.pallas import tpu as pltpu

F32 = jnp.float32
BF16 = jnp.bfloat16

D_MODEL = 1024
BATCH = 32
SEQ = 256
DEPTH = 2
DEC_BATCH = 4
DEC_SEQ = 4096
PAST_LEN = 256
GRID_W = 64
GRID_H = DEC_SEQ // GRID_W
HEAD_DIM = 64
H_A = 4
H_B = 8
H_C = 8
KV_C = 2
NA_KH = 8
NA_KW = 16
ROPE_THETA = 10000.0
N_EXPERTS = 16
N_GROUPS = 4
EXPERTS_PER_GROUP = N_EXPERTS // N_GROUPS
D_EXPERT = 256
BRANCH_W = 512
LN_EPS = 1e-5
RMS_EPS = 1e-6
LOG2E = math.log2(math.e)
QK_SCALE = HEAD_DIM ** -0.5 * LOG2E
NEG_BIAS = -1e30
NBR_ROWS = 4
NBR_KEY_ROWS = 12

LANES = 128
N_COND = 8
VMEM_LIMIT = 56 * 1024 * 1024

OFF_QA, OFF_KA, OFF_VA, OFF_QB, OFF_KB, OFF_VB, OFF_QC, OFF_KC, OFF_VC = (
    0, 512, 1024, 1536, 2048, 2560, 3072, 3584, 3712)
IN_WIDTH = 3840
Q_PACK = 1536
KV_PACK = 1152

TM = 256
TR = 256
GROUP_PAIRS = tuple((i, j) for i in range(EXPERTS_PER_GROUP) for j in range(i + 1, EXPERTS_PER_GROUP))
N_BUCKETS = N_GROUPS * len(GROUP_PAIRS)
BUCKET_ROWS = 32
N_TOKENS = BATCH * SEQ + DEC_BATCH * DEC_SEQ
SORTED_ROWS = N_TOKENS + N_BUCKETS * TR
KEY_CHUNK = 256
TQ_SUB = 256


def _params(n_axes):
    return pltpu.CompilerParams(dimension_semantics=("parallel",) * n_axes,
                                vmem_limit_bytes=VMEM_LIMIT)


def _dot(a, b):
    return jnp.dot(a, b, preferred_element_type=F32)


def _dot_nt(a, b):
    return lax.dot_general(a, b, (((1,), (1,)), ((), ())), preferred_element_type=F32)


def _layernorm(z, g, b):
    mu = jnp.mean(z, axis=-1, keepdims=True)
    zc = z - mu
    var = jnp.mean(zc * zc, axis=-1, keepdims=True)
    return zc * lax.rsqrt(var + LN_EPS) * g + b


def _mod_body(c_ref, w_ref, b_ref, o_ref):
    c = c_ref[...]
    s = (c * jax.nn.sigmoid(c)).astype(BF16)
    o_ref[...] = _dot(s, w_ref[...].astype(BF16)) + b_ref[...]


def _modulation(cond, w_mod, b_mod):
    tn = 1536
    return pl.pallas_call(
        _mod_body,
        out_shape=jax.ShapeDtypeStruct((DEPTH, N_COND, 6 * D_MODEL), F32),
        grid=(DEPTH, 6 * D_MODEL // tn),
        in_specs=[
            pl.BlockSpec((N_COND, D_MODEL), lambda l, j: (0, 0)),
            pl.BlockSpec((None, D_MODEL, tn), lambda l, j: (l, 0, j)),
            pl.BlockSpec((None, 1, tn), lambda l, j: (l, 0, j)),
        ],
        out_specs=pl.BlockSpec((None, N_COND, tn), lambda l, j: (l, 0, j)),
        compiler_params=_params(2),
        name="modulation",
    )(cond, w_mod, b_mod.reshape(DEPTH, 1, 6 * D_MODEL))


def _in_proj_body(rope, x_ref, mod_ref, w_ref, qn_ref, kn_ref, g512_ref, g128_ref, *rest):
    if rope:
        ca_ref, cb_ref, cc_ref = rest[:3]
        outs = rest[3:]
    else:
        outs = rest
    x = x_ref[...]
    sh1 = mod_ref[:, 0:D_MODEL]
    sc1 = mod_ref[:, D_MODEL:2 * D_MODEL]
    h = (x * (1.0 + sc1) + sh1).astype(BF16)

    def proj(off, width):
        return _dot(h, w_ref[:, off:off + width])

    def rms(t, g_ref, wn_ref):
        t2 = t * t
        hi = t2.astype(BF16)
        lo = (t2 - hi.astype(F32)).astype(BF16)
        ms = _dot(hi, g_ref[...]) + _dot(lo, g_ref[...])
        return t * lax.rsqrt(ms + RMS_EPS) * wn_ref[...]

    def rot(t):
        if not rope:
            return t
        ca, cb, cc = ca_ref[...], cb_ref[...], cc_ref[...]
        pieces = []
        for j in range(t.shape[1] // LANES):
            blk = t[:, j * LANES:(j + 1) * LANES]
            pieces.append(blk * ca + pltpu.roll(blk, LANES - 1, 1) * cb + pltpu.roll(blk, 1, 1) * cc)
        return pieces[0] if len(pieces) == 1 else jnp.concatenate(pieces, axis=1)

    qa = rot(proj(OFF_QA, 512)) * QK_SCALE
    ka = rot(proj(OFF_KA, 512))
    va = proj(OFF_VA, 512)
    qb = proj(OFF_QB, 512) * QK_SCALE
    kb = proj(OFF_KB, 512)
    vb = proj(OFF_VB, 512)
    qc = rot(rms(proj(OFF_QC, 512), g512_ref, qn_ref)) * QK_SCALE
    kc = rot(rms(proj(OFF_KC, 128), g128_ref, kn_ref))
    vc = proj(OFF_VC, 128)

    q_ref = outs[0]
    q_ref[:, 0:512] = qa.astype(BF16)
    q_ref[:, 512:1024] = qb.astype(BF16)
    q_ref[:, 1024:1536] = qc.astype(BF16)
    if rope:
        k_ref, v_ref = outs[1:]
        k_ref[:, 0:512] = ka.astype(BF16)
        k_ref[:, 512:1024] = kb.astype(BF16)
        k_ref[:, 1024:1152] = kc.astype(BF16)
        v_ref[:, 0:512] = va.astype(BF16)
        v_ref[:, 512:1024] = vb.astype(BF16)
        v_ref[:, 1024:1152] = vc.astype(BF16)
    else:
        ka_ref, va_ref, kb_ref, vb_ref, kc_ref, vc_ref = outs[1:]
        ka_ref[...] = ka
        va_ref[...] = va
        kb_ref[...] = kb
        vb_ref[...] = vb
        kc_ref[...] = kc
        vc_ref[...] = vc


def _in_proj(x, mod_l, row_of_tile, w_bf, qn, kn, g512, g128, rope_tabs):
    n_tok = x.shape[0]
    n_tiles = n_tok // TM
    rope = rope_tabs is not None
    const = lambda i: (0, 0)
    in_specs = [
        pl.BlockSpec((TM, D_MODEL), lambda i: (i, 0)),
        pl.BlockSpec((None, 1, 6 * D_MODEL), lambda i: (row_of_tile(i), 0, 0)),
        pl.BlockSpec((D_MODEL, IN_WIDTH), const),
        pl.BlockSpec((1, 512), const),
        pl.BlockSpec((1, 128), const),
        pl.BlockSpec((512, 512), const),
        pl.BlockSpec((128, 128), const),
    ]
    args = [x, mod_l, w_bf, qn, kn, g512, g128]
    tile = lambda width: pl.BlockSpec((TM, width), lambda i: (i, 0))
    if rope:
        tiles_per_seq = DEC_SEQ // TM
        in_specs += [pl.BlockSpec((TM, LANES), lambda i: (i % tiles_per_seq, 0))] * 3
        args += list(rope_tabs)
        out_shape = [jax.ShapeDtypeStruct((n_tok, Q_PACK), BF16),
                     jax.ShapeDtypeStruct((n_tok, KV_PACK), BF16),
                     jax.ShapeDtypeStruct((n_tok, KV_PACK), BF16)]
        out_specs = [tile(Q_PACK), tile(KV_PACK), tile(KV_PACK)]
    else:
        widths = (512, 512, 512, 512, 128, 128)
        out_shape = [jax.ShapeDtypeStruct((n_tok, Q_PACK), BF16)] + [
            jax.ShapeDtypeStruct((n_tok, w), F32) for w in widths]
        out_specs = [tile(Q_PACK)] + [tile(w) for w in widths]
    return pl.pallas_call(
        functools.partial(_in_proj_body, rope),
        out_shape=out_shape,
        grid=(n_tiles,),
        in_specs=in_specs,
        out_specs=out_specs,
        compiler_params=_params(1),
        name="in_proj_latent" if rope else "in_proj_context",
    )(*args)


def _head_scores(q_pair, half, k_pair):
    lane = lax.broadcasted_iota(jnp.int32, q_pair.shape, 1)
    keep = (lane < HEAD_DIM) if half == 0 else (lane >= HEAD_DIM)
    qm = jnp.where(keep, q_pair, jnp.zeros_like(q_pair))
    return _dot_nt(qm, k_pair)


def _softmax_parts(s_list):
    mx = functools.reduce(jnp.maximum, [jnp.max(s, axis=-1, keepdims=True) for s in s_list])
    e_list = [jnp.exp2(s - mx) for s in s_list]
    den = functools.reduce(operator.add, [jnp.sum(e, axis=-1, keepdims=True) for e in e_list])
    return e_list, 1.0 / den


def _pair_attention(q_pair, segs):
    outs = []
    for half in (0, 1):
        s_list = []
        for k_pair, _, bias in segs:
            s = _head_scores(q_pair, half, k_pair)
            if bias is not None:
                s = s + bias(half)
            s_list.append(s)
        e_list, inv = _softmax_parts(s_list)
        o = functools.reduce(operator.add, [_dot(e.astype(BF16), seg[1]) for e, seg in zip(e_list, segs)])
        outs.append(o * inv)
    lane = lax.broadcasted_iota(jnp.int32, outs[0].shape, 1)
    return jnp.where(lane < HEAD_DIM, outs[0], outs[1])


def _lambda(lam_ref, layer):
    lam_init = 0.8 - 0.6 * math.exp(-0.3 * layer)
    l = lam_ref[...]
    a = jnp.sum(l[0:1] * l[1:2], axis=-1, keepdims=True)
    b = jnp.sum(l[2:3] * l[3:4], axis=-1, keepdims=True)
    return jnp.exp(a) - jnp.exp(b) + lam_init, lam_init


def _diff_attention(q_blk, k_segs, v_segs, lam_v, lam_init, subln):
    outs = []
    for h in range(H_A):
        j, half = h // 2, h % 2
        parts = []
        for m in range(2):
            qb = q_blk(2 * m + j)
            s_list = [_head_scores(qb, half, ks(2 * m + j)) for ks in k_segs]
            parts.append(_softmax_parts(s_list))
        (e1, inv1), (e2, inv2) = parts
        c2 = lam_v * inv2
        o = functools.reduce(operator.add, [
            _dot((e1[s] * inv1 - e2[s] * c2).astype(BF16), v_segs[s](h)) for s in range(len(k_segs))])
        ms = jnp.mean(o * o, axis=-1, keepdims=True)
        outs.append(o * lax.rsqrt(ms + RMS_EPS) * subln * (1.0 - lam_init))
    return outs


def _tile(ref, j, dtype=None):
    t = ref[:, j * LANES:(j + 1) * LANES]
    return t if dtype is None else t.astype(dtype)


def _ctx_attn_body(layer, q_ref, ka_ref, va_ref, kb_ref, vb_ref, kc_ref, vc_ref, lam_ref, subln_ref,
                   oa_ref, ob_ref, oc_ref):
    lam_v, lam_init = _lambda(lam_ref, layer)
    oa = _diff_attention(lambda j: _tile(q_ref, j),
                         [lambda j: _tile(ka_ref, j, BF16)],
                         [lambda h: _tile(va_ref, h, BF16)],
                         lam_v, lam_init, subln_ref[...])
    for h in range(H_A):
        oa_ref[:, h * LANES:(h + 1) * LANES] = oa[h].astype(BF16)
    for j in range(4):
        o = _pair_attention(_tile(q_ref, 4 + j), [(_tile(kb_ref, j, BF16), _tile(vb_ref, j, BF16), None)])
        ob_ref[:, j * LANES:(j + 1) * LANES] = o.astype(BF16)
    kc = kc_ref[...].astype(BF16)
    vc = vc_ref[...].astype(BF16)
    for g in range(4):
        o = _pair_attention(_tile(q_ref, 8 + g), [(kc, vc, None)])
        oc_ref[:, g * LANES:(g + 1) * LANES] = o.astype(BF16)


def _ctx_attention(layer, q, ka, va, kb, vb, kc, vc, lam_l, subln_l):
    n_tok = q.shape[0]
    tile = lambda width: pl.BlockSpec((SEQ, width), lambda b: (b, 0))
    const = lambda b: (0, 0)
    return pl.pallas_call(
        functools.partial(_ctx_attn_body, layer),
        out_shape=[jax.ShapeDtypeStruct((n_tok, BRANCH_W), BF16)] * 3,
        grid=(n_tok // SEQ,),
        in_specs=[tile(Q_PACK), tile(512), tile(512), tile(512), tile(512), tile(128), tile(128),
                  pl.BlockSpec((4, HEAD_DIM), const), pl.BlockSpec((1, 2 * HEAD_DIM), const)],
        out_specs=[tile(BRANCH_W)] * 3,
        compiler_params=_params(1),
        name="context_attention",
    )(q, ka, va, kb, vb, kc, vc, lam_l, subln_l)


N_KEYS = PAST_LEN + DEC_SEQ
N_SUB = DEC_SEQ // TQ_SUB


def _sub_rows(i):
    return pl.ds(pl.multiple_of(i * TQ_SUB, TQ_SUB), TQ_SUB)


def _join_keys(all_ref, ctx_ref, lat_ref):
    all_ref[0:PAST_LEN, :] = ctx_ref[...].astype(BF16)
    all_ref[PAST_LEN:N_KEYS, :] = lat_ref[...]


def _join_values_t(vt_ref, ctx_ref, lat_ref):
    vt_ref[:, 0:PAST_LEN] = ctx_ref[...].T.astype(BF16)
    vt_ref[:, PAST_LEN:N_KEYS] = lat_ref[...].astype(F32).T.astype(BF16)


def _softmax_t(k_ref, q, keep):
    qm = jnp.where(keep, q, jnp.zeros_like(q))
    pieces = [_dot_nt(k_ref[c * KEY_CHUNK:(c + 1) * KEY_CHUNK, :], qm) for c in range(N_KEYS // KEY_CHUNK)]

    def over_keys(x, op):
        return op(op(x.reshape(KEY_CHUNK // 8, 8, x.shape[1]), axis=0), axis=0, keepdims=True)

    m = over_keys(functools.reduce(jnp.maximum, pieces), jnp.max)
    es = [jnp.exp2(s - m) for s in pieces]
    return es, 1.0 / over_keys(functools.reduce(operator.add, es), jnp.sum)


def _values_t(vt, weights):
    return functools.reduce(operator.add, [
        _dot(vt[:, c * KEY_CHUNK:(c + 1) * KEY_CHUNK], w.astype(BF16)) for c, w in enumerate(weights)])


def _lat_diff_body(layer, q1_ref, q2_ref, k1l_ref, k2l_ref, vl_ref, k1c_ref, k2c_ref, vc_ref, lam_ref,
                   subln_ref, o_ref, k1_ref, k2_ref, vt_ref):
    _join_keys(k1_ref, k1c_ref, k1l_ref)
    _join_keys(k2_ref, k2c_ref, k2l_ref)
    _join_values_t(vt_ref, vc_ref, vl_ref)
    lam_v, lam_init = _lambda(lam_ref, layer)
    subln = subln_ref[...] * (1.0 - lam_init)
    lo = (pl.program_id(1) % 2) * HEAD_DIM
    lane = lax.broadcasted_iota(jnp.int32, (TQ_SUB, LANES), 1)
    keep = (lane >= lo) & (lane < lo + HEAD_DIM)

    def sub_tile(i, carry):
        e1, inv1 = _softmax_t(k1_ref, q1_ref[_sub_rows(i), :], keep)
        e2, inv2 = _softmax_t(k2_ref, q2_ref[_sub_rows(i), :], keep)
        c2 = lam_v * inv2
        o = _values_t(vt_ref, [a * inv1 - b * c2 for a, b in zip(e1, e2)]).T
        ms = jnp.mean(o * o, axis=-1, keepdims=True)
        o_ref[_sub_rows(i), :] = (o * lax.rsqrt(ms + RMS_EPS) * subln).astype(BF16)
        return carry

    lax.fori_loop(0, N_SUB, sub_tile, 0, unroll=2)


def _lat_diff_attention(layer, q, k, v, cache_k, cache_v, lam_l, subln_l):
    const = lambda b, h: (0, 0)
    seq = lambda col: pl.BlockSpec((DEC_SEQ, LANES), lambda b, h: (b, col(h)))
    past = lambda col: pl.BlockSpec((None, None, PAST_LEN, LANES), lambda b, h: (b, layer, 0, col(h)))
    map1 = lambda h: h // 2
    map2 = lambda h: 2 + h // 2
    head = lambda h: h
    return pl.pallas_call(
        functools.partial(_lat_diff_body, layer),
        out_shape=jax.ShapeDtypeStruct((DEC_BATCH * DEC_SEQ, BRANCH_W), BF16),
        grid=(DEC_BATCH, H_A),
        in_specs=[seq(map1), seq(map2), seq(map1), seq(map2), seq(head),
                  past(map1), past(map2), past(head),
                  pl.BlockSpec((4, HEAD_DIM), const), pl.BlockSpec((1, 2 * HEAD_DIM), const)],
        out_specs=seq(head),
        scratch_shapes=[pltpu.VMEM((N_KEYS, LANES), BF16)] * 2 + [pltpu.VMEM((LANES, N_KEYS), BF16)],
        compiler_params=_params(2),
        name="latent_diff_attention",
    )(q, q, k, k, v, cache_k, cache_k, cache_v, lam_l, subln_l)


def _lat_gqa_body(q_ref, kl_ref, vl_ref, kc_ref, vc_ref, o_ref, k_ref, vt_ref):
    _join_keys(k_ref, kc_ref, kl_ref)
    _join_values_t(vt_ref, vc_ref, vl_ref)
    lane = lax.broadcasted_iota(jnp.int32, (TQ_SUB, LANES), 1)

    def sub_tile(i, carry):
        q = q_ref[_sub_rows(i), :]
        outs = []
        for n in range(KV_C):
            keep = (lane < HEAD_DIM) if n == 0 else (lane >= HEAD_DIM)
            es, inv = _softmax_t(k_ref, q, keep)
            outs.append(_values_t(vt_ref.at[n * HEAD_DIM:(n + 1) * HEAD_DIM], es) * inv)
        o_ref[_sub_rows(i), :] = jnp.concatenate(outs, axis=0).T.astype(BF16)
        return carry

    lax.fori_loop(0, N_SUB, sub_tile, 0, unroll=2)


def _lat_gqa_attention(layer, q, k, v, cache_k, cache_v):
    n_pairs = H_C // KV_C
    kv_col = 1024 // LANES
    past = pl.BlockSpec((None, None, PAST_LEN, LANES), lambda b, g: (b, layer, 0, 0))
    return pl.pallas_call(
        _lat_gqa_body,
        out_shape=jax.ShapeDtypeStruct((DEC_BATCH * DEC_SEQ, BRANCH_W), BF16),
        grid=(DEC_BATCH, n_pairs),
        in_specs=[
            pl.BlockSpec((DEC_SEQ, LANES), lambda b, g: (b, 1024 // LANES + g)),
            pl.BlockSpec((DEC_SEQ, LANES), lambda b, g: (b, kv_col)),
            pl.BlockSpec((DEC_SEQ, LANES), lambda b, g: (b, kv_col)),
            past, past,
        ],
        out_specs=pl.BlockSpec((DEC_SEQ, LANES), lambda b, g: (b, g)),
        scratch_shapes=[pltpu.VMEM((N_KEYS, LANES), BF16), pltpu.VMEM((LANES, N_KEYS), BF16)],
        compiler_params=_params(2),
        name="latent_gqa_attention",
    )(q, k, v, cache_k, cache_v)


def _nbr_key_start(g):
    return jnp.clip(g * NBR_ROWS - NA_KH // 2, 0, GRID_H - NBR_KEY_ROWS)


def _lat_nbr_body(q_ref, kl_ref, vl_ref, kc_ref, vc_ref, bias_ref, o_ref):
    base = pl.multiple_of(_nbr_key_start(pl.program_id(1)) * GRID_W, GRID_W)
    n_loc = NBR_KEY_ROWS * GRID_W
    for j in range(4):
        k_loc = kl_ref[pl.ds(base, n_loc), j * LANES:(j + 1) * LANES]
        v_loc = vl_ref[pl.ds(base, n_loc), j * LANES:(j + 1) * LANES]
        segs = [(k_loc, v_loc, lambda half, j=j: bias_ref[2 * j + half]),
                (_tile(kc_ref, j, BF16), _tile(vc_ref, j, BF16), None)]
        o = _pair_attention(_tile(q_ref, j), segs)
        o_ref[:, j * LANES:(j + 1) * LANES] = o.astype(BF16)


def _lat_nbr_attention(layer, q, k, v, cache_k, cache_v, bias_tab):
    n_groups = GRID_H // NBR_ROWS
    tq = NBR_ROWS * GRID_W
    case = lambda g: jnp.where(g == 0, 0, jnp.where(g == n_groups - 1, 2, 1))
    return pl.pallas_call(
        _lat_nbr_body,
        out_shape=jax.ShapeDtypeStruct((DEC_BATCH * DEC_SEQ, BRANCH_W), BF16),
        grid=(DEC_BATCH, n_groups),
        in_specs=[
            pl.BlockSpec((tq, 512), lambda b, g: (b * n_groups + g, 1)),
            pl.BlockSpec((DEC_SEQ, 512), lambda b, g: (b, 1)),
            pl.BlockSpec((DEC_SEQ, 512), lambda b, g: (b, 1)),
            pl.BlockSpec((None, None, PAST_LEN, 512), lambda b, g: (b, layer, 0, 0)),
            pl.BlockSpec((None, None, PAST_LEN, 512), lambda b, g: (b, layer, 0, 0)),
            pl.BlockSpec((None, H_B, tq, NBR_KEY_ROWS * GRID_W), lambda b, g: (case(g), 0, 0, 0)),
        ],
        out_specs=pl.BlockSpec((tq, BRANCH_W), lambda b, g: (b * n_groups + g, 0)),
        compiler_params=_params(2),
        name="latent_neighbourhood_attention",
    )(q, k, v, cache_k, cache_v, bias_tab)


def _merge_body(alpha, x_ref, oa_ref, ob_ref, oc_ref, mod_ref, wg_ref, bg_ref, wbr_ref, wo_ref,
                lng_ref, lnb_ref, out_ref):
    x = x_ref[...]
    sh1 = mod_ref[:, 0:D_MODEL]
    sc1 = mod_ref[:, D_MODEL:2 * D_MODEL]
    g1 = mod_ref[:, 2 * D_MODEL:3 * D_MODEL]
    h = (x * (1.0 + sc1) + sh1).astype(BF16)
    acc = None
    for i, o_ref in enumerate((oa_ref, ob_ref, oc_ref)):
        cols = slice(i * D_MODEL, (i + 1) * D_MODEL)
        gate = jax.nn.sigmoid(_dot(h, wg_ref[:, cols]) + bg_ref[:, cols])
        term = gate * _dot(o_ref[...], wbr_ref[i])
        acc = term if acc is None else acc + term
    y = _dot(acc.astype(BF16), wo_ref[...])
    out_ref[...] = _layernorm(alpha * x + g1 * y, lng_ref[...], lnb_ref[...])


def _merge(alpha, x, oa, ob, oc, mod_l, row_of_tile, wg, bg, wbr, wo, lng, lnb):
    n_tok = x.shape[0]
    const = lambda i: (0, 0)
    tile = lambda width: pl.BlockSpec((TM, width), lambda i: (i, 0))
    return pl.pallas_call(
        functools.partial(_merge_body, alpha),
        out_shape=jax.ShapeDtypeStruct((n_tok, D_MODEL), F32),
        grid=(n_tok // TM,),
        in_specs=[
            tile(D_MODEL), tile(BRANCH_W), tile(BRANCH_W), tile(BRANCH_W),
            pl.BlockSpec((None, 1, 6 * D_MODEL), lambda i: (row_of_tile(i), 0, 0)),
            pl.BlockSpec((D_MODEL, 3 * D_MODEL), const),
            pl.BlockSpec((1, 3 * D_MODEL), const),
            pl.BlockSpec((3, BRANCH_W, D_MODEL), lambda i: (0, 0, 0)),
            pl.BlockSpec((D_MODEL, D_MODEL), const),
            pl.BlockSpec((1, D_MODEL), const),
            pl.BlockSpec((1, D_MODEL), const),
        ],
        out_specs=tile(D_MODEL),
        compiler_params=_params(1),
        name="branch_merge",
    )(x, oa, ob, oc, mod_l, wg, bg, wbr, wo, lng, lnb)


def _routing(logits, bias):
    scores = jax.nn.sigmoid(logits)
    sel_all = scores + bias
    sel = [sel_all[e:e + 1] for e in range(N_EXPERTS)]
    gscore = []
    for g in range(N_GROUPS):
        a, b, c, d = sel[4 * g:4 * g + 4]
        hi1, lo1 = jnp.maximum(a, b), jnp.minimum(a, b)
        hi2, lo2 = jnp.maximum(c, d), jnp.minimum(c, d)
        gscore.append(jnp.maximum(hi1, hi2) + jnp.maximum(jnp.minimum(hi1, hi2), jnp.maximum(lo1, lo2)))
    best = []
    taken = None
    for g in range(N_GROUPS):
        ok = functools.reduce(operator.and_, [gscore[g] >= gscore[k] for k in range(g + 1, N_GROUPS)],
                              jnp.ones_like(gscore[g], dtype=jnp.bool_))
        if taken is not None:
            ok = ok & jnp.logical_not(taken)
        best.append(ok)
        taken = ok if taken is None else (taken | ok)
    picked = []
    for e in range(N_EXPERTS):
        g, i = divmod(e, EXPERTS_PER_GROUP)
        rank = jnp.zeros_like(sel[0])
        for k in range(EXPERTS_PER_GROUP):
            if k == i:
                continue
            other = sel[4 * g + k]
            ahead = (other >= sel[e]) if k < i else (other > sel[e])
            rank = rank + jnp.where(ahead, 1.0, 0.0)
        picked.append((rank < 2.0) & best[g])
    return [picked[4 * g + i] & picked[4 * g + j] for g in range(N_GROUPS) for i, j in GROUP_PAIRS]


def _split_bf16(v):
    hi = v.astype(BF16)
    return hi, (v - hi.astype(F32)).astype(BF16)


TILE_ROWS = D_MODEL // LANES


def _store_token_tiles(ref, val):
    for t0 in range(0, val.shape[0], 8):
        for j in range(TILE_ROWS):
            ref[pl.ds(t0 * TILE_ROWS + j, 8, stride=TILE_ROWS), :] = val[t0:t0 + 8, j * LANES:(j + 1) * LANES]


def _load_token_tiles(ref, n):
    rows = [jnp.concatenate([ref[pl.ds(t0 * TILE_ROWS + j, 8, stride=TILE_ROWS), :] for j in range(TILE_ROWS)],
                            axis=1) for t0 in range(0, n, 8)]
    return jnp.concatenate(rows, axis=0)


def _route_body(x_ref, mod_ref, wrt_ref, rb_ref, tri_ref, cnt_in_ref, h2_ref, meta_ref, cnt_ref):
    @pl.when(pl.program_id(0) == 0)
    def _():
        cnt_ref[...] = cnt_in_ref[...]

    x = x_ref[...]
    sh2 = mod_ref[:, 3 * D_MODEL:4 * D_MODEL]
    sc2 = mod_ref[:, 4 * D_MODEL:5 * D_MODEL]
    h2 = x * (1.0 + sc2) + sh2
    _store_token_tiles(h2_ref, h2)
    hi, lo = _split_bf16(h2)
    wr_hi, wr_lo = _split_bf16(wrt_ref[...])
    logits = _dot_nt(wr_hi, hi) + _dot_nt(wr_hi, lo) + _dot_nt(wr_lo, hi)
    masks = _routing(logits, rb_ref[...])
    tm = x.shape[0]
    onehot = jnp.concatenate([jnp.where(m, 1.0, 0.0) for m in masks]
                             + [jnp.zeros((BUCKET_ROWS - N_BUCKETS, tm), F32)], axis=0)
    ahead = _dot(onehot.astype(BF16), tri_ref[...]) + cnt_ref[:, 0:1]
    rank = jnp.sum(onehot * ahead, axis=0, keepdims=True)
    bucket_id = lax.broadcasted_iota(jnp.int32, onehot.shape, 0).astype(F32)
    bucket = jnp.sum(onehot * bucket_id, axis=0, keepdims=True)
    meta_ref[...] = jnp.concatenate([bucket, rank, jnp.zeros((6, tm), F32)], axis=0).astype(jnp.int32)
    cnt_ref[...] = cnt_ref[...] + jnp.sum(onehot, axis=1, keepdims=True)


def _route(x, mod_l, row_of_tile, wrt, rb, tri, cnt_in):
    n_tok = x.shape[0]
    const = lambda i: (0, 0)
    tile = pl.BlockSpec((TM, D_MODEL), lambda i: (i, 0))
    return pl.pallas_call(
        _route_body,
        out_shape=[jax.ShapeDtypeStruct((n_tok * TILE_ROWS, LANES), F32),
                   jax.ShapeDtypeStruct((8, n_tok), jnp.int32),
                   jax.ShapeDtypeStruct((BUCKET_ROWS, LANES), F32)],
        grid=(n_tok // TM,),
        in_specs=[
            tile,
            pl.BlockSpec((None, 1, 6 * D_MODEL), lambda i: (row_of_tile(i), 0, 0)),
            pl.BlockSpec((N_EXPERTS, D_MODEL), const),
            pl.BlockSpec((N_EXPERTS, 1), const),
            pl.BlockSpec((TM, TM), const),
            pl.BlockSpec((BUCKET_ROWS, LANES), const),
        ],
        out_specs=[pl.BlockSpec((TM * TILE_ROWS, LANES), lambda i: (i, 0)),
                   pl.BlockSpec((8, TM), lambda i: (0, i)), pl.BlockSpec((BUCKET_ROWS, LANES), const)],
        compiler_params=pltpu.CompilerParams(dimension_semantics=("arbitrary",), vmem_limit_bytes=VMEM_LIMIT),
        name="moe_route",
    )(x, mod_l, wrt, rb, tri, cnt_in)


def _row_copy(src_ref, src_row, dst_ref, dst_row, sem):
    src = src_ref.at[pl.ds(pl.multiple_of(src_row * TILE_ROWS, TILE_ROWS), TILE_ROWS)]
    dst = dst_ref.at[pl.ds(pl.multiple_of(dst_row * TILE_ROWS, TILE_ROWS), TILE_ROWS)]
    return pltpu.make_async_copy(src, dst, sem)


def _dest_row(base_ref, bucket_ref, rank_ref, token):
    return base_ref[bucket_ref[token]] + rank_ref[token]


def _scatter_body(base_ref, bucket_ref, rank_ref, h_ref, xs_in_ref, xs_ref, sem):
    del xs_in_ref
    first = pl.program_id(0) * TM

    def issue(t, carry):
        _row_copy(h_ref, t, xs_ref, _dest_row(base_ref, bucket_ref, rank_ref, first + t), sem).start()
        return carry

    def drain(t, carry):
        _row_copy(h_ref, t, xs_ref, 0, sem).wait()
        return carry

    lax.fori_loop(0, TM, issue, 0, unroll=8)
    lax.fori_loop(0, TM, drain, 0, unroll=8)


def _scatter(base, bucket, rank, h2, xs):
    n_tok = h2.shape[0] // TILE_ROWS
    return pl.pallas_call(
        _scatter_body,
        out_shape=jax.ShapeDtypeStruct(xs.shape, xs.dtype),
        grid_spec=pltpu.PrefetchScalarGridSpec(
            num_scalar_prefetch=3,
            grid=(n_tok // TM,),
            in_specs=[pl.BlockSpec((TM * TILE_ROWS, LANES), lambda i, *_: (i, 0)),
                      pl.BlockSpec(memory_space=pl.ANY)],
            out_specs=pl.BlockSpec(memory_space=pl.ANY),
            scratch_shapes=[pltpu.SemaphoreType.DMA(())],
        ),
        input_output_aliases={4: 0},
        compiler_params=pltpu.CompilerParams(dimension_semantics=("arbitrary",), vmem_limit_bytes=VMEM_LIMIT),
        name="moe_scatter",
    )(base, bucket, rank, h2, xs)


def _experts_body(e0_ref, e1_ref, used_ref, xs_ref, wrt_ref, w13a_ref, w2a_ref, w13b_ref, w2b_ref, y_ref):
    j = pl.program_id(0)

    @pl.when(j < used_ref[0])
    def _():
        hi, lo = _split_bf16(_load_token_tiles(xs_ref, TR))
        wr = jnp.concatenate([wrt_ref[pl.ds(e0_ref[j], 1), :], wrt_ref[pl.ds(e1_ref[j], 1), :],
                              jnp.zeros((6, D_MODEL), F32)], axis=0)
        wr_hi, wr_lo = _split_bf16(wr)
        logits = _dot_nt(hi, wr_hi) + _dot_nt(lo, wr_hi) + _dot_nt(hi, wr_lo)
        s = jax.nn.sigmoid(logits)
        s0 = s[:, 0:1]
        s1 = s[:, 1:2]
        inv = 1.0 / (s0 + s1)

        def ffn(w13_ref, w2_ref):
            h13 = _dot(hi, w13_ref[...])
            a = h13[:, :D_EXPERT]
            b = h13[:, D_EXPERT:]
            return _dot((a * jax.nn.sigmoid(a) * b).astype(BF16), w2_ref[...])

        _store_token_tiles(y_ref, (s0 * inv) * ffn(w13a_ref, w2a_ref) + (s1 * inv) * ffn(w13b_ref, w2b_ref))

    @pl.when(j >= used_ref[0])
    def _():
        y_ref[...] = jnp.zeros_like(y_ref)


def _experts(tile_e0, tile_e1, n_used, xs, wrt, w13, w2):
    n_rows = xs.shape[0] // TILE_ROWS
    tile = pl.BlockSpec((TR * TILE_ROWS, LANES), lambda j, *_: (j, 0))
    return pl.pallas_call(
        _experts_body,
        out_shape=jax.ShapeDtypeStruct(xs.shape, F32),
        grid_spec=pltpu.PrefetchScalarGridSpec(
            num_scalar_prefetch=3,
            grid=(n_rows // TR,),
            in_specs=[
                tile,
                pl.BlockSpec((N_EXPERTS, D_MODEL), lambda j, *_: (0, 0)),
                pl.BlockSpec((None, D_MODEL, 2 * D_EXPERT), lambda j, e0, e1, u: (e0[j], 0, 0)),
                pl.BlockSpec((None, D_EXPERT, D_MODEL), lambda j, e0, e1, u: (e0[j], 0, 0)),
                pl.BlockSpec((None, D_MODEL, 2 * D_EXPERT), lambda j, e0, e1, u: (e1[j], 0, 0)),
                pl.BlockSpec((None, D_EXPERT, D_MODEL), lambda j, e0, e1, u: (e1[j], 0, 0)),
            ],
            out_specs=tile,
        ),
        compiler_params=pltpu.CompilerParams(dimension_semantics=("arbitrary",), vmem_limit_bytes=VMEM_LIMIT),
        name="moe_experts",
    )(tile_e0, tile_e1, n_used, xs, wrt, w13, w2, w13, w2)


def _gather_norm_body(alpha, base_ref, bucket_ref, rank_ref, x_ref, mod_ref, y_ref, lng_ref, lnb_ref,
                      out_ref, m_ref, sem):
    i = pl.program_id(0)

    def fetch(step, slot):
        def issue(t, carry):
            src_row = _dest_row(base_ref, bucket_ref, rank_ref, step * TM + t)
            _row_copy(y_ref, src_row, m_ref.at[slot], t, sem.at[slot]).start()
            return carry

        lax.fori_loop(0, TM, issue, 0, unroll=8)

    @pl.when(i == 0)
    def _():
        fetch(0, 0)

    @pl.when(i + 1 < pl.num_programs(0))
    def _():
        fetch(i + 1, (i + 1) % 2)

    slot = i % 2

    def drain(t, carry):
        _row_copy(y_ref, 0, m_ref.at[slot], t, sem.at[slot]).wait()
        return carry

    lax.fori_loop(0, TM, drain, 0, unroll=8)
    g2 = mod_ref[:, 5 * D_MODEL:6 * D_MODEL]
    m = _load_token_tiles(m_ref.at[slot], TM)
    out_ref[...] = _layernorm(alpha * x_ref[...] + g2 * m, lng_ref[...], lnb_ref[...])


def _gather_norm(alpha, base, bucket, rank, x, mod_l, row_of_tile, y, lng, lnb):
    n_tok = x.shape[0]
    const = lambda i, *_: (0, 0)
    tile = pl.BlockSpec((TM, D_MODEL), lambda i, *_: (i, 0))
    return pl.pallas_call(
        functools.partial(_gather_norm_body, alpha),
        out_shape=jax.ShapeDtypeStruct((n_tok, D_MODEL), F32),
        grid_spec=pltpu.PrefetchScalarGridSpec(
            num_scalar_prefetch=3,
            grid=(n_tok // TM,),
            in_specs=[
                tile,
                pl.BlockSpec((None, 1, 6 * D_MODEL), lambda i, *_: (row_of_tile(i), 0, 0)),
                pl.BlockSpec(memory_space=pl.ANY),
                pl.BlockSpec((1, D_MODEL), const),
                pl.BlockSpec((1, D_MODEL), const),
            ],
            out_specs=tile,
            scratch_shapes=[pltpu.VMEM((2, TM * TILE_ROWS, LANES), F32), pltpu.SemaphoreType.DMA((2,))],
        ),
        compiler_params=pltpu.CompilerParams(dimension_semantics=("arbitrary",), vmem_limit_bytes=VMEM_LIMIT),
        name="moe_gather_norm",
    )(base, bucket, rank, x, mod_l, y, lng, lnb)


def _bucket_layout(counts):
    cnt = counts[:N_BUCKETS, 0].astype(jnp.int32)
    padded = (cnt + TR - 1) // TR * TR
    ends = jnp.cumsum(padded)
    base = jnp.zeros((BUCKET_ROWS,), jnp.int32).at[:N_BUCKETS].set(ends - padded)
    tile_start = jnp.arange(SORTED_ROWS // TR, dtype=jnp.int32) * TR
    tile_bucket = jnp.minimum(jnp.sum(tile_start[:, None] >= ends[None, :], axis=1), N_BUCKETS - 1)
    first = jnp.array([4 * g + i for g in range(N_GROUPS) for i, _ in GROUP_PAIRS], jnp.int32)
    second = jnp.array([4 * g + j for g in range(N_GROUPS) for _, j in GROUP_PAIRS], jnp.int32)
    return base, first[tile_bucket], second[tile_bucket], (ends[-1:] // TR).astype(jnp.int32)


def _rope_tables():
    t = jnp.arange(DEC_SEQ)
    row = (t // GRID_W).astype(F32)
    col = (t % GRID_W).astype(F32)
    n = HEAD_DIM // 4
    freqs = ROPE_THETA ** (-jnp.arange(n, dtype=F32) / n)
    ang = jnp.concatenate([row[:, None] * freqs, col[:, None] * freqs], axis=-1)
    cos = jnp.tile(jnp.repeat(jnp.cos(ang), 2, axis=-1), (1, LANES // HEAD_DIM))
    sin = jnp.tile(jnp.repeat(jnp.sin(ang), 2, axis=-1), (1, LANES // HEAD_DIM))
    even = (jnp.arange(LANES) % 2 == 0)[None, :]
    return cos, jnp.where(even, -sin, 0.0), jnp.where(even, 0.0, sin)


def _group_mean_matrix(width):
    g = jnp.arange(width) // HEAD_DIM
    return jnp.where(g[:, None] == g[None, :], 1.0 / HEAD_DIM, 0.0).astype(BF16)


def _neighbourhood_bias(rpb_l):
    cx = jnp.arange(GRID_W)
    col_start = jnp.clip(cx - NA_KW // 2, 0, GRID_W - NA_KW)
    kx = jnp.arange(GRID_W)
    inside_col = (kx[None, :] >= col_start[:, None]) & (kx[None, :] < col_start[:, None] + NA_KW)
    dx = jnp.clip(kx[None, :] - cx[:, None] + (NA_KW - 1), 0, 2 * NA_KW - 2)
    exact = lax.Precision.HIGHEST
    pick_dx = ((dx[:, :, None] == jnp.arange(2 * NA_KW - 1)) & inside_col[:, :, None]).astype(F32)
    n_groups = GRID_H // NBR_ROWS
    i = jnp.arange(NBR_KEY_ROWS)
    tabs = []
    for g in (0, 1, n_groups - 1):
        key_start = min(max(g * NBR_ROWS - NA_KH // 2, 0), GRID_H - NBR_KEY_ROWS)
        r = g * NBR_ROWS + jnp.arange(NBR_ROWS)
        row_start = jnp.clip(r - NA_KH // 2, 0, GRID_H - NA_KH)
        ky = key_start + i
        inside_row = (ky[None, :] >= row_start[:, None]) & (ky[None, :] < row_start[:, None] + NA_KH)
        dy = jnp.clip(ky[None, :] - r[:, None] + (NA_KH - 1), 0, 2 * NA_KH - 2)
        pick_dy = ((dy[:, :, None] == jnp.arange(2 * NA_KH - 1)) & inside_row[:, :, None]).astype(F32)
        by_row = jnp.einsum("hyd,jiy->hjid", rpb_l, pick_dy, precision=exact)
        vals = jnp.einsum("hjid,ckd->hjcik", by_row, pick_dx, precision=exact)
        ok = inside_row[:, None, :, None] & inside_col[None, :, None, :]
        tabs.append(jnp.where(ok[None], vals * LOG2E, NEG_BIAS)
                    .reshape(H_B, NBR_ROWS * GRID_W, NBR_KEY_ROWS * GRID_W))
    return jnp.stack(tabs, axis=0)


def _gqa_regroup(w, axis):
    shape = w.shape
    w = w.reshape(shape[:axis] + (KV_C, H_C // KV_C, HEAD_DIM) + shape[axis + 1:])
    return jnp.swapaxes(w, axis, axis + 1).reshape(shape)


def kernel(x_prompt, x_sample, c, cache_a_k, cache_a_v, cache_b_k, cache_b_v, cache_c_k, cache_c_v,
           c_ctx, w_mod, b_mod, w_in, w_gate, b_gate, lam, a_subln, rpb, c_qnorm, c_knorm,
           w_br, w_o, ln_g, ln_b, w_router, router_bias, moe_w1, moe_w3, moe_w2):
    alpha = (2.0 * DEPTH) ** 0.25
    cond = jnp.concatenate([c_ctx[None, :], c, jnp.zeros((N_COND - 1 - DEC_BATCH, D_MODEL), F32)], axis=0)
    mods = _modulation(cond, w_mod, b_mod).reshape(DEPTH, N_COND, 1, 6 * D_MODEL)

    rope_tabs = _rope_tables()
    g512 = _group_mean_matrix(512)
    g128 = _group_mean_matrix(128)
    wrt = w_router.T
    rb = router_bias.reshape(N_EXPERTS, 1)
    tok = jnp.arange(TM)
    tri = (tok[:, None] < tok[None, :]).astype(BF16)

    ck_a = cache_a_k.reshape(DEC_BATCH, DEPTH, PAST_LEN, 512)
    cv_a = cache_a_v.reshape(DEC_BATCH, DEPTH, PAST_LEN, 512)
    ck_b = cache_b_k.reshape(DEC_BATCH, DEPTH, PAST_LEN, 512)
    cv_b = cache_b_v.reshape(DEC_BATCH, DEPTH, PAST_LEN, 512)
    ck_c = cache_c_k.reshape(DEC_BATCH, DEPTH, PAST_LEN, 128)
    cv_c = cache_c_v.reshape(DEC_BATCH, DEPTH, PAST_LEN, 128)

    ctx_row = lambda i: 0
    lat_tiles = DEC_SEQ // TM
    lat_row = lambda i: 1 + i // lat_tiles

    xp = x_prompt.reshape(BATCH * SEQ, D_MODEL)
    xs = x_sample.reshape(DEC_BATCH * DEC_SEQ, D_MODEL)
    new_kv = []
    for l in range(DEPTH):
        w_in_l = w_in[l]
        w_in_l = jnp.concatenate([w_in_l[:, :OFF_QC], _gqa_regroup(w_in_l[:, OFF_QC:OFF_KC], 1),
                                  w_in_l[:, OFF_KC:]], axis=1).astype(BF16)
        qn = jnp.tile(c_qnorm[l], 512 // HEAD_DIM)[None, :]
        kn = jnp.tile(c_knorm[l], 128 // HEAD_DIM)[None, :]
        wg = w_gate[l].astype(BF16)
        bg = b_gate[l][None, :]
        wbr = jnp.stack([w_br[l, 0], w_br[l, 1], _gqa_regroup(w_br[l, 2], 0)], axis=0).astype(BF16)
        wo = w_o[l].astype(BF16)
        w13 = jnp.concatenate([moe_w1[l], moe_w3[l]], axis=-1).astype(BF16)
        w2 = moe_w2[l].astype(BF16)
        lam_l = lam[l]
        subln_l = a_subln[l][None, :]
        bias_tab = _neighbourhood_bias(rpb[l])
        mod_l = mods[l]

        q, ka, va, kb, vb, kc, vc = _in_proj(xp, mod_l, ctx_row, w_in_l, qn, kn, g512, g128, None)
        new_kv.append((ka, va, kb, vb, kc, vc))
        oa, ob, oc = _ctx_attention(l, q, ka, va, kb, vb, kc, vc, lam_l, subln_l)
        xp = _merge(alpha, xp, oa, ob, oc, mod_l, ctx_row, wg, bg, wbr, wo, ln_g[l, 0:1], ln_b[l, 0:1])

        q, k, v = _in_proj(xs, mod_l, lat_row, w_in_l, qn, kn, g512, g128, rope_tabs)
        oa = _lat_diff_attention(l, q, k, v, ck_a, cv_a, lam_l, subln_l)
        ob = _lat_nbr_attention(l, q, k, v, ck_b, cv_b, bias_tab)
        oc = _lat_gqa_attention(l, q, k, v, ck_c, cv_c)
        xs = _merge(alpha, xs, oa, ob, oc, mod_l, lat_row, wg, bg, wbr, wo, ln_g[l, 0:1], ln_b[l, 0:1])

        h2p, meta_p, counts = _route(xp, mod_l, ctx_row, wrt, rb, tri, jnp.zeros((BUCKET_ROWS, LANES), F32))
        h2s, meta_s, counts = _route(xs, mod_l, lat_row, wrt, rb, tri, counts)
        base, tile_e0, tile_e1, n_used = _bucket_layout(counts)
        rows = jnp.zeros((SORTED_ROWS * TILE_ROWS, LANES), F32)
        rows = _scatter(base, meta_p[0], meta_p[1], h2p, rows)
        rows = _scatter(base, meta_s[0], meta_s[1], h2s, rows)
        y = _experts(tile_e0, tile_e1, n_used, rows, wrt, w13, w2)
        xp = _gather_norm(alpha, base, meta_p[0], meta_p[1], xp, mod_l, ctx_row, y, ln_g[l, 1:2], ln_b[l, 1:2])
        xs = _gather_norm(alpha, base, meta_s[0], meta_s[1], xs, mod_l, lat_row, y, ln_g[l, 1:2], ln_b[l, 1:2])

    def cache(idx, shape):
        return jnp.stack([new_kv[l][idx].reshape(BATCH, SEQ, *shape) for l in range(DEPTH)], axis=1)

    return (xp.reshape(BATCH, SEQ, D_MODEL), xs.reshape(DEC_BATCH, DEC_SEQ, D_MODEL),
            cache(0, (2, H_A, HEAD_DIM)), cache(1, (H_A, 2 * HEAD_DIM)),
            cache(2, (H_B, HEAD_DIM)), cache(3, (H_B, HEAD_DIM)),
            cache(4, (KV_C, HEAD_DIM)), cache(5, (KV_C, HEAD_DIM)))
```

```python
import functools
import math
import operator

import jax
import jax.numpy as jnp
from jax import lax
from jax.experimental import pallas as pl
from jax.experimental.pallas import tpu as pltpu

F32 = jnp.float32
BF16 = jnp.bfloat16

D_MODEL = 1024
BATCH = 32
SEQ = 256
DEPTH = 2
DEC_BATCH = 4
DEC_SEQ = 4096
PAST_LEN = 256
GRID_W = 64
GRID_H = DEC_SEQ // GRID_W
HEAD_DIM = 64
H_A = 4
H_B = 8
H_C = 8
KV_C = 2
NA_KH = 8
NA_KW = 16
ROPE_THETA = 10000.0
N_EXPERTS = 16
N_GROUPS = 4
EXPERTS_PER_GROUP = N_EXPERTS // N_GROUPS
D_EXPERT = 256
BRANCH_W = 512
LN_EPS = 1e-5
RMS_EPS = 1e-6
LOG2E = math.log2(math.e)
QK_SCALE = HEAD_DIM ** -0.5 * LOG2E
NEG_BIAS = -1e30
NBR_ROWS = 4
NBR_KEY_ROWS = 12

LANES = 128
N_COND = 8
VMEM_LIMIT = 56 * 1024 * 1024

OFF_QA, OFF_KA, OFF_VA, OFF_QB, OFF_KB, OFF_VB, OFF_QC, OFF_KC, OFF_VC = (
    0, 512, 1024, 1536, 2048, 2560, 3072, 3584, 3712)
IN_WIDTH = 3840
CACHE_WIDTHS = (512, 512, 512, 512, 128, 128)
Q_PACK = 1536
KV_PACK = 1152

TM = 256
TR = 256
GROUP_PAIRS = tuple((i, j) for i in range(EXPERTS_PER_GROUP) for j in range(i + 1, EXPERTS_PER_GROUP))
N_BUCKETS = N_GROUPS * len(GROUP_PAIRS)
BUCKET_ROWS = 32
N_TOKENS = BATCH * SEQ + DEC_BATCH * DEC_SEQ
SORTED_ROWS = N_TOKENS + N_BUCKETS * TR
KEY_CHUNK = 256
SUB_UNROLL = 4
TQ_SUB = 256


def _params(n_axes):
    return pltpu.CompilerParams(dimension_semantics=("parallel",) * n_axes,
                                vmem_limit_bytes=VMEM_LIMIT)


def _dot(a, b):
    return jnp.dot(a, b, preferred_element_type=F32)


def _dot_nt(a, b):
    return lax.dot_general(a, b, (((1,), (1,)), ((), ())), preferred_element_type=F32)


def _layernorm(z, g, b):
    mu = jnp.mean(z, axis=-1, keepdims=True)
    zc = z - mu
    var = jnp.mean(zc * zc, axis=-1, keepdims=True)
    return zc * lax.rsqrt(var + LN_EPS) * g + b


def _mod_body(c_ref, w_ref, b_ref, o_ref):
    c = c_ref[...]
    s = (c * jax.nn.sigmoid(c)).astype(BF16)
    o_ref[...] = _dot(s, w_ref[...].astype(BF16)) + b_ref[...]


def _modulation(cond, w_mod, b_mod):
    tn = 1536
    return pl.pallas_call(
        _mod_body,
        out_shape=jax.ShapeDtypeStruct((DEPTH, N_COND, 6 * D_MODEL), F32),
        grid=(DEPTH, 6 * D_MODEL // tn),
        in_specs=[
            pl.BlockSpec((N_COND, D_MODEL), lambda l, j: (0, 0)),
            pl.BlockSpec((None, D_MODEL, tn), lambda l, j: (l, 0, j)),
            pl.BlockSpec((None, 1, tn), lambda l, j: (l, 0, j)),
        ],
        out_specs=pl.BlockSpec((None, N_COND, tn), lambda l, j: (l, 0, j)),
        compiler_params=_params(2),
        name="modulation",
    )(cond, w_mod, b_mod.reshape(DEPTH, 1, 6 * D_MODEL))


def _in_proj_body(rope, x_ref, mod_ref, w_ref, qn_ref, kn_ref, g512_ref, g128_ref, *rest):
    if rope:
        ca_ref, cb_ref, cc_ref = rest[:3]
        outs = rest[3:]
    else:
        outs = rest[len(CACHE_WIDTHS):]
    x = x_ref[...]
    sh1 = mod_ref[:, 0:D_MODEL]
    sc1 = mod_ref[:, D_MODEL:2 * D_MODEL]
    h = (x * (1.0 + sc1) + sh1).astype(BF16)

    def proj(off, width):
        return _dot(h, w_ref[:, off:off + width])

    def rms(t, g_ref, wn_ref):
        t2 = t * t
        hi = t2.astype(BF16)
        lo = (t2 - hi.astype(F32)).astype(BF16)
        ms = _dot(hi, g_ref[...]) + _dot(lo, g_ref[...])
        return t * lax.rsqrt(ms + RMS_EPS) * wn_ref[...]

    def rot(t):
        if not rope:
            return t
        ca, cb, cc = ca_ref[...], cb_ref[...], cc_ref[...]
        pieces = []
        for j in range(t.shape[1] // LANES):
            blk = t[:, j * LANES:(j + 1) * LANES]
            pieces.append(blk * ca + pltpu.roll(blk, LANES - 1, 1) * cb + pltpu.roll(blk, 1, 1) * cc)
        return pieces[0] if len(pieces) == 1 else jnp.concatenate(pieces, axis=1)

    qa = rot(proj(OFF_QA, 512)) * QK_SCALE
    ka = rot(proj(OFF_KA, 512))
    va = proj(OFF_VA, 512)
    qb = proj(OFF_QB, 512) * QK_SCALE
    kb = proj(OFF_KB, 512)
    vb = proj(OFF_VB, 512)
    qc = rot(rms(proj(OFF_QC, 512), g512_ref, qn_ref)) * QK_SCALE
    kc = rot(rms(proj(OFF_KC, 128), g128_ref, kn_ref))
    vc = proj(OFF_VC, 128)

    q_ref = outs[0]
    q_ref[:, 0:512] = qa.astype(BF16)
    q_ref[:, 512:1024] = qb.astype(BF16)
    q_ref[:, 1024:1536] = qc.astype(BF16)
    if rope:
        k_ref, v_ref = outs[1:]
        k_ref[:, 0:512] = ka.astype(BF16)
        k_ref[:, 512:1024] = kb.astype(BF16)
        k_ref[:, 1024:1152] = kc.astype(BF16)
        v_ref[:, 0:512] = va.astype(BF16)
        v_ref[:, 512:1024] = vb.astype(BF16)
        v_ref[:, 1024:1152] = vc.astype(BF16)
    else:
        ka_ref, va_ref, kb_ref, vb_ref, kc_ref, vc_ref = outs[1:]
        ka_ref[...] = ka
        va_ref[...] = va
        kb_ref[...] = kb
        vb_ref[...] = vb
        kc_ref[...] = kc
        vc_ref[...] = vc


def _in_proj(x, mod_l, row_of_tile, w_bf, qn, kn, g512, g128, rope_tabs, caches=None, layer=None):
    n_tok = x.shape[0]
    n_tiles = n_tok // TM
    rope = rope_tabs is not None
    aliases = {}
    const = lambda i: (0, 0)
    in_specs = [
        pl.BlockSpec((TM, D_MODEL), lambda i: (i, 0)),
        pl.BlockSpec((None, 1, 6 * D_MODEL), lambda i: (row_of_tile(i), 0, 0)),
        pl.BlockSpec((D_MODEL, IN_WIDTH), const),
        pl.BlockSpec((1, 512), const),
        pl.BlockSpec((1, 128), const),
        pl.BlockSpec((512, 512), const),
        pl.BlockSpec((128, 128), const),
    ]
    args = [x, mod_l, w_bf, qn, kn, g512, g128]
    tile = lambda width: pl.BlockSpec((TM, width), lambda i: (i, 0))
    if rope:
        tiles_per_seq = DEC_SEQ // TM
        in_specs += [pl.BlockSpec((TM, LANES), lambda i: (i % tiles_per_seq, 0))] * 3
        args += list(rope_tabs)
        out_shape = [jax.ShapeDtypeStruct((n_tok, Q_PACK), BF16),
                     jax.ShapeDtypeStruct((n_tok, KV_PACK), BF16),
                     jax.ShapeDtypeStruct((n_tok, KV_PACK), BF16)]
        out_specs = [tile(Q_PACK), tile(KV_PACK), tile(KV_PACK)]
    else:
        assert TM == SEQ and n_tiles == BATCH
        aliases = {len(args) + k: 1 + k for k in range(len(CACHE_WIDTHS))}
        in_specs += [pl.BlockSpec(memory_space=pl.ANY)] * len(CACHE_WIDTHS)
        args += list(caches)
        out_shape = [jax.ShapeDtypeStruct((n_tok, Q_PACK), BF16)] + [
            jax.ShapeDtypeStruct(c.shape, F32) for c in caches]
        out_specs = [tile(Q_PACK)] + [pl.BlockSpec((None, None, SEQ, w), lambda i: (i, layer, 0, 0))
                                      for w in CACHE_WIDTHS]
    return pl.pallas_call(
        functools.partial(_in_proj_body, rope),
        out_shape=out_shape,
        grid=(n_tiles,),
        in_specs=in_specs,
        out_specs=out_specs,
        input_output_aliases=aliases,
        compiler_params=_params(1),
        name="in_proj_latent" if rope else "in_proj_context",
    )(*args)


def _head_scores(q_pair, half, k_pair):
    lane = lax.broadcasted_iota(jnp.int32, q_pair.shape, 1)
    keep = (lane < HEAD_DIM) if half == 0 else (lane >= HEAD_DIM)
    qm = jnp.where(keep, q_pair, jnp.zeros_like(q_pair))
    return _dot_nt(qm, k_pair)


def _softmax_parts(s_list):
    mx = functools.reduce(jnp.maximum, [jnp.max(s, axis=-1, keepdims=True) for s in s_list])
    e_list = [jnp.exp2(s - mx) for s in s_list]
    den = functools.reduce(operator.add, [jnp.sum(e, axis=-1, keepdims=True) for e in e_list])
    return e_list, 1.0 / den


def _pair_attention(q_pair, segs):
    outs = []
    for half in (0, 1):
        s_list = []
        for k_pair, _, bias in segs:
            s = _head_scores(q_pair, half, k_pair)
            if bias is not None:
                s = s + bias(half)
            s_list.append(s)
        e_list, inv = _softmax_parts(s_list)
        o = functools.reduce(operator.add, [_dot(e.astype(BF16), seg[1]) for e, seg in zip(e_list, segs)])
        outs.append(o * inv)
    lane = lax.broadcasted_iota(jnp.int32, outs[0].shape, 1)
    return jnp.where(lane < HEAD_DIM, outs[0], outs[1])


def _lambda(lam_ref, layer):
    lam_init = 0.8 - 0.6 * math.exp(-0.3 * layer)
    l = lam_ref[...]
    a = jnp.sum(l[0:1] * l[1:2], axis=-1, keepdims=True)
    b = jnp.sum(l[2:3] * l[3:4], axis=-1, keepdims=True)
    return jnp.exp(a) - jnp.exp(b) + lam_init, lam_init


def _diff_attention(q_blk, k_segs, v_segs, lam_v, lam_init, subln):
    outs = []
    for h in range(H_A):
        j, half = h // 2, h % 2
        parts = []
        for m in range(2):
            qb = q_blk(2 * m + j)
            s_list = [_head_scores(qb, half, ks(2 * m + j)) for ks in k_segs]
            parts.append(_softmax_parts(s_list))
        (e1, inv1), (e2, inv2) = parts
        c2 = lam_v * inv2
        o = functools.reduce(operator.add, [
            _dot((e1[s] * inv1 - e2[s] * c2).astype(BF16), v_segs[s](h)) for s in range(len(k_segs))])
        ms = jnp.mean(o * o, axis=-1, keepdims=True)
        outs.append(o * lax.rsqrt(ms + RMS_EPS) * subln * (1.0 - lam_init))
    return outs


def _tile(ref, j, dtype=None):
    t = ref[:, j * LANES:(j + 1) * LANES]
    return t if dtype is None else t.astype(dtype)


def _ctx_attn_body(layer, q_ref, ka_ref, va_ref, kb_ref, vb_ref, kc_ref, vc_ref, lam_ref, subln_ref,
                   oa_ref, ob_ref, oc_ref):
    lam_v, lam_init = _lambda(lam_ref, layer)
    oa = _diff_attention(lambda j: _tile(q_ref, j),
                         [lambda j: _tile(ka_ref, j, BF16)],
                         [lambda h: _tile(va_ref, h, BF16)],
                         lam_v, lam_init, subln_ref[...])
    for h in range(H_A):
        oa_ref[:, h * LANES:(h + 1) * LANES] = oa[h].astype(BF16)
    for j in range(4):
        o = _pair_attention(_tile(q_ref, 4 + j), [(_tile(kb_ref, j, BF16), _tile(vb_ref, j, BF16), None)])
        ob_ref[:, j * LANES:(j + 1) * LANES] = o.astype(BF16)
    kc = kc_ref[...].astype(BF16)
    vc = vc_ref[...].astype(BF16)
    for g in range(4):
        o = _pair_attention(_tile(q_ref, 8 + g), [(kc, vc, None)])
        oc_ref[:, g * LANES:(g + 1) * LANES] = o.astype(BF16)


def _ctx_attention(layer, q, ka, va, kb, vb, kc, vc, lam_l, subln_l):
    n_tok = q.shape[0]
    tile = lambda width: pl.BlockSpec((SEQ, width), lambda b: (b, 0))
    slab = lambda width: pl.BlockSpec((None, None, SEQ, width), lambda b: (b, layer, 0, 0))
    const = lambda b: (0, 0)
    return pl.pallas_call(
        functools.partial(_ctx_attn_body, layer),
        out_shape=[jax.ShapeDtypeStruct((n_tok, BRANCH_W), BF16)] * 3,
        grid=(n_tok // SEQ,),
        in_specs=[tile(Q_PACK)] + [slab(w) for w in CACHE_WIDTHS]
                 + [pl.BlockSpec((4, HEAD_DIM), const), pl.BlockSpec((1, 2 * HEAD_DIM), const)],
        out_specs=[tile(BRANCH_W)] * 3,
        compiler_params=_params(1),
        name="context_attention",
    )(q, ka, va, kb, vb, kc, vc, lam_l, subln_l)


N_KEYS = PAST_LEN + DEC_SEQ
N_SUB = DEC_SEQ // TQ_SUB


def _sub_rows(i):
    return pl.ds(pl.multiple_of(i * TQ_SUB, TQ_SUB), TQ_SUB)


def _join_keys(all_ref, ctx_ref, lat_ref):
    all_ref[0:PAST_LEN, :] = ctx_ref[...].astype(BF16)
    all_ref[PAST_LEN:N_KEYS, :] = lat_ref[...]


def _join_values_t(vt_ref, ctx_ref, lat_ref):
    vt_ref[:, 0:PAST_LEN] = ctx_ref[...].T.astype(BF16)
    vt_ref[:, PAST_LEN:N_KEYS] = lat_ref[...].astype(F32).T.astype(BF16)


def _softmax_t(k_ref, q, keep):
    qm = jnp.where(keep, q, jnp.zeros_like(q))
    pieces = [_dot_nt(k_ref[c * KEY_CHUNK:(c + 1) * KEY_CHUNK, :], qm) for c in range(N_KEYS // KEY_CHUNK)]

    def over_keys(x, op):
        return op(op(x.reshape(KEY_CHUNK // 8, 8, x.shape[1]), axis=0), axis=0, keepdims=True)

    m = over_keys(functools.reduce(jnp.maximum, pieces), jnp.max)
    es = [jnp.exp2(s - m) for s in pieces]
    return es, 1.0 / over_keys(functools.reduce(operator.add, es), jnp.sum)


def _values_t(vt, weights):
    return functools.reduce(operator.add, [
        _dot(vt[:, c * KEY_CHUNK:(c + 1) * KEY_CHUNK], w.astype(BF16)) for c, w in enumerate(weights)])


def _lat_diff_body(layer, q1_ref, q2_ref, k1l_ref, k2l_ref, vl_ref, k1c_ref, k2c_ref, vc_ref, lam_ref,
                   subln_ref, o_ref, k1_ref, k2_ref, vt_ref):
    _join_keys(k1_ref, k1c_ref, k1l_ref)
    _join_keys(k2_ref, k2c_ref, k2l_ref)
    _join_values_t(vt_ref, vc_ref, vl_ref)
    lam_v, lam_init = _lambda(lam_ref, layer)
    subln = subln_ref[...] * (1.0 - lam_init)
    lo = (pl.program_id(1) % 2) * HEAD_DIM
    lane = lax.broadcasted_iota(jnp.int32, (TQ_SUB, LANES), 1)
    keep = (lane >= lo) & (lane < lo + HEAD_DIM)

    def sub_tile(i, carry):
        e1, inv1 = _softmax_t(k1_ref, q1_ref[_sub_rows(i), :], keep)
        e2, inv2 = _softmax_t(k2_ref, q2_ref[_sub_rows(i), :], keep)
        c2 = lam_v * inv2
        o = _values_t(vt_ref, [a * inv1 - b * c2 for a, b in zip(e1, e2)]).T
        ms = jnp.mean(o * o, axis=-1, keepdims=True)
        o_ref[_sub_rows(i), :] = (o * lax.rsqrt(ms + RMS_EPS) * subln).astype(BF16)
        return carry

    lax.fori_loop(0, N_SUB, sub_tile, 0, unroll=SUB_UNROLL)


def _lat_diff_attention(layer, q, k, v, cache_k, cache_v, lam_l, subln_l):
    const = lambda b, h: (0, 0)
    seq = lambda col: pl.BlockSpec((DEC_SEQ, LANES), lambda b, h: (b, col(h)))
    past = lambda col: pl.BlockSpec((None, None, PAST_LEN, LANES), lambda b, h: (b, layer, 0, col(h)))
    map1 = lambda h: h // 2
    map2 = lambda h: 2 + h // 2
    head = lambda h: h
    return pl.pallas_call(
        functools.partial(_lat_diff_body, layer),
        out_shape=jax.ShapeDtypeStruct((DEC_BATCH * DEC_SEQ, BRANCH_W), BF16),
        grid=(DEC_BATCH, H_A),
        in_specs=[seq(map1), seq(map2), seq(map1), seq(map2), seq(head),
                  past(map1), past(map2), past(head),
                  pl.BlockSpec((4, HEAD_DIM), const), pl.BlockSpec((1, 2 * HEAD_DIM), const)],
        out_specs=seq(head),
        scratch_shapes=[pltpu.VMEM((N_KEYS, LANES), BF16)] * 2 + [pltpu.VMEM((LANES, N_KEYS), BF16)],
        compiler_params=_params(2),
        name="latent_diff_attention",
    )(q, q, k, k, v, cache_k, cache_k, cache_v, lam_l, subln_l)


def _lat_gqa_body(q_ref, kl_ref, vl_ref, kc_ref, vc_ref, o_ref, k_ref, vt_ref):
    _join_keys(k_ref, kc_ref, kl_ref)
    _join_values_t(vt_ref, vc_ref, vl_ref)
    lane = lax.broadcasted_iota(jnp.int32, (TQ_SUB, LANES), 1)

    def sub_tile(i, carry):
        q = q_ref[_sub_rows(i), :]
        outs = []
        for n in range(KV_C):
            keep = (lane < HEAD_DIM) if n == 0 else (lane >= HEAD_DIM)
            es, inv = _softmax_t(k_ref, q, keep)
            outs.append(_values_t(vt_ref.at[n * HEAD_DIM:(n + 1) * HEAD_DIM], es) * inv)
        o_ref[_sub_rows(i), :] = jnp.concatenate(outs, axis=0).T.astype(BF16)
        return carry

    lax.fori_loop(0, N_SUB, sub_tile, 0, unroll=SUB_UNROLL)


def _lat_gqa_attention(layer, q, k, v, cache_k, cache_v):
    n_pairs = H_C // KV_C
    kv_col = 1024 // LANES
    past = pl.BlockSpec((None, None, PAST_LEN, LANES), lambda b, g: (b, layer, 0, 0))
    return pl.pallas_call(
        _lat_gqa_body,
        out_shape=jax.ShapeDtypeStruct((DEC_BATCH * DEC_SEQ, BRANCH_W), BF16),
        grid=(DEC_BATCH, n_pairs),
        in_specs=[
            pl.BlockSpec((DEC_SEQ, LANES), lambda b, g: (b, 1024 // LANES + g)),
            pl.BlockSpec((DEC_SEQ, LANES), lambda b, g: (b, kv_col)),
            pl.BlockSpec((DEC_SEQ, LANES), lambda b, g: (b, kv_col)),
            past, past,
        ],
        out_specs=pl.BlockSpec((DEC_SEQ, LANES), lambda b, g: (b, g)),
        scratch_shapes=[pltpu.VMEM((N_KEYS, LANES), BF16), pltpu.VMEM((LANES, N_KEYS), BF16)],
        compiler_params=_params(2),
        name="latent_gqa_attention",
    )(q, k, v, cache_k, cache_v)


def _nbr_key_start(g):
    return jnp.clip(g * NBR_ROWS - NA_KH // 2, 0, GRID_H - NBR_KEY_ROWS)


def _lat_nbr_body(q_ref, kl_ref, vl_ref, kc_ref, vc_ref, bias_ref, o_ref):
    base = pl.multiple_of(_nbr_key_start(pl.program_id(1)) * GRID_W, GRID_W)
    n_loc = NBR_KEY_ROWS * GRID_W
    for j in range(4):
        k_loc = kl_ref[pl.ds(base, n_loc), j * LANES:(j + 1) * LANES]
        v_loc = vl_ref[pl.ds(base, n_loc), j * LANES:(j + 1) * LANES]
        segs = [(k_loc, v_loc, lambda half, j=j: bias_ref[2 * j + half]),
                (_tile(kc_ref, j, BF16), _tile(vc_ref, j, BF16), None)]
        o = _pair_attention(_tile(q_ref, j), segs)
        o_ref[:, j * LANES:(j + 1) * LANES] = o.astype(BF16)


def _lat_nbr_attention(layer, q, k, v, cache_k, cache_v, bias_tab):
    n_groups = GRID_H // NBR_ROWS
    tq = NBR_ROWS * GRID_W
    case = lambda g: jnp.where(g == 0, 0, jnp.where(g == n_groups - 1, 2, 1))
    return pl.pallas_call(
        _lat_nbr_body,
        out_shape=jax.ShapeDtypeStruct((DEC_BATCH * DEC_SEQ, BRANCH_W), BF16),
        grid=(DEC_BATCH, n_groups),
        in_specs=[
            pl.BlockSpec((tq, 512), lambda b, g: (b * n_groups + g, 1)),
            pl.BlockSpec((DEC_SEQ, 512), lambda b, g: (b, 1)),
            pl.BlockSpec((DEC_SEQ, 512), lambda b, g: (b, 1)),
            pl.BlockSpec((None, None, PAST_LEN, 512), lambda b, g: (b, layer, 0, 0)),
            pl.BlockSpec((None, None, PAST_LEN, 512), lambda b, g: (b, layer, 0, 0)),
            pl.BlockSpec((None, H_B, tq, NBR_KEY_ROWS * GRID_W), lambda b, g: (case(g), 0, 0, 0)),
        ],
        out_specs=pl.BlockSpec((tq, BRANCH_W), lambda b, g: (b * n_groups + g, 0)),
        compiler_params=_params(2),
        name="latent_neighbourhood_attention",
    )(q, k, v, cache_k, cache_v, bias_tab)


def _merge_body(alpha, x_ref, oa_ref, ob_ref, oc_ref, mod_ref, wg_ref, bg_ref, wbr_ref, wo_ref,
                lng_ref, lnb_ref, out_ref):
    x = x_ref[...]
    sh1 = mod_ref[:, 0:D_MODEL]
    sc1 = mod_ref[:, D_MODEL:2 * D_MODEL]
    g1 = mod_ref[:, 2 * D_MODEL:3 * D_MODEL]
    h = (x * (1.0 + sc1) + sh1).astype(BF16)
    acc = None
    for i, o_ref in enumerate((oa_ref, ob_ref, oc_ref)):
        cols = slice(i * D_MODEL, (i + 1) * D_MODEL)
        gate = jax.nn.sigmoid(_dot(h, wg_ref[:, cols]) + bg_ref[:, cols])
        term = gate * _dot(o_ref[...], wbr_ref[i])
        acc = term if acc is None else acc + term
    y = _dot(acc.astype(BF16), wo_ref[...])
    out_ref[...] = _layernorm(alpha * x + g1 * y, lng_ref[...], lnb_ref[...])


def _merge(alpha, x, oa, ob, oc, mod_l, row_of_tile, wg, bg, wbr, wo, lng, lnb):
    n_tok = x.shape[0]
    const = lambda i: (0, 0)
    tile = lambda width: pl.BlockSpec((TM, width), lambda i: (i, 0))
    return pl.pallas_call(
        functools.partial(_merge_body, alpha),
        out_shape=jax.ShapeDtypeStruct((n_tok, D_MODEL), F32),
        grid=(n_tok // TM,),
        in_specs=[
            tile(D_MODEL), tile(BRANCH_W), tile(BRANCH_W), tile(BRANCH_W),
            pl.BlockSpec((None, 1, 6 * D_MODEL), lambda i: (row_of_tile(i), 0, 0)),
            pl.BlockSpec((D_MODEL, 3 * D_MODEL), const),
            pl.BlockSpec((1, 3 * D_MODEL), const),
            pl.BlockSpec((3, BRANCH_W, D_MODEL), lambda i: (0, 0, 0)),
            pl.BlockSpec((D_MODEL, D_MODEL), const),
            pl.BlockSpec((1, D_MODEL), const),
            pl.BlockSpec((1, D_MODEL), const),
        ],
        out_specs=tile(D_MODEL),
        compiler_params=_params(1),
        name="branch_merge",
    )(x, oa, ob, oc, mod_l, wg, bg, wbr, wo, lng, lnb)


def _routing(logits, bias):
    scores = jax.nn.sigmoid(logits)
    sel_all = scores + bias
    sel = [sel_all[e:e + 1] for e in range(N_EXPERTS)]
    gscore = []
    for g in range(N_GROUPS):
        a, b, c, d = sel[4 * g:4 * g + 4]
        hi1, lo1 = jnp.maximum(a, b), jnp.minimum(a, b)
        hi2, lo2 = jnp.maximum(c, d), jnp.minimum(c, d)
        gscore.append(jnp.maximum(hi1, hi2) + jnp.maximum(jnp.minimum(hi1, hi2), jnp.maximum(lo1, lo2)))
    best = []
    taken = None
    for g in range(N_GROUPS):
        ok = functools.reduce(operator.and_, [gscore[g] >= gscore[k] for k in range(g + 1, N_GROUPS)],
                              jnp.ones_like(gscore[g], dtype=jnp.bool_))
        if taken is not None:
            ok = ok & jnp.logical_not(taken)
        best.append(ok)
        taken = ok if taken is None else (taken | ok)
    picked = []
    for e in range(N_EXPERTS):
        g, i = divmod(e, EXPERTS_PER_GROUP)
        rank = jnp.zeros_like(sel[0])
        for k in range(EXPERTS_PER_GROUP):
            if k == i:
                continue
            other = sel[4 * g + k]
            ahead = (other >= sel[e]) if k < i else (other > sel[e])
            rank = rank + jnp.where(ahead, 1.0, 0.0)
        picked.append((rank < 2.0) & best[g])
    return [picked[4 * g + i] & picked[4 * g + j] for g in range(N_GROUPS) for i, j in GROUP_PAIRS]


def _split_bf16(v):
    hi = v.astype(BF16)
    return hi, (v - hi.astype(F32)).astype(BF16)


TILE_ROWS = D_MODEL // LANES


def _store_token_tiles(ref, val):
    for t0 in range(0, val.shape[0], 8):
        for j in range(TILE_ROWS):
            ref[pl.ds(t0 * TILE_ROWS + j, 8, stride=TILE_ROWS), :] = val[t0:t0 + 8, j * LANES:(j + 1) * LANES]


def _load_token_tiles(ref, n):
    rows = [jnp.concatenate([ref[pl.ds(t0 * TILE_ROWS + j, 8, stride=TILE_ROWS), :] for j in range(TILE_ROWS)],
                            axis=1) for t0 in range(0, n, 8)]
    return jnp.concatenate(rows, axis=0)


def _route_body(x_ref, mod_ref, wrt_ref, rb_ref, tri_ref, cnt_in_ref, h2_ref, meta_ref, cnt_ref):
    @pl.when(pl.program_id(0) == 0)
    def _():
        cnt_ref[...] = cnt_in_ref[...]

    x = x_ref[...]
    sh2 = mod_ref[:, 3 * D_MODEL:4 * D_MODEL]
    sc2 = mod_ref[:, 4 * D_MODEL:5 * D_MODEL]
    h2 = x * (1.0 + sc2) + sh2
    _store_token_tiles(h2_ref, h2)
    hi, lo = _split_bf16(h2)
    wr_hi, wr_lo = _split_bf16(wrt_ref[...])
    logits = _dot_nt(wr_hi, hi) + _dot_nt(wr_hi, lo) + _dot_nt(wr_lo, hi)
    masks = _routing(logits, rb_ref[...])
    tm = x.shape[0]
    onehot = jnp.concatenate([jnp.where(m, 1.0, 0.0) for m in masks]
                             + [jnp.zeros((BUCKET_ROWS - N_BUCKETS, tm), F32)], axis=0)
    ahead = _dot(onehot.astype(BF16), tri_ref[...]) + cnt_ref[:, 0:1]
    rank = jnp.sum(onehot * ahead, axis=0, keepdims=True)
    bucket_id = lax.broadcasted_iota(jnp.int32, onehot.shape, 0).astype(F32)
    bucket = jnp.sum(onehot * bucket_id, axis=0, keepdims=True)
    meta_ref[...] = jnp.concatenate([bucket, rank, jnp.zeros((6, tm), F32)], axis=0).astype(jnp.int32)
    cnt_ref[...] = cnt_ref[...] + jnp.sum(onehot, axis=1, keepdims=True)


def _route(x, mod_l, row_of_tile, wrt, rb, tri, cnt_in):
    n_tok = x.shape[0]
    const = lambda i: (0, 0)
    tile = pl.BlockSpec((TM, D_MODEL), lambda i: (i, 0))
    return pl.pallas_call(
        _route_body,
        out_shape=[jax.ShapeDtypeStruct((n_tok * TILE_ROWS, LANES), F32),
                   jax.ShapeDtypeStruct((8, n_tok), jnp.int32),
                   jax.ShapeDtypeStruct((BUCKET_ROWS, LANES), F32)],
        grid=(n_tok // TM,),
        in_specs=[
            tile,
            pl.BlockSpec((None, 1, 6 * D_MODEL), lambda i: (row_of_tile(i), 0, 0)),
            pl.BlockSpec((N_EXPERTS, D_MODEL), const),
            pl.BlockSpec((N_EXPERTS, 1), const),
            pl.BlockSpec((TM, TM), const),
            pl.BlockSpec((BUCKET_ROWS, LANES), const),
        ],
        out_specs=[pl.BlockSpec((TM * TILE_ROWS, LANES), lambda i: (i, 0)),
                   pl.BlockSpec((8, TM), lambda i: (0, i)), pl.BlockSpec((BUCKET_ROWS, LANES), const)],
        compiler_params=pltpu.CompilerParams(dimension_semantics=("arbitrary",), vmem_limit_bytes=VMEM_LIMIT),
        name="moe_route",
    )(x, mod_l, wrt, rb, tri, cnt_in)


def _row_copy(src_ref, src_row, dst_ref, dst_row, sem):
    src = src_ref.at[pl.ds(pl.multiple_of(src_row * TILE_ROWS, TILE_ROWS), TILE_ROWS)]
    dst = dst_ref.at[pl.ds(pl.multiple_of(dst_row * TILE_ROWS, TILE_ROWS), TILE_ROWS)]
    return pltpu.make_async_copy(src, dst, sem)


def _dest_row(base_ref, bucket_ref, rank_ref, token):
    return base_ref[bucket_ref[token]] + rank_ref[token]


def _scatter_body(base_ref, bucket_ref, rank_ref, h_ref, xs_in_ref, xs_ref, sem):
    del xs_in_ref
    first = pl.program_id(0) * TM

    def issue(t, carry):
        _row_copy(h_ref, t, xs_ref, _dest_row(base_ref, bucket_ref, rank_ref, first + t), sem).start()
        return carry

    def drain(t, carry):
        _row_copy(h_ref, t, xs_ref, 0, sem).wait()
        return carry

    lax.fori_loop(0, TM, issue, 0, unroll=8)
    lax.fori_loop(0, TM, drain, 0, unroll=8)


def _scatter(base, bucket, rank, h2, xs):
    n_tok = h2.shape[0] // TILE_ROWS
    return pl.pallas_call(
        _scatter_body,
        out_shape=jax.ShapeDtypeStruct(xs.shape, xs.dtype),
        grid_spec=pltpu.PrefetchScalarGridSpec(
            num_scalar_prefetch=3,
            grid=(n_tok // TM,),
            in_specs=[pl.BlockSpec((TM * TILE_ROWS, LANES), lambda i, *_: (i, 0)),
                      pl.BlockSpec(memory_space=pl.ANY)],
            out_specs=pl.BlockSpec(memory_space=pl.ANY),
            scratch_shapes=[pltpu.SemaphoreType.DMA(())],
        ),
        input_output_aliases={4: 0},
        compiler_params=pltpu.CompilerParams(dimension_semantics=("arbitrary",), vmem_limit_bytes=VMEM_LIMIT),
        name="moe_scatter",
    )(base, bucket, rank, h2, xs)


def _experts_body(e0_ref, e1_ref, used_ref, xs_ref, wrt_ref, w13a_ref, w2a_ref, w13b_ref, w2b_ref, y_ref):
    j = pl.program_id(0)

    @pl.when(j < used_ref[0])
    def _():
        hi, lo = _split_bf16(_load_token_tiles(xs_ref, TR))
        wr = jnp.concatenate([wrt_ref[pl.ds(e0_ref[j], 1), :], wrt_ref[pl.ds(e1_ref[j], 1), :],
                              jnp.zeros((6, D_MODEL), F32)], axis=0)
        wr_hi, wr_lo = _split_bf16(wr)
        logits = _dot_nt(hi, wr_hi) + _dot_nt(lo, wr_hi) + _dot_nt(hi, wr_lo)
        s = jax.nn.sigmoid(logits)
        s0 = s[:, 0:1]
        s1 = s[:, 1:2]
        inv = 1.0 / (s0 + s1)

        def ffn(w13_ref, w2_ref):
            h13 = _dot(hi, w13_ref[...])
            a = h13[:, :D_EXPERT]
            b = h13[:, D_EXPERT:]
            return _dot((a * jax.nn.sigmoid(a) * b).astype(BF16), w2_ref[...])

        _store_token_tiles(y_ref, (s0 * inv) * ffn(w13a_ref, w2a_ref) + (s1 * inv) * ffn(w13b_ref, w2b_ref))

    @pl.when(j >= used_ref[0])
    def _():
        y_ref[...] = jnp.zeros_like(y_ref)


def _experts(tile_e0, tile_e1, n_used, xs, wrt, w13, w2):
    n_rows = xs.shape[0] // TILE_ROWS
    tile = pl.BlockSpec((TR * TILE_ROWS, LANES), lambda j, *_: (j, 0))
    return pl.pallas_call(
        _experts_body,
        out_shape=jax.ShapeDtypeStruct(xs.shape, F32),
        grid_spec=pltpu.PrefetchScalarGridSpec(
            num_scalar_prefetch=3,
            grid=(n_rows // TR,),
            in_specs=[
                tile,
                pl.BlockSpec((N_EXPERTS, D_MODEL), lambda j, *_: (0, 0)),
                pl.BlockSpec((None, D_MODEL, 2 * D_EXPERT), lambda j, e0, e1, u: (e0[j], 0, 0)),
                pl.BlockSpec((None, D_EXPERT, D_MODEL), lambda j, e0, e1, u: (e0[j], 0, 0)),
                pl.BlockSpec((None, D_MODEL, 2 * D_EXPERT), lambda j, e0, e1, u: (e1[j], 0, 0)),
                pl.BlockSpec((None, D_EXPERT, D_MODEL), lambda j, e0, e1, u: (e1[j], 0, 0)),
            ],
            out_specs=tile,
        ),
        compiler_params=pltpu.CompilerParams(dimension_semantics=("arbitrary",), vmem_limit_bytes=VMEM_LIMIT),
        name="moe_experts",
    )(tile_e0, tile_e1, n_used, xs, wrt, w13, w2, w13, w2)


def _gather_norm_body(alpha, base_ref, bucket_ref, rank_ref, x_ref, mod_ref, y_ref, lng_ref, lnb_ref,
                      out_ref, m_ref, sem):
    i = pl.program_id(0)

    def fetch(step, slot):
        def issue(t, carry):
            src_row = _dest_row(base_ref, bucket_ref, rank_ref, step * TM + t)
            _row_copy(y_ref, src_row, m_ref.at[slot], t, sem.at[slot]).start()
            return carry

        lax.fori_loop(0, TM, issue, 0, unroll=8)

    @pl.when(i == 0)
    def _():
        fetch(0, 0)

    @pl.when(i + 1 < pl.num_programs(0))
    def _():
        fetch(i + 1, (i + 1) % 2)

    slot = i % 2

    def drain(t, carry):
        _row_copy(y_ref, 0, m_ref.at[slot], t, sem.at[slot]).wait()
        return carry

    lax.fori_loop(0, TM, drain, 0, unroll=8)
    g2 = mod_ref[:, 5 * D_MODEL:6 * D_MODEL]
    m = _load_token_tiles(m_ref.at[slot], TM)
    out_ref[...] = _layernorm(alpha * x_ref[...] + g2 * m, lng_ref[...], lnb_ref[...])


def _gather_norm(alpha, base, bucket, rank, x, mod_l, row_of_tile, y, lng, lnb):
    n_tok = x.shape[0]
    const = lambda i, *_: (0, 0)
    tile = pl.BlockSpec((TM, D_MODEL), lambda i, *_: (i, 0))
    return pl.pallas_call(
        functools.partial(_gather_norm_body, alpha),
        out_shape=jax.ShapeDtypeStruct((n_tok, D_MODEL), F32),
        grid_spec=pltpu.PrefetchScalarGridSpec(
            num_scalar_prefetch=3,
            grid=(n_tok // TM,),
            in_specs=[
                tile,
                pl.BlockSpec((None, 1, 6 * D_MODEL), lambda i, *_: (row_of_tile(i), 0, 0)),
                pl.BlockSpec(memory_space=pl.ANY),
                pl.BlockSpec((1, D_MODEL), const),
                pl.BlockSpec((1, D_MODEL), const),
            ],
            out_specs=tile,
            scratch_shapes=[pltpu.VMEM((2, TM * TILE_ROWS, LANES), F32), pltpu.SemaphoreType.DMA((2,))],
        ),
        compiler_params=pltpu.CompilerParams(dimension_semantics=("arbitrary",), vmem_limit_bytes=VMEM_LIMIT),
        name="moe_gather_norm",
    )(base, bucket, rank, x, mod_l, y, lng, lnb)


def _bucket_layout(counts):
    cnt = counts[:N_BUCKETS, 0].astype(jnp.int32)
    padded = (cnt + TR - 1) // TR * TR
    ends = jnp.cumsum(padded)
    base = jnp.zeros((BUCKET_ROWS,), jnp.int32).at[:N_BUCKETS].set(ends - padded)
    tile_start = jnp.arange(SORTED_ROWS // TR, dtype=jnp.int32) * TR
    tile_bucket = jnp.minimum(jnp.sum(tile_start[:, None] >= ends[None, :], axis=1), N_BUCKETS - 1)
    first = jnp.array([4 * g + i for g in range(N_GROUPS) for i, _ in GROUP_PAIRS], jnp.int32)
    second = jnp.array([4 * g + j for g in range(N_GROUPS) for _, j in GROUP_PAIRS], jnp.int32)
    return base, first[tile_bucket], second[tile_bucket], (ends[-1:] // TR).astype(jnp.int32)


def _rope_tables():
    t = jnp.arange(DEC_SEQ)
    row = (t // GRID_W).astype(F32)
    col = (t % GRID_W).astype(F32)
    n = HEAD_DIM // 4
    freqs = ROPE_THETA ** (-jnp.arange(n, dtype=F32) / n)
    ang = jnp.concatenate([row[:, None] * freqs, col[:, None] * freqs], axis=-1)
    cos = jnp.tile(jnp.repeat(jnp.cos(ang), 2, axis=-1), (1, LANES // HEAD_DIM))
    sin = jnp.tile(jnp.repeat(jnp.sin(ang), 2, axis=-1), (1, LANES // HEAD_DIM))
    even = (jnp.arange(LANES) % 2 == 0)[None, :]
    return cos, jnp.where(even, -sin, 0.0), jnp.where(even, 0.0, sin)


def _group_mean_matrix(width):
    g = jnp.arange(width) // HEAD_DIM
    return jnp.where(g[:, None] == g[None, :], 1.0 / HEAD_DIM, 0.0).astype(BF16)


def _neighbourhood_bias(rpb_l):
    cx = jnp.arange(GRID_W)
    col_start = jnp.clip(cx - NA_KW // 2, 0, GRID_W - NA_KW)
    kx = jnp.arange(GRID_W)
    inside_col = (kx[None, :] >= col_start[:, None]) & (kx[None, :] < col_start[:, None] + NA_KW)
    dx = jnp.clip(kx[None, :] - cx[:, None] + (NA_KW - 1), 0, 2 * NA_KW - 2)
    exact = lax.Precision.HIGHEST
    pick_dx = ((dx[:, :, None] == jnp.arange(2 * NA_KW - 1)) & inside_col[:, :, None]).astype(F32)
    n_groups = GRID_H // NBR_ROWS
    i = jnp.arange(NBR_KEY_ROWS)
    tabs = []
    for g in (0, 1, n_groups - 1):
        key_start = min(max(g * NBR_ROWS - NA_KH // 2, 0), GRID_H - NBR_KEY_ROWS)
        r = g * NBR_ROWS + jnp.arange(NBR_ROWS)
        row_start = jnp.clip(r - NA_KH // 2, 0, GRID_H - NA_KH)
        ky = key_start + i
        inside_row = (ky[None, :] >= row_start[:, None]) & (ky[None, :] < row_start[:, None] + NA_KH)
        dy = jnp.clip(ky[None, :] - r[:, None] + (NA_KH - 1), 0, 2 * NA_KH - 2)
        pick_dy = ((dy[:, :, None] == jnp.arange(2 * NA_KH - 1)) & inside_row[:, :, None]).astype(F32)
        by_row = jnp.einsum("hyd,jiy->hjid", rpb_l, pick_dy, precision=exact)
        vals = jnp.einsum("hjid,ckd->hjcik", by_row, pick_dx, precision=exact)
        ok = inside_row[:, None, :, None] & inside_col[None, :, None, :]
        tabs.append(jnp.where(ok[None], vals * LOG2E, NEG_BIAS)
                    .reshape(H_B, NBR_ROWS * GRID_W, NBR_KEY_ROWS * GRID_W))
    return jnp.stack(tabs, axis=0)


def _gqa_regroup(w, axis):
    shape = w.shape
    w = w.reshape(shape[:axis] + (KV_C, H_C // KV_C, HEAD_DIM) + shape[axis + 1:])
    return jnp.swapaxes(w, axis, axis + 1).reshape(shape)


def kernel(x_prompt, x_sample, c, cache_a_k, cache_a_v, cache_b_k, cache_b_v, cache_c_k, cache_c_v,
           c_ctx, w_mod, b_mod, w_in, w_gate, b_gate, lam, a_subln, rpb, c_qnorm, c_knorm,
           w_br, w_o, ln_g, ln_b, w_router, router_bias, moe_w1, moe_w3, moe_w2):
    alpha = (2.0 * DEPTH) ** 0.25
    cond = jnp.concatenate([c_ctx[None, :], c, jnp.zeros((N_COND - 1 - DEC_BATCH, D_MODEL), F32)], axis=0)
    mods = _modulation(cond, w_mod, b_mod).reshape(DEPTH, N_COND, 1, 6 * D_MODEL)

    rope_tabs = _rope_tables()
    g512 = _group_mean_matrix(512)
    g128 = _group_mean_matrix(128)
    wrt = w_router.T
    rb = router_bias.reshape(N_EXPERTS, 1)
    tok = jnp.arange(TM)
    tri = (tok[:, None] < tok[None, :]).astype(BF16)

    ck_a = cache_a_k.reshape(DEC_BATCH, DEPTH, PAST_LEN, 512)
    cv_a = cache_a_v.reshape(DEC_BATCH, DEPTH, PAST_LEN, 512)
    ck_b = cache_b_k.reshape(DEC_BATCH, DEPTH, PAST_LEN, 512)
    cv_b = cache_b_v.reshape(DEC_BATCH, DEPTH, PAST_LEN, 512)
    ck_c = cache_c_k.reshape(DEC_BATCH, DEPTH, PAST_LEN, 128)
    cv_c = cache_c_v.reshape(DEC_BATCH, DEPTH, PAST_LEN, 128)

    ctx_row = lambda i: 0
    lat_tiles = DEC_SEQ // TM
    lat_row = lambda i: 1 + i // lat_tiles

    xp = x_prompt.reshape(BATCH * SEQ, D_MODEL)
    xs = x_sample.reshape(DEC_BATCH * DEC_SEQ, D_MODEL)
    new_kv = [jnp.zeros((BATCH, DEPTH, SEQ, w), F32) for w in CACHE_WIDTHS]
    for l in range(DEPTH):
        w_in_l = w_in[l]
        w_in_l = jnp.concatenate([w_in_l[:, :OFF_QC], _gqa_regroup(w_in_l[:, OFF_QC:OFF_KC], 1),
                                  w_in_l[:, OFF_KC:]], axis=1).astype(BF16)
        qn = jnp.tile(c_qnorm[l], 512 // HEAD_DIM)[None, :]
        kn = jnp.tile(c_knorm[l], 128 // HEAD_DIM)[None, :]
        wg = w_gate[l].astype(BF16)
        bg = b_gate[l][None, :]
        wbr = jnp.stack([w_br[l, 0], w_br[l, 1], _gqa_regroup(w_br[l, 2], 0)], axis=0).astype(BF16)
        wo = w_o[l].astype(BF16)
        w13 = jnp.concatenate([moe_w1[l], moe_w3[l]], axis=-1).astype(BF16)
        w2 = moe_w2[l].astype(BF16)
        lam_l = lam[l]
        subln_l = a_subln[l][None, :]
        bias_tab = _neighbourhood_bias(rpb[l])
        mod_l = mods[l]

        q, *new_kv = _in_proj(xp, mod_l, ctx_row, w_in_l, qn, kn, g512, g128, None, new_kv, l)
        oa, ob, oc = _ctx_attention(l, q, *new_kv, lam_l, subln_l)
        xp = _merge(alpha, xp, oa, ob, oc, mod_l, ctx_row, wg, bg, wbr, wo, ln_g[l, 0:1], ln_b[l, 0:1])

        q, k, v = _in_proj(xs, mod_l, lat_row, w_in_l, qn, kn, g512, g128, rope_tabs)
        oa = _lat_diff_attention(l, q, k, v, ck_a, cv_a, lam_l, subln_l)
        ob = _lat_nbr_attention(l, q, k, v, ck_b, cv_b, bias_tab)
        oc = _lat_gqa_attention(l, q, k, v, ck_c, cv_c)
        xs = _merge(alpha, xs, oa, ob, oc, mod_l, lat_row, wg, bg, wbr, wo, ln_g[l, 0:1], ln_b[l, 0:1])

        h2p, meta_p, counts = _route(xp, mod_l, ctx_row, wrt, rb, tri, jnp.zeros((BUCKET_ROWS, LANES), F32))
        h2s, meta_s, counts = _route(xs, mod_l, lat_row, wrt, rb, tri, counts)
        base, tile_e0, tile_e1, n_used = _bucket_layout(counts)
        rows = jnp.zeros((SORTED_ROWS * TILE_ROWS, LANES), F32)
        rows = _scatter(base, meta_p[0], meta_p[1], h2p, rows)
        rows = _scatter(base, meta_s[0], meta_s[1], h2s, rows)
        y = _experts(tile_e0, tile_e1, n_used, rows, wrt, w13, w2)
        xp = _gather_norm(alpha, base, meta_p[0], meta_p[1], xp, mod_l, ctx_row, y, ln_g[l, 1:2], ln_b[l, 1:2])
        xs = _gather_norm(alpha, base, meta_s[0], meta_s[1], xs, mod_l, lat_row, y, ln_g[l, 1:2], ln_b[l, 1:2])

    def cache(idx, shape):
        return new_kv[idx].reshape(BATCH, DEPTH, SEQ, *shape)

    return (xp.reshape(BATCH, SEQ, D_MODEL), xs.reshape(DEC_BATCH, DEC_SEQ, D_MODEL),
            cache(0, (2, H_A, HEAD_DIM)), cache(1, (H_A, 2 * HEAD_DIM)),
            cache(2, (H_B, HEAD_DIM)), cache(3, (H_B, HEAD_DIM)),
            cache(4, (KV_C, HEAD_DIM)), cache(5, (KV_C, HEAD_DIM)))
```

```python
import functools
import math
import operator

import jax
import jax.numpy as jnp
from jax import lax
from jax.experimental import pallas as pl
from jax.experimental.pallas import tpu as pltpu

F32 = jnp.float32
BF16 = jnp.bfloat16

D_MODEL = 1024
BATCH = 32
SEQ = 256
DEPTH = 2
DEC_BATCH = 4
DEC_SEQ = 4096
PAST_LEN = 256
GRID_W = 64
GRID_H = DEC_SEQ // GRID_W
HEAD_DIM = 64
H_A = 4
H_B = 8
H_C = 8
KV_C = 2
NA_KH = 8
NA_KW = 16
ROPE_THETA = 10000.0
N_EXPERTS = 16
N_GROUPS = 4
EXPERTS_PER_GROUP = N_EXPERTS // N_GROUPS
D_EXPERT = 256
BRANCH_W = 512
LN_EPS = 1e-5
RMS_EPS = 1e-6
LOG2E = math.log2(math.e)
QK_SCALE = HEAD_DIM ** -0.5 * LOG2E
NEG_BIAS = -1e30
NBR_ROWS = 4
NBR_KEY_ROWS = 12

LANES = 128
N_COND = 8
VMEM_LIMIT = 56 * 1024 * 1024

OFF_QA, OFF_KA, OFF_VA, OFF_QB, OFF_KB, OFF_VB, OFF_QC, OFF_KC, OFF_VC = (
    0, 512, 1024, 1536, 2048, 2560, 3072, 3584, 3712)
IN_WIDTH = 3840
CACHE_WIDTHS = (512, 512, 512, 512, 128, 128)
Q_PACK = 1536
KV_PACK = 1152

TM = 256
TR = 256
GROUP_PAIRS = tuple((i, j) for i in range(EXPERTS_PER_GROUP) for j in range(i + 1, EXPERTS_PER_GROUP))
N_BUCKETS = N_GROUPS * len(GROUP_PAIRS)
BUCKET_ROWS = 32
N_TOKENS = BATCH * SEQ + DEC_BATCH * DEC_SEQ
SORTED_ROWS = N_TOKENS + N_BUCKETS * TR
KEY_CHUNK = 256
SUB_UNROLL = 4
TQ_SUB = 256


def _params(n_axes):
    return pltpu.CompilerParams(dimension_semantics=("parallel",) * n_axes,
                                vmem_limit_bytes=VMEM_LIMIT)


def _dot(a, b):
    return jnp.dot(a, b, preferred_element_type=F32)


def _dot_nt(a, b):
    return lax.dot_general(a, b, (((1,), (1,)), ((), ())), preferred_element_type=F32)


def _layernorm(z, g, b):
    mu = jnp.mean(z, axis=-1, keepdims=True)
    zc = z - mu
    var = jnp.mean(zc * zc, axis=-1, keepdims=True)
    return zc * lax.rsqrt(var + LN_EPS) * g + b


def _mod_body(c_ref, w_ref, b_ref, o_ref):
    c = c_ref[...]
    s = (c * jax.nn.sigmoid(c)).astype(BF16)
    o_ref[...] = _dot(s, w_ref[...].astype(BF16)) + b_ref[...]


def _modulation(cond, w_mod, b_mod):
    tn = 1536
    return pl.pallas_call(
        _mod_body,
        out_shape=jax.ShapeDtypeStruct((DEPTH, N_COND, 6 * D_MODEL), F32),
        grid=(DEPTH, 6 * D_MODEL // tn),
        in_specs=[
            pl.BlockSpec((N_COND, D_MODEL), lambda l, j: (0, 0)),
            pl.BlockSpec((None, D_MODEL, tn), lambda l, j: (l, 0, j)),
            pl.BlockSpec((None, 1, tn), lambda l, j: (l, 0, j)),
        ],
        out_specs=pl.BlockSpec((None, N_COND, tn), lambda l, j: (l, 0, j)),
        compiler_params=_params(2),
        name="modulation",
    )(cond, w_mod, b_mod.reshape(DEPTH, 1, 6 * D_MODEL))


def _in_proj_body(rope, x_ref, mod_ref, w_ref, qn_ref, kn_ref, g512_ref, g128_ref, *rest):
    if rope:
        ca_ref, cb_ref, cc_ref = rest[:3]
        outs = rest[3:]
    else:
        outs = rest[len(CACHE_WIDTHS):]
    x = x_ref[...]
    sh1 = mod_ref[:, 0:D_MODEL]
    sc1 = mod_ref[:, D_MODEL:2 * D_MODEL]
    h = (x * (1.0 + sc1) + sh1).astype(BF16)

    def proj(off, width):
        return _dot(h, w_ref[:, off:off + width])

    def rms(t, g_ref, wn_ref):
        t2 = t * t
        hi = t2.astype(BF16)
        lo = (t2 - hi.astype(F32)).astype(BF16)
        ms = _dot(hi, g_ref[...]) + _dot(lo, g_ref[...])
        return t * lax.rsqrt(ms + RMS_EPS) * wn_ref[...]

    def rot(t):
        if not rope:
            return t
        ca, cb, cc = ca_ref[...], cb_ref[...], cc_ref[...]
        pieces = []
        for j in range(t.shape[1] // LANES):
            blk = t[:, j * LANES:(j + 1) * LANES]
            pieces.append(blk * ca + pltpu.roll(blk, LANES - 1, 1) * cb + pltpu.roll(blk, 1, 1) * cc)
        return pieces[0] if len(pieces) == 1 else jnp.concatenate(pieces, axis=1)

    qa = rot(proj(OFF_QA, 512)) * QK_SCALE
    ka = rot(proj(OFF_KA, 512))
    va = proj(OFF_VA, 512)
    qb = proj(OFF_QB, 512) * QK_SCALE
    kb = proj(OFF_KB, 512)
    vb = proj(OFF_VB, 512)
    qc = rot(rms(proj(OFF_QC, 512), g512_ref, qn_ref)) * QK_SCALE
    kc = rot(rms(proj(OFF_KC, 128), g128_ref, kn_ref))
    vc = proj(OFF_VC, 128)

    q_ref = outs[0]
    q_ref[:, 0:512] = qa.astype(BF16)
    q_ref[:, 512:1024] = qb.astype(BF16)
    q_ref[:, 1024:1536] = qc.astype(BF16)
    if rope:
        k_ref, v_ref = outs[1:]
        k_ref[:, 0:512] = ka.astype(BF16)
        k_ref[:, 512:1024] = kb.astype(BF16)
        k_ref[:, 1024:1152] = kc.astype(BF16)
        v_ref[:, 0:512] = va.astype(BF16)
        v_ref[:, 512:1024] = vb.astype(BF16)
        v_ref[:, 1024:1152] = vc.astype(BF16)
    else:
        ka_ref, va_ref, kb_ref, vb_ref, kc_ref, vc_ref = outs[1:]
        ka_ref[...] = ka
        va_ref[...] = va
        kb_ref[...] = kb
        vb_ref[...] = vb
        kc_ref[...] = kc
        vc_ref[...] = vc


def _in_proj(x, mod_l, row_of_tile, w_bf, qn, kn, g512, g128, rope_tabs, caches=None, layer=None):
    n_tok = x.shape[0]
    n_tiles = n_tok // TM
    rope = rope_tabs is not None
    aliases = {}
    const = lambda i: (0, 0)
    in_specs = [
        pl.BlockSpec((TM, D_MODEL), lambda i: (i, 0)),
        pl.BlockSpec((None, 1, 6 * D_MODEL), lambda i: (row_of_tile(i), 0, 0)),
        pl.BlockSpec((D_MODEL, IN_WIDTH), const),
        pl.BlockSpec((1, 512), const),
        pl.BlockSpec((1, 128), const),
        pl.BlockSpec((512, 512), const),
        pl.BlockSpec((128, 128), const),
    ]
    args = [x, mod_l, w_bf, qn, kn, g512, g128]
    tile = lambda width: pl.BlockSpec((TM, width), lambda i: (i, 0))
    if rope:
        tiles_per_seq = DEC_SEQ // TM
        in_specs += [pl.BlockSpec((TM, LANES), lambda i: (i % tiles_per_seq, 0))] * 3
        args += list(rope_tabs)
        out_shape = [jax.ShapeDtypeStruct((n_tok, Q_PACK), BF16),
                     jax.ShapeDtypeStruct((n_tok, KV_PACK), BF16),
                     jax.ShapeDtypeStruct((n_tok, KV_PACK), BF16)]
        out_specs = [tile(Q_PACK), tile(KV_PACK), tile(KV_PACK)]
    else:
        assert TM == SEQ and n_tiles == BATCH
        aliases = {len(args) + k: 1 + k for k in range(len(CACHE_WIDTHS))}
        in_specs += [pl.BlockSpec(memory_space=pl.ANY)] * len(CACHE_WIDTHS)
        args += list(caches)
        out_shape = [jax.ShapeDtypeStruct((n_tok, Q_PACK), BF16)] + [
            jax.ShapeDtypeStruct(c.shape, F32) for c in caches]
        out_specs = [tile(Q_PACK)] + [pl.BlockSpec((None, None, SEQ, w), lambda i: (i, layer, 0, 0))
                                      for w in CACHE_WIDTHS]
    return pl.pallas_call(
        functools.partial(_in_proj_body, rope),
        out_shape=out_shape,
        grid=(n_tiles,),
        in_specs=in_specs,
        out_specs=out_specs,
        input_output_aliases=aliases,
        compiler_params=_params(1),
        name="in_proj_latent" if rope else "in_proj_context",
    )(*args)


def _head_scores(q_pair, half, k_pair):
    lane = lax.broadcasted_iota(jnp.int32, q_pair.shape, 1)
    keep = (lane < HEAD_DIM) if half == 0 else (lane >= HEAD_DIM)
    qm = jnp.where(keep, q_pair, jnp.zeros_like(q_pair))
    return _dot_nt(qm, k_pair)


def _softmax_parts(s_list):
    mx = functools.reduce(jnp.maximum, [jnp.max(s, axis=-1, keepdims=True) for s in s_list])
    e_list = [jnp.exp2(s - mx) for s in s_list]
    den = functools.reduce(operator.add, [jnp.sum(e, axis=-1, keepdims=True) for e in e_list])
    return e_list, 1.0 / den


def _pair_attention(q_pair, segs):
    outs = []
    for half in (0, 1):
        s_list = []
        for k_pair, _, bias in segs:
            s = _head_scores(q_pair, half, k_pair)
            if bias is not None:
                s = s + bias(half)
            s_list.append(s)
        e_list, inv = _softmax_parts(s_list)
        o = functools.reduce(operator.add, [_dot(e.astype(BF16), seg[1]) for e, seg in zip(e_list, segs)])
        outs.append(o * inv)
    lane = lax.broadcasted_iota(jnp.int32, outs[0].shape, 1)
    return jnp.where(lane < HEAD_DIM, outs[0], outs[1])


def _lambda(lam_ref, layer):
    lam_init = 0.8 - 0.6 * math.exp(-0.3 * layer)
    l = lam_ref[...]
    a = jnp.sum(l[0:1] * l[1:2], axis=-1, keepdims=True)
    b = jnp.sum(l[2:3] * l[3:4], axis=-1, keepdims=True)
    return jnp.exp(a) - jnp.exp(b) + lam_init, lam_init


def _diff_attention(q_blk, k_segs, v_segs, lam_v, lam_init, subln):
    outs = []
    for h in range(H_A):
        j, half = h // 2, h % 2
        parts = []
        for m in range(2):
            qb = q_blk(2 * m + j)
            s_list = [_head_scores(qb, half, ks(2 * m + j)) for ks in k_segs]
            parts.append(_softmax_parts(s_list))
        (e1, inv1), (e2, inv2) = parts
        c2 = lam_v * inv2
        o = functools.reduce(operator.add, [
            _dot((e1[s] * inv1 - e2[s] * c2).astype(BF16), v_segs[s](h)) for s in range(len(k_segs))])
        ms = jnp.mean(o * o, axis=-1, keepdims=True)
        outs.append(o * lax.rsqrt(ms + RMS_EPS) * subln * (1.0 - lam_init))
    return outs


def _tile(ref, j, dtype=None):
    t = ref[:, j * LANES:(j + 1) * LANES]
    return t if dtype is None else t.astype(dtype)


def _ctx_attn_body(layer, q_ref, ka_ref, va_ref, kb_ref, vb_ref, kc_ref, vc_ref, lam_ref, subln_ref,
                   oa_ref, ob_ref, oc_ref):
    lam_v, lam_init = _lambda(lam_ref, layer)
    oa = _diff_attention(lambda j: _tile(q_ref, j),
                         [lambda j: _tile(ka_ref, j, BF16)],
                         [lambda h: _tile(va_ref, h, BF16)],
                         lam_v, lam_init, subln_ref[...])
    for h in range(H_A):
        oa_ref[:, h * LANES:(h + 1) * LANES] = oa[h].astype(BF16)
    for j in range(4):
        o = _pair_attention(_tile(q_ref, 4 + j), [(_tile(kb_ref, j, BF16), _tile(vb_ref, j, BF16), None)])
        ob_ref[:, j * LANES:(j + 1) * LANES] = o.astype(BF16)
    kc = kc_ref[...].astype(BF16)
    vc = vc_ref[...].astype(BF16)
    for g in range(4):
        o = _pair_attention(_tile(q_ref, 8 + g), [(kc, vc, None)])
        oc_ref[:, g * LANES:(g + 1) * LANES] = o.astype(BF16)


def _ctx_attention(layer, q, ka, va, kb, vb, kc, vc, lam_l, subln_l):
    n_tok = q.shape[0]
    tile = lambda width: pl.BlockSpec((SEQ, width), lambda b: (b, 0))
    slab = lambda width: pl.BlockSpec((None, None, SEQ, width), lambda b: (b, layer, 0, 0))
    const = lambda b: (0, 0)
    return pl.pallas_call(
        functools.partial(_ctx_attn_body, layer),
        out_shape=[jax.ShapeDtypeStruct((n_tok, BRANCH_W), BF16)] * 3,
        grid=(n_tok // SEQ,),
        in_specs=[tile(Q_PACK)] + [slab(w) for w in CACHE_WIDTHS]
                 + [pl.BlockSpec((4, HEAD_DIM), const), pl.BlockSpec((1, 2 * HEAD_DIM), const)],
        out_specs=[tile(BRANCH_W)] * 3,
        compiler_params=_params(1),
        name="context_attention",
    )(q, ka, va, kb, vb, kc, vc, lam_l, subln_l)


N_KEYS = PAST_LEN + DEC_SEQ
N_SUB = DEC_SEQ // TQ_SUB


def _sub_rows(i):
    return pl.ds(pl.multiple_of(i * TQ_SUB, TQ_SUB), TQ_SUB)


def _join_keys(all_ref, ctx_ref, lat_ref):
    all_ref[0:PAST_LEN, :] = ctx_ref[...].astype(BF16)
    all_ref[PAST_LEN:N_KEYS, :] = lat_ref[...]


def _join_values_t(vt_ref, ctx_ref, lat_ref):
    vt_ref[:, 0:PAST_LEN] = ctx_ref[...].T.astype(BF16)
    vt_ref[:, PAST_LEN:N_KEYS] = lat_ref[...].astype(F32).T.astype(BF16)


def _softmax_t(k_ref, q, keep):
    qm = jnp.where(keep, q, jnp.zeros_like(q))
    pieces = [_dot_nt(k_ref[c * KEY_CHUNK:(c + 1) * KEY_CHUNK, :], qm) for c in range(N_KEYS // KEY_CHUNK)]

    def over_keys(x, op):
        return op(op(x.reshape(KEY_CHUNK // 8, 8, x.shape[1]), axis=0), axis=0, keepdims=True)

    m = over_keys(functools.reduce(jnp.maximum, pieces), jnp.max)
    es = [jnp.exp2(s - m) for s in pieces]
    return es, 1.0 / over_keys(functools.reduce(operator.add, es), jnp.sum)


def _values_t(vt, weights):
    return functools.reduce(operator.add, [
        _dot(vt[:, c * KEY_CHUNK:(c + 1) * KEY_CHUNK], w.astype(BF16)) for c, w in enumerate(weights)])


def _lat_diff_body(layer, q1_ref, q2_ref, k1l_ref, k2l_ref, vl_ref, k1c_ref, k2c_ref, vc_ref, lam_ref,
                   subln_ref, o_ref, k1_ref, k2_ref, vt_ref):
    _join_keys(k1_ref, k1c_ref, k1l_ref)
    _join_keys(k2_ref, k2c_ref, k2l_ref)
    _join_values_t(vt_ref, vc_ref, vl_ref)
    lam_v, lam_init = _lambda(lam_ref, layer)
    subln = subln_ref[...] * (1.0 - lam_init)
    lo = (pl.program_id(1) % 2) * HEAD_DIM
    lane = lax.broadcasted_iota(jnp.int32, (TQ_SUB, LANES), 1)
    keep = (lane >= lo) & (lane < lo + HEAD_DIM)

    def sub_tile(i, carry):
        e1, inv1 = _softmax_t(k1_ref, q1_ref[_sub_rows(i), :], keep)
        e2, inv2 = _softmax_t(k2_ref, q2_ref[_sub_rows(i), :], keep)
        ratio = lam_v * inv2 / inv1
        o = (_values_t(vt_ref, [a - b * ratio for a, b in zip(e1, e2)]) * inv1).T
        ms = jnp.mean(o * o, axis=-1, keepdims=True)
        o_ref[_sub_rows(i), :] = (o * lax.rsqrt(ms + RMS_EPS) * subln).astype(BF16)
        return carry

    lax.fori_loop(0, N_SUB, sub_tile, 0, unroll=SUB_UNROLL)


def _lat_diff_attention(layer, q, k, v, cache_k, cache_v, lam_l, subln_l):
    const = lambda b, h: (0, 0)
    seq = lambda col: pl.BlockSpec((DEC_SEQ, LANES), lambda b, h: (b, col(h)))
    past = lambda col: pl.BlockSpec((None, None, PAST_LEN, LANES), lambda b, h: (b, layer, 0, col(h)))
    map1 = lambda h: h // 2
    map2 = lambda h: 2 + h // 2
    head = lambda h: h
    return pl.pallas_call(
        functools.partial(_lat_diff_body, layer),
        out_shape=jax.ShapeDtypeStruct((DEC_BATCH * DEC_SEQ, BRANCH_W), BF16),
        grid=(DEC_BATCH, H_A),
        in_specs=[seq(map1), seq(map2), seq(map1), seq(map2), seq(head),
                  past(map1), past(map2), past(head),
                  pl.BlockSpec((4, HEAD_DIM), const), pl.BlockSpec((1, 2 * HEAD_DIM), const)],
        out_specs=seq(head),
        scratch_shapes=[pltpu.VMEM((N_KEYS, LANES), BF16)] * 2 + [pltpu.VMEM((LANES, N_KEYS), BF16)],
        compiler_params=_params(2),
        name="latent_diff_attention",
    )(q, q, k, k, v, cache_k, cache_k, cache_v, lam_l, subln_l)


def _lat_gqa_body(q_ref, kl_ref, vl_ref, kc_ref, vc_ref, o_ref, k_ref, vt_ref):
    _join_keys(k_ref, kc_ref, kl_ref)
    _join_values_t(vt_ref, vc_ref, vl_ref)
    lane = lax.broadcasted_iota(jnp.int32, (TQ_SUB, LANES), 1)

    def sub_tile(i, carry):
        q = q_ref[_sub_rows(i), :]
        outs = []
        for n in range(KV_C):
            keep = (lane < HEAD_DIM) if n == 0 else (lane >= HEAD_DIM)
            es, inv = _softmax_t(k_ref, q, keep)
            outs.append(_values_t(vt_ref.at[n * HEAD_DIM:(n + 1) * HEAD_DIM], es) * inv)
        o_ref[_sub_rows(i), :] = jnp.concatenate(outs, axis=0).T.astype(BF16)
        return carry

    lax.fori_loop(0, N_SUB, sub_tile, 0, unroll=SUB_UNROLL)


def _lat_gqa_attention(layer, q, k, v, cache_k, cache_v):
    n_pairs = H_C // KV_C
    kv_col = 1024 // LANES
    past = pl.BlockSpec((None, None, PAST_LEN, LANES), lambda b, g: (b, layer, 0, 0))
    return pl.pallas_call(
        _lat_gqa_body,
        out_shape=jax.ShapeDtypeStruct((DEC_BATCH * DEC_SEQ, BRANCH_W), BF16),
        grid=(DEC_BATCH, n_pairs),
        in_specs=[
            pl.BlockSpec((DEC_SEQ, LANES), lambda b, g: (b, 1024 // LANES + g)),
            pl.BlockSpec((DEC_SEQ, LANES), lambda b, g: (b, kv_col)),
            pl.BlockSpec((DEC_SEQ, LANES), lambda b, g: (b, kv_col)),
            past, past,
        ],
        out_specs=pl.BlockSpec((DEC_SEQ, LANES), lambda b, g: (b, g)),
        scratch_shapes=[pltpu.VMEM((N_KEYS, LANES), BF16), pltpu.VMEM((LANES, N_KEYS), BF16)],
        compiler_params=_params(2),
        name="latent_gqa_attention",
    )(q, k, v, cache_k, cache_v)


def _nbr_key_start(g):
    return jnp.clip(g * NBR_ROWS - NA_KH // 2, 0, GRID_H - NBR_KEY_ROWS)


def _lat_nbr_body(q_ref, kl_ref, vl_ref, kc_ref, vc_ref, bias_ref, o_ref):
    base = pl.multiple_of(_nbr_key_start(pl.program_id(1)) * GRID_W, GRID_W)
    n_loc = NBR_KEY_ROWS * GRID_W
    for j in range(4):
        k_loc = kl_ref[pl.ds(base, n_loc), j * LANES:(j + 1) * LANES]
        v_loc = vl_ref[pl.ds(base, n_loc), j * LANES:(j + 1) * LANES]
        segs = [(k_loc, v_loc, lambda half, j=j: bias_ref[2 * j + half]),
                (_tile(kc_ref, j, BF16), _tile(vc_ref, j, BF16), None)]
        o = _pair_attention(_tile(q_ref, j), segs)
        o_ref[:, j * LANES:(j + 1) * LANES] = o.astype(BF16)


def _lat_nbr_attention(layer, q, k, v, cache_k, cache_v, bias_tab):
    n_groups = GRID_H // NBR_ROWS
    tq = NBR_ROWS * GRID_W
    case = lambda g: jnp.where(g == 0, 0, jnp.where(g == n_groups - 1, 2, 1))
    return pl.pallas_call(
        _lat_nbr_body,
        out_shape=jax.ShapeDtypeStruct((DEC_BATCH * DEC_SEQ, BRANCH_W), BF16),
        grid=(DEC_BATCH, n_groups),
        in_specs=[
            pl.BlockSpec((tq, 512), lambda b, g: (b * n_groups + g, 1)),
            pl.BlockSpec((DEC_SEQ, 512), lambda b, g: (b, 1)),
            pl.BlockSpec((DEC_SEQ, 512), lambda b, g: (b, 1)),
            pl.BlockSpec((None, None, PAST_LEN, 512), lambda b, g: (b, layer, 0, 0)),
            pl.BlockSpec((None, None, PAST_LEN, 512), lambda b, g: (b, layer, 0, 0)),
            pl.BlockSpec((None, H_B, tq, NBR_KEY_ROWS * GRID_W), lambda b, g: (case(g), 0, 0, 0)),
        ],
        out_specs=pl.BlockSpec((tq, BRANCH_W), lambda b, g: (b * n_groups + g, 0)),
        compiler_params=_params(2),
        name="latent_neighbourhood_attention",
    )(q, k, v, cache_k, cache_v, bias_tab)


def _merge_body(alpha, x_ref, oa_ref, ob_ref, oc_ref, mod_ref, wg_ref, bg_ref, wbr_ref, wo_ref,
                lng_ref, lnb_ref, out_ref):
    x = x_ref[...]
    sh1 = mod_ref[:, 0:D_MODEL]
    sc1 = mod_ref[:, D_MODEL:2 * D_MODEL]
    g1 = mod_ref[:, 2 * D_MODEL:3 * D_MODEL]
    h = (x * (1.0 + sc1) + sh1).astype(BF16)
    acc = None
    for i, o_ref in enumerate((oa_ref, ob_ref, oc_ref)):
        cols = slice(i * D_MODEL, (i + 1) * D_MODEL)
        gate = jax.nn.sigmoid(_dot(h, wg_ref[:, cols]) + bg_ref[:, cols])
        term = gate * _dot(o_ref[...], wbr_ref[i])
        acc = term if acc is None else acc + term
    y = _dot(acc.astype(BF16), wo_ref[...])
    out_ref[...] = _layernorm(alpha * x + g1 * y, lng_ref[...], lnb_ref[...])


def _merge(alpha, x, oa, ob, oc, mod_l, row_of_tile, wg, bg, wbr, wo, lng, lnb):
    n_tok = x.shape[0]
    const = lambda i: (0, 0)
    tile = lambda width: pl.BlockSpec((TM, width), lambda i: (i, 0))
    return pl.pallas_call(
        functools.partial(_merge_body, alpha),
        out_shape=jax.ShapeDtypeStruct((n_tok, D_MODEL), F32),
        grid=(n_tok // TM,),
        in_specs=[
            tile(D_MODEL), tile(BRANCH_W), tile(BRANCH_W), tile(BRANCH_W),
            pl.BlockSpec((None, 1, 6 * D_MODEL), lambda i: (row_of_tile(i), 0, 0)),
            pl.BlockSpec((D_MODEL, 3 * D_MODEL), const),
            pl.BlockSpec((1, 3 * D_MODEL), const),
            pl.BlockSpec((3, BRANCH_W, D_MODEL), lambda i: (0, 0, 0)),
            pl.BlockSpec((D_MODEL, D_MODEL), const),
            pl.BlockSpec((1, D_MODEL), const),
            pl.BlockSpec((1, D_MODEL), const),
        ],
        out_specs=tile(D_MODEL),
        compiler_params=_params(1),
        name="branch_merge",
    )(x, oa, ob, oc, mod_l, wg, bg, wbr, wo, lng, lnb)


def _routing(logits, bias):
    scores = jax.nn.sigmoid(logits)
    sel_all = scores + bias
    sel = [sel_all[e:e + 1] for e in range(N_EXPERTS)]
    gscore = []
    for g in range(N_GROUPS):
        a, b, c, d = sel[4 * g:4 * g + 4]
        hi1, lo1 = jnp.maximum(a, b), jnp.minimum(a, b)
        hi2, lo2 = jnp.maximum(c, d), jnp.minimum(c, d)
        gscore.append(jnp.maximum(hi1, hi2) + jnp.maximum(jnp.minimum(hi1, hi2), jnp.maximum(lo1, lo2)))
    best = []
    taken = None
    for g in range(N_GROUPS):
        ok = functools.reduce(operator.and_, [gscore[g] >= gscore[k] for k in range(g + 1, N_GROUPS)],
                              jnp.ones_like(gscore[g], dtype=jnp.bool_))
        if taken is not None:
            ok = ok & jnp.logical_not(taken)
        best.append(ok)
        taken = ok if taken is None else (taken | ok)
    picked = []
    for e in range(N_EXPERTS):
        g, i = divmod(e, EXPERTS_PER_GROUP)
        rank = jnp.zeros_like(sel[0])
        for k in range(EXPERTS_PER_GROUP):
            if k == i:
                continue
            other = sel[4 * g + k]
            ahead = (other >= sel[e]) if k < i else (other > sel[e])
            rank = rank + jnp.where(ahead, 1.0, 0.0)
        picked.append((rank < 2.0) & best[g])
    return [picked[4 * g + i] & picked[4 * g + j] for g in range(N_GROUPS) for i, j in GROUP_PAIRS]


def _split_bf16(v):
    hi = v.astype(BF16)
    return hi, (v - hi.astype(F32)).astype(BF16)


TILE_ROWS = D_MODEL // LANES


def _store_token_tiles(ref, val):
    for t0 in range(0, val.shape[0], 8):
        for j in range(TILE_ROWS):
            ref[pl.ds(t0 * TILE_ROWS + j, 8, stride=TILE_ROWS), :] = val[t0:t0 + 8, j * LANES:(j + 1) * LANES]


def _load_token_tiles(ref, n):
    rows = [jnp.concatenate([ref[pl.ds(t0 * TILE_ROWS + j, 8, stride=TILE_ROWS), :] for j in range(TILE_ROWS)],
                            axis=1) for t0 in range(0, n, 8)]
    return jnp.concatenate(rows, axis=0)


def _route_body(x_ref, mod_ref, wrt_ref, rb_ref, tri_ref, cnt_in_ref, h2_ref, meta_ref, cnt_ref):
    @pl.when(pl.program_id(0) == 0)
    def _():
        cnt_ref[...] = cnt_in_ref[...]

    x = x_ref[...]
    sh2 = mod_ref[:, 3 * D_MODEL:4 * D_MODEL]
    sc2 = mod_ref[:, 4 * D_MODEL:5 * D_MODEL]
    h2 = x * (1.0 + sc2) + sh2
    _store_token_tiles(h2_ref, h2)
    hi, lo = _split_bf16(h2)
    wr_hi, wr_lo = _split_bf16(wrt_ref[...])
    logits = _dot_nt(wr_hi, hi) + _dot_nt(wr_hi, lo) + _dot_nt(wr_lo, hi)
    masks = _routing(logits, rb_ref[...])
    tm = x.shape[0]
    onehot = jnp.concatenate([jnp.where(m, 1.0, 0.0) for m in masks]
                             + [jnp.zeros((BUCKET_ROWS - N_BUCKETS, tm), F32)], axis=0)
    ahead = _dot(onehot.astype(BF16), tri_ref[...]) + cnt_ref[:, 0:1]
    rank = jnp.sum(onehot * ahead, axis=0, keepdims=True)
    bucket_id = lax.broadcasted_iota(jnp.int32, onehot.shape, 0).astype(F32)
    bucket = jnp.sum(onehot * bucket_id, axis=0, keepdims=True)
    meta_ref[...] = jnp.concatenate([bucket, rank, jnp.zeros((6, tm), F32)], axis=0).astype(jnp.int32)
    cnt_ref[...] = cnt_ref[...] + jnp.sum(onehot, axis=1, keepdims=True)


def _route(x, mod_l, row_of_tile, wrt, rb, tri, cnt_in):
    n_tok = x.shape[0]
    const = lambda i: (0, 0)
    tile = pl.BlockSpec((TM, D_MODEL), lambda i: (i, 0))
    return pl.pallas_call(
        _route_body,
        out_shape=[jax.ShapeDtypeStruct((n_tok * TILE_ROWS, LANES), F32),
                   jax.ShapeDtypeStruct((8, n_tok), jnp.int32),
                   jax.ShapeDtypeStruct((BUCKET_ROWS, LANES), F32)],
        grid=(n_tok // TM,),
        in_specs=[
            tile,
            pl.BlockSpec((None, 1, 6 * D_MODEL), lambda i: (row_of_tile(i), 0, 0)),
            pl.BlockSpec((N_EXPERTS, D_MODEL), const),
            pl.BlockSpec((N_EXPERTS, 1), const),
            pl.BlockSpec((TM, TM), const),
            pl.BlockSpec((BUCKET_ROWS, LANES), const),
        ],
        out_specs=[pl.BlockSpec((TM * TILE_ROWS, LANES), lambda i: (i, 0)),
                   pl.BlockSpec((8, TM), lambda i: (0, i)), pl.BlockSpec((BUCKET_ROWS, LANES), const)],
        compiler_params=pltpu.CompilerParams(dimension_semantics=("arbitrary",), vmem_limit_bytes=VMEM_LIMIT),
        name="moe_route",
    )(x, mod_l, wrt, rb, tri, cnt_in)


def _row_copy(src_ref, src_row, dst_ref, dst_row, sem):
    src = src_ref.at[pl.ds(pl.multiple_of(src_row * TILE_ROWS, TILE_ROWS), TILE_ROWS)]
    dst = dst_ref.at[pl.ds(pl.multiple_of(dst_row * TILE_ROWS, TILE_ROWS), TILE_ROWS)]
    return pltpu.make_async_copy(src, dst, sem)


def _dest_row(base_ref, bucket_ref, rank_ref, token):
    return base_ref[bucket_ref[token]] + rank_ref[token]


def _scatter_body(base_ref, bucket_ref, rank_ref, h_ref, xs_in_ref, xs_ref, sem):
    del xs_in_ref
    first = pl.program_id(0) * TM

    def issue(t, carry):
        _row_copy(h_ref, t, xs_ref, _dest_row(base_ref, bucket_ref, rank_ref, first + t), sem).start()
        return carry

    def drain(t, carry):
        _row_copy(h_ref, t, xs_ref, 0, sem).wait()
        return carry

    lax.fori_loop(0, TM, issue, 0, unroll=8)
    lax.fori_loop(0, TM, drain, 0, unroll=8)


def _scatter(base, bucket, rank, h2, xs):
    n_tok = h2.shape[0] // TILE_ROWS
    return pl.pallas_call(
        _scatter_body,
        out_shape=jax.ShapeDtypeStruct(xs.shape, xs.dtype),
        grid_spec=pltpu.PrefetchScalarGridSpec(
            num_scalar_prefetch=3,
            grid=(n_tok // TM,),
            in_specs=[pl.BlockSpec((TM * TILE_ROWS, LANES), lambda i, *_: (i, 0)),
                      pl.BlockSpec(memory_space=pl.ANY)],
            out_specs=pl.BlockSpec(memory_space=pl.ANY),
            scratch_shapes=[pltpu.SemaphoreType.DMA(())],
        ),
        input_output_aliases={4: 0},
        compiler_params=pltpu.CompilerParams(dimension_semantics=("arbitrary",), vmem_limit_bytes=VMEM_LIMIT),
        name="moe_scatter",
    )(base, bucket, rank, h2, xs)


def _experts_body(e0_ref, e1_ref, used_ref, xs_ref, wrt_ref, w13a_ref, w2a_ref, w13b_ref, w2b_ref, y_ref):
    j = pl.program_id(0)

    @pl.when(j < used_ref[0])
    def _():
        hi, lo = _split_bf16(_load_token_tiles(xs_ref, TR))
        wr = jnp.concatenate([wrt_ref[pl.ds(e0_ref[j], 1), :], wrt_ref[pl.ds(e1_ref[j], 1), :],
                              jnp.zeros((6, D_MODEL), F32)], axis=0)
        wr_hi, wr_lo = _split_bf16(wr)
        logits = _dot_nt(hi, wr_hi) + _dot_nt(lo, wr_hi) + _dot_nt(hi, wr_lo)
        s = jax.nn.sigmoid(logits)
        s0 = s[:, 0:1]
        s1 = s[:, 1:2]
        inv = 1.0 / (s0 + s1)

        def ffn(w13_ref, w2_ref):
            h13 = _dot(hi, w13_ref[...])
            a = h13[:, :D_EXPERT]
            b = h13[:, D_EXPERT:]
            return _dot((a * jax.nn.sigmoid(a) * b).astype(BF16), w2_ref[...])

        _store_token_tiles(y_ref, (s0 * inv) * ffn(w13a_ref, w2a_ref) + (s1 * inv) * ffn(w13b_ref, w2b_ref))

    @pl.when(j >= used_ref[0])
    def _():
        y_ref[...] = jnp.zeros_like(y_ref)


def _experts(tile_e0, tile_e1, n_used, xs, wrt, w13, w2):
    n_rows = xs.shape[0] // TILE_ROWS
    tile = pl.BlockSpec((TR * TILE_ROWS, LANES), lambda j, *_: (j, 0))
    return pl.pallas_call(
        _experts_body,
        out_shape=jax.ShapeDtypeStruct(xs.shape, F32),
        grid_spec=pltpu.PrefetchScalarGridSpec(
            num_scalar_prefetch=3,
            grid=(n_rows // TR,),
            in_specs=[
                tile,
                pl.BlockSpec((N_EXPERTS, D_MODEL), lambda j, *_: (0, 0)),
                pl.BlockSpec((None, D_MODEL, 2 * D_EXPERT), lambda j, e0, e1, u: (e0[j], 0, 0)),
                pl.BlockSpec((None, D_EXPERT, D_MODEL), lambda j, e0, e1, u: (e0[j], 0, 0)),
                pl.BlockSpec((None, D_MODEL, 2 * D_EXPERT), lambda j, e0, e1, u: (e1[j], 0, 0)),
                pl.BlockSpec((None, D_EXPERT, D_MODEL), lambda j, e0, e1, u: (e1[j], 0, 0)),
            ],
            out_specs=tile,
        ),
        compiler_params=pltpu.CompilerParams(dimension_semantics=("arbitrary",), vmem_limit_bytes=VMEM_LIMIT),
        name="moe_experts",
    )(tile_e0, tile_e1, n_used, xs, wrt, w13, w2, w13, w2)


def _gather_norm_body(alpha, base_ref, bucket_ref, rank_ref, x_ref, mod_ref, y_ref, lng_ref, lnb_ref,
                      out_ref, m_ref, sem):
    i = pl.program_id(0)

    def fetch(step, slot):
        def issue(t, carry):
            src_row = _dest_row(base_ref, bucket_ref, rank_ref, step * TM + t)
            _row_copy(y_ref, src_row, m_ref.at[slot], t, sem.at[slot]).start()
            return carry

        lax.fori_loop(0, TM, issue, 0, unroll=8)

    @pl.when(i == 0)
    def _():
        fetch(0, 0)

    @pl.when(i + 1 < pl.num_programs(0))
    def _():
        fetch(i + 1, (i + 1) % 2)

    slot = i % 2

    def drain(t, carry):
        _row_copy(y_ref, 0, m_ref.at[slot], t, sem.at[slot]).wait()
        return carry

    lax.fori_loop(0, TM, drain, 0, unroll=8)
    g2 = mod_ref[:, 5 * D_MODEL:6 * D_MODEL]
    m = _load_token_tiles(m_ref.at[slot], TM)
    out_ref[...] = _layernorm(alpha * x_ref[...] + g2 * m, lng_ref[...], lnb_ref[...])


def _gather_norm(alpha, base, bucket, rank, x, mod_l, row_of_tile, y, lng, lnb):
    n_tok = x.shape[0]
    const = lambda i, *_: (0, 0)
    tile = pl.BlockSpec((TM, D_MODEL), lambda i, *_: (i, 0))
    return pl.pallas_call(
        functools.partial(_gather_norm_body, alpha),
        out_shape=jax.ShapeDtypeStruct((n_tok, D_MODEL), F32),
        grid_spec=pltpu.PrefetchScalarGridSpec(
            num_scalar_prefetch=3,
            grid=(n_tok // TM,),
            in_specs=[
                tile,
                pl.BlockSpec((None, 1, 6 * D_MODEL), lambda i, *_: (row_of_tile(i), 0, 0)),
                pl.BlockSpec(memory_space=pl.ANY),
                pl.BlockSpec((1, D_MODEL), const),
                pl.BlockSpec((1, D_MODEL), const),
            ],
            out_specs=tile,
            scratch_shapes=[pltpu.VMEM((2, TM * TILE_ROWS, LANES), F32), pltpu.SemaphoreType.DMA((2,))],
        ),
        compiler_params=pltpu.CompilerParams(dimension_semantics=("arbitrary",), vmem_limit_bytes=VMEM_LIMIT),
        name="moe_gather_norm",
    )(base, bucket, rank, x, mod_l, y, lng, lnb)


def _bucket_layout(counts):
    cnt = counts[:N_BUCKETS, 0].astype(jnp.int32)
    padded = (cnt + TR - 1) // TR * TR
    ends = jnp.cumsum(padded)
    base = jnp.zeros((BUCKET_ROWS,), jnp.int32).at[:N_BUCKETS].set(ends - padded)
    tile_start = jnp.arange(SORTED_ROWS // TR, dtype=jnp.int32) * TR
    tile_bucket = jnp.minimum(jnp.sum(tile_start[:, None] >= ends[None, :], axis=1), N_BUCKETS - 1)
    first = jnp.array([4 * g + i for g in range(N_GROUPS) for i, _ in GROUP_PAIRS], jnp.int32)
    second = jnp.array([4 * g + j for g in range(N_GROUPS) for _, j in GROUP_PAIRS], jnp.int32)
    return base, first[tile_bucket], second[tile_bucket], (ends[-1:] // TR).astype(jnp.int32)


def _rope_tables():
    t = jnp.arange(DEC_SEQ)
    row = (t // GRID_W).astype(F32)
    col = (t % GRID_W).astype(F32)
    n = HEAD_DIM // 4
    freqs = ROPE_THETA ** (-jnp.arange(n, dtype=F32) / n)
    ang = jnp.concatenate([row[:, None] * freqs, col[:, None] * freqs], axis=-1)
    cos = jnp.tile(jnp.repeat(jnp.cos(ang), 2, axis=-1), (1, LANES // HEAD_DIM))
    sin = jnp.tile(jnp.repeat(jnp.sin(ang), 2, axis=-1), (1, LANES // HEAD_DIM))
    even = (jnp.arange(LANES) % 2 == 0)[None, :]
    return cos, jnp.where(even, -sin, 0.0), jnp.where(even, 0.0, sin)


def _group_mean_matrix(width):
    g = jnp.arange(width) // HEAD_DIM
    return jnp.where(g[:, None] == g[None, :], 1.0 / HEAD_DIM, 0.0).astype(BF16)


def _neighbourhood_bias(rpb_l):
    cx = jnp.arange(GRID_W)
    col_start = jnp.clip(cx - NA_KW // 2, 0, GRID_W - NA_KW)
    kx = jnp.arange(GRID_W)
    inside_col = (kx[None, :] >= col_start[:, None]) & (kx[None, :] < col_start[:, None] + NA_KW)
    dx = jnp.clip(kx[None, :] - cx[:, None] + (NA_KW - 1), 0, 2 * NA_KW - 2)
    exact = lax.Precision.HIGHEST
    pick_dx = ((dx[:, :, None] == jnp.arange(2 * NA_KW - 1)) & inside_col[:, :, None]).astype(F32)
    n_groups = GRID_H // NBR_ROWS
    i = jnp.arange(NBR_KEY_ROWS)
    tabs = []
    for g in (0, 1, n_groups - 1):
        key_start = min(max(g * NBR_ROWS - NA_KH // 2, 0), GRID_H - NBR_KEY_ROWS)
        r = g * NBR_ROWS + jnp.arange(NBR_ROWS)
        row_start = jnp.clip(r - NA_KH // 2, 0, GRID_H - NA_KH)
        ky = key_start + i
        inside_row = (ky[None, :] >= row_start[:, None]) & (ky[None, :] < row_start[:, None] + NA_KH)
        dy = jnp.clip(ky[None, :] - r[:, None] + (NA_KH - 1), 0, 2 * NA_KH - 2)
        pick_dy = ((dy[:, :, None] == jnp.arange(2 * NA_KH - 1)) & inside_row[:, :, None]).astype(F32)
        by_row = jnp.einsum("hyd,jiy->hjid", rpb_l, pick_dy, precision=exact)
        vals = jnp.einsum("hjid,ckd->hjcik", by_row, pick_dx, precision=exact)
        ok = inside_row[:, None, :, None] & inside_col[None, :, None, :]
        tabs.append(jnp.where(ok[None], vals * LOG2E, NEG_BIAS)
                    .reshape(H_B, NBR_ROWS * GRID_W, NBR_KEY_ROWS * GRID_W))
    return jnp.stack(tabs, axis=0)


def _gqa_regroup(w, axis):
    shape = w.shape
    w = w.reshape(shape[:axis] + (KV_C, H_C // KV_C, HEAD_DIM) + shape[axis + 1:])
    return jnp.swapaxes(w, axis, axis + 1).reshape(shape)


def kernel(x_prompt, x_sample, c, cache_a_k, cache_a_v, cache_b_k, cache_b_v, cache_c_k, cache_c_v,
           c_ctx, w_mod, b_mod, w_in, w_gate, b_gate, lam, a_subln, rpb, c_qnorm, c_knorm,
           w_br, w_o, ln_g, ln_b, w_router, router_bias, moe_w1, moe_w3, moe_w2):
    alpha = (2.0 * DEPTH) ** 0.25
    cond = jnp.concatenate([c_ctx[None, :], c, jnp.zeros((N_COND - 1 - DEC_BATCH, D_MODEL), F32)], axis=0)
    mods = _modulation(cond, w_mod, b_mod).reshape(DEPTH, N_COND, 1, 6 * D_MODEL)

    rope_tabs = _rope_tables()
    g512 = _group_mean_matrix(512)
    g128 = _group_mean_matrix(128)
    wrt = w_router.T
    rb = router_bias.reshape(N_EXPERTS, 1)
    tok = jnp.arange(TM)
    tri = (tok[:, None] < tok[None, :]).astype(BF16)

    ck_a = cache_a_k.reshape(DEC_BATCH, DEPTH, PAST_LEN, 512)
    cv_a = cache_a_v.reshape(DEC_BATCH, DEPTH, PAST_LEN, 512)
    ck_b = cache_b_k.reshape(DEC_BATCH, DEPTH, PAST_LEN, 512)
    cv_b = cache_b_v.reshape(DEC_BATCH, DEPTH, PAST_LEN, 512)
    ck_c = cache_c_k.reshape(DEC_BATCH, DEPTH, PAST_LEN, 128)
    cv_c = cache_c_v.reshape(DEC_BATCH, DEPTH, PAST_LEN, 128)

    ctx_row = lambda i: 0
    lat_tiles = DEC_SEQ // TM
    lat_row = lambda i: 1 + i // lat_tiles

    xp = x_prompt.reshape(BATCH * SEQ, D_MODEL)
    xs = x_sample.reshape(DEC_BATCH * DEC_SEQ, D_MODEL)
    new_kv = [jnp.zeros((BATCH, DEPTH, SEQ, w), F32) for w in CACHE_WIDTHS]
    rows = jnp.zeros((SORTED_ROWS * TILE_ROWS, LANES), F32)
    for l in range(DEPTH):
        w_in_l = w_in[l]
        w_in_l = jnp.concatenate([w_in_l[:, :OFF_QC], _gqa_regroup(w_in_l[:, OFF_QC:OFF_KC], 1),
                                  w_in_l[:, OFF_KC:]], axis=1).astype(BF16)
        qn = jnp.tile(c_qnorm[l], 512 // HEAD_DIM)[None, :]
        kn = jnp.tile(c_knorm[l], 128 // HEAD_DIM)[None, :]
        wg = w_gate[l].astype(BF16)
        bg = b_gate[l][None, :]
        wbr = jnp.stack([w_br[l, 0], w_br[l, 1], _gqa_regroup(w_br[l, 2], 0)], axis=0).astype(BF16)
        wo = w_o[l].astype(BF16)
        w13 = jnp.concatenate([moe_w1[l], moe_w3[l]], axis=-1).astype(BF16)
        w2 = moe_w2[l].astype(BF16)
        lam_l = lam[l]
        subln_l = a_subln[l][None, :]
        bias_tab = _neighbourhood_bias(rpb[l])
        mod_l = mods[l]

        q, *new_kv = _in_proj(xp, mod_l, ctx_row, w_in_l, qn, kn, g512, g128, None, new_kv, l)
        oa, ob, oc = _ctx_attention(l, q, *new_kv, lam_l, subln_l)
        xp = _merge(alpha, xp, oa, ob, oc, mod_l, ctx_row, wg, bg, wbr, wo, ln_g[l, 0:1], ln_b[l, 0:1])

        q, k, v = _in_proj(xs, mod_l, lat_row, w_in_l, qn, kn, g512, g128, rope_tabs)
        oa = _lat_diff_attention(l, q, k, v, ck_a, cv_a, lam_l, subln_l)
        ob = _lat_nbr_attention(l, q, k, v, ck_b, cv_b, bias_tab)
        oc = _lat_gqa_attention(l, q, k, v, ck_c, cv_c)
        xs = _merge(alpha, xs, oa, ob, oc, mod_l, lat_row, wg, bg, wbr, wo, ln_g[l, 0:1], ln_b[l, 0:1])

        h2p, meta_p, counts = _route(xp, mod_l, ctx_row, wrt, rb, tri, jnp.zeros((BUCKET_ROWS, LANES), F32))
        h2s, meta_s, counts = _route(xs, mod_l, lat_row, wrt, rb, tri, counts)
        base, tile_e0, tile_e1, n_used = _bucket_layout(counts)
        rows = _scatter(base, meta_p[0], meta_p[1], h2p, rows)
        rows = _scatter(base, meta_s[0], meta_s[1], h2s, rows)
        y = _experts(tile_e0, tile_e1, n_used, rows, wrt, w13, w2)
        xp = _gather_norm(alpha, base, meta_p[0], meta_p[1], xp, mod_l, ctx_row, y, ln_g[l, 1:2], ln_b[l, 1:2])
        xs = _gather_norm(alpha, base, meta_s[0], meta_s[1], xs, mod_l, lat_row, y, ln_g[l, 1:2], ln_b[l, 1:2])

    def cache(idx, shape):
        return new_kv[idx].reshape(BATCH, DEPTH, SEQ, *shape)

    return (xp.reshape(BATCH, SEQ, D_MODEL), xs.reshape(DEC_BATCH, DEC_SEQ, D_MODEL),
            cache(0, (2, H_A, HEAD_DIM)), cache(1, (H_A, 2 * HEAD_DIM)),
            cache(2, (H_B, HEAD_DIM)), cache(3, (H_B, HEAD_DIM)),
            cache(4, (KV_C, HEAD_DIM)), cache(5, (KV_C, HEAD_DIM)))
```

```python
import functools
import math
import operator

import jax
import jax.numpy as jnp
from jax import lax
from jax.experimental import pallas as pl
from jax.experimental.pallas import tpu as pltpu

F32 = jnp.float32
BF16 = jnp.bfloat16

D_MODEL = 1024
BATCH = 32
SEQ = 256
DEPTH = 2
DEC_BATCH = 4
DEC_SEQ = 4096
PAST_LEN = 256
GRID_W = 64
GRID_H = DEC_SEQ // GRID_W
HEAD_DIM = 64
H_A = 4
H_B = 8
H_C = 8
KV_C = 2
NA_KH = 8
NA_KW = 16
ROPE_THETA = 10000.0
N_EXPERTS = 16
N_GROUPS = 4
EXPERTS_PER_GROUP = N_EXPERTS // N_GROUPS
D_EXPERT = 256
BRANCH_W = 512
LN_EPS = 1e-5
RMS_EPS = 1e-6
LOG2E = math.log2(math.e)
QK_SCALE = HEAD_DIM ** -0.5 * LOG2E
NEG_BIAS = -1e30
NBR_ROWS = 4
NBR_KEY_ROWS = 12

LANES = 128
N_COND = 8
VMEM_LIMIT = 56 * 1024 * 1024

OFF_QA, OFF_KA, OFF_VA, OFF_QB, OFF_KB, OFF_VB, OFF_QC, OFF_KC, OFF_VC = (
    0, 512, 1024, 1536, 2048, 2560, 3072, 3584, 3712)
IN_WIDTH = 3840
CACHE_WIDTHS = (512, 512, 512, 512, 128, 128)
Q_PACK = 1536
KV_PACK = 1152

TM = 256
TR = 256
GROUP_PAIRS = tuple((i, j) for i in range(EXPERTS_PER_GROUP) for j in range(i + 1, EXPERTS_PER_GROUP))
N_BUCKETS = N_GROUPS * len(GROUP_PAIRS)
BUCKET_ROWS = 32
N_TOKENS = BATCH * SEQ + DEC_BATCH * DEC_SEQ
SORTED_ROWS = N_TOKENS + N_BUCKETS * TR
KEY_CHUNK = 256
SUB_UNROLL = 8
TQ_SUB = 256


def _params(n_axes):
    return pltpu.CompilerParams(dimension_semantics=("parallel",) * n_axes,
                                vmem_limit_bytes=VMEM_LIMIT)


def _dot(a, b):
    return jnp.dot(a, b, preferred_element_type=F32)


def _dot_nt(a, b):
    return lax.dot_general(a, b, (((1,), (1,)), ((), ())), preferred_element_type=F32)


def _layernorm(z, g, b):
    mu = jnp.mean(z, axis=-1, keepdims=True)
    zc = z - mu
    var = jnp.mean(zc * zc, axis=-1, keepdims=True)
    return zc * lax.rsqrt(var + LN_EPS) * g + b


def _mod_body(c_ref, w_ref, b_ref, o_ref):
    c = c_ref[...]
    s = (c * jax.nn.sigmoid(c)).astype(BF16)
    o_ref[...] = _dot(s, w_ref[...].astype(BF16)) + b_ref[...]


def _modulation(cond, w_mod, b_mod):
    tn = 1536
    return pl.pallas_call(
        _mod_body,
        out_shape=jax.ShapeDtypeStruct((DEPTH, N_COND, 6 * D_MODEL), F32),
        grid=(DEPTH, 6 * D_MODEL // tn),
        in_specs=[
            pl.BlockSpec((N_COND, D_MODEL), lambda l, j: (0, 0)),
            pl.BlockSpec((None, D_MODEL, tn), lambda l, j: (l, 0, j)),
            pl.BlockSpec((None, 1, tn), lambda l, j: (l, 0, j)),
        ],
        out_specs=pl.BlockSpec((None, N_COND, tn), lambda l, j: (l, 0, j)),
        compiler_params=_params(2),
        name="modulation",
    )(cond, w_mod, b_mod.reshape(DEPTH, 1, 6 * D_MODEL))


def _in_proj_body(rope, x_ref, mod_ref, w_ref, qn_ref, kn_ref, g512_ref, g128_ref, *rest):
    if rope:
        ca_ref, cb_ref, cc_ref = rest[:3]
        outs = rest[3:]
    else:
        outs = rest[len(CACHE_WIDTHS):]
    x = x_ref[...]
    sh1 = mod_ref[:, 0:D_MODEL]
    sc1 = mod_ref[:, D_MODEL:2 * D_MODEL]
    h = (x * (1.0 + sc1) + sh1).astype(BF16)

    def proj(off, width):
        return _dot(h, w_ref[:, off:off + width])

    def rms(t, g_ref, wn_ref):
        t2 = t * t
        hi = t2.astype(BF16)
        lo = (t2 - hi.astype(F32)).astype(BF16)
        ms = _dot(hi, g_ref[...]) + _dot(lo, g_ref[...])
        return t * lax.rsqrt(ms + RMS_EPS) * wn_ref[...]

    def rot(t):
        if not rope:
            return t
        ca, cb, cc = ca_ref[...], cb_ref[...], cc_ref[...]
        pieces = []
        for j in range(t.shape[1] // LANES):
            blk = t[:, j * LANES:(j + 1) * LANES]
            pieces.append(blk * ca + pltpu.roll(blk, LANES - 1, 1) * cb + pltpu.roll(blk, 1, 1) * cc)
        return pieces[0] if len(pieces) == 1 else jnp.concatenate(pieces, axis=1)

    qa = rot(proj(OFF_QA, 512)) * QK_SCALE
    ka = rot(proj(OFF_KA, 512))
    va = proj(OFF_VA, 512)
    qb = proj(OFF_QB, 512) * QK_SCALE
    kb = proj(OFF_KB, 512)
    vb = proj(OFF_VB, 512)
    qc = rot(rms(proj(OFF_QC, 512), g512_ref, qn_ref)) * QK_SCALE
    kc = rot(rms(proj(OFF_KC, 128), g128_ref, kn_ref))
    vc = proj(OFF_VC, 128)

    q_ref = outs[0]
    q_ref[:, 0:512] = qa.astype(BF16)
    q_ref[:, 512:1024] = qb.astype(BF16)
    q_ref[:, 1024:1536] = qc.astype(BF16)
    if rope:
        k_ref, v_ref = outs[1:]
        k_ref[:, 0:512] = ka.astype(BF16)
        k_ref[:, 512:1024] = kb.astype(BF16)
        k_ref[:, 1024:1152] = kc.astype(BF16)
        v_ref[:, 0:512] = va.astype(BF16)
        v_ref[:, 512:1024] = vb.astype(BF16)
        v_ref[:, 1024:1152] = vc.astype(BF16)
    else:
        ka_ref, va_ref, kb_ref, vb_ref, kc_ref, vc_ref = outs[1:]
        ka_ref[...] = ka
        va_ref[...] = va
        kb_ref[...] = kb
        vb_ref[...] = vb
        kc_ref[...] = kc
        vc_ref[...] = vc


def _in_proj(x, mod_l, row_of_tile, w_bf, qn, kn, g512, g128, rope_tabs, caches=None, layer=None):
    n_tok = x.shape[0]
    n_tiles = n_tok // TM
    rope = rope_tabs is not None
    aliases = {}
    const = lambda i: (0, 0)
    in_specs = [
        pl.BlockSpec((TM, D_MODEL), lambda i: (i, 0)),
        pl.BlockSpec((None, 1, 6 * D_MODEL), lambda i: (row_of_tile(i), 0, 0)),
        pl.BlockSpec((D_MODEL, IN_WIDTH), const),
        pl.BlockSpec((1, 512), const),
        pl.BlockSpec((1, 128), const),
        pl.BlockSpec((512, 512), const),
        pl.BlockSpec((128, 128), const),
    ]
    args = [x, mod_l, w_bf, qn, kn, g512, g128]
    tile = lambda width: pl.BlockSpec((TM, width), lambda i: (i, 0))
    if rope:
        tiles_per_seq = DEC_SEQ // TM
        in_specs += [pl.BlockSpec((TM, LANES), lambda i: (i % tiles_per_seq, 0))] * 3
        args += list(rope_tabs)
        out_shape = [jax.ShapeDtypeStruct((n_tok, Q_PACK), BF16),
                     jax.ShapeDtypeStruct((n_tok, KV_PACK), BF16),
                     jax.ShapeDtypeStruct((n_tok, KV_PACK), BF16)]
        out_specs = [tile(Q_PACK), tile(KV_PACK), tile(KV_PACK)]
    else:
        assert TM == SEQ and n_tiles == BATCH
        aliases = {len(args) + k: 1 + k for k in range(len(CACHE_WIDTHS))}
        in_specs += [pl.BlockSpec(memory_space=pl.ANY)] * len(CACHE_WIDTHS)
        args += list(caches)
        out_shape = [jax.ShapeDtypeStruct((n_tok, Q_PACK), BF16)] + [
            jax.ShapeDtypeStruct(c.shape, F32) for c in caches]
        out_specs = [tile(Q_PACK)] + [pl.BlockSpec((None, None, SEQ, w), lambda i: (i, layer, 0, 0))
                                      for w in CACHE_WIDTHS]
    return pl.pallas_call(
        functools.partial(_in_proj_body, rope),
        out_shape=out_shape,
        grid=(n_tiles,),
        in_specs=in_specs,
        out_specs=out_specs,
        input_output_aliases=aliases,
        compiler_params=_params(1),
        name="in_proj_latent" if rope else "in_proj_context",
    )(*args)


def _head_scores(q_pair, half, k_pair):
    lane = lax.broadcasted_iota(jnp.int32, q_pair.shape, 1)
    keep = (lane < HEAD_DIM) if half == 0 else (lane >= HEAD_DIM)
    qm = jnp.where(keep, q_pair, jnp.zeros_like(q_pair))
    return _dot_nt(qm, k_pair)


def _softmax_parts(s_list):
    mx = functools.reduce(jnp.maximum, [jnp.max(s, axis=-1, keepdims=True) for s in s_list])
    e_list = [jnp.exp2(s - mx) for s in s_list]
    den = functools.reduce(operator.add, [jnp.sum(e, axis=-1, keepdims=True) for e in e_list])
    return e_list, 1.0 / den


def _pair_attention(q_pair, segs):
    outs = []
    for half in (0, 1):
        s_list = []
        for k_pair, _, bias in segs:
            s = _head_scores(q_pair, half, k_pair)
            if bias is not None:
                s = s + bias(half)
            s_list.append(s)
        e_list, inv = _softmax_parts(s_list)
        o = functools.reduce(operator.add, [_dot(e.astype(BF16), seg[1]) for e, seg in zip(e_list, segs)])
        outs.append(o * inv)
    lane = lax.broadcasted_iota(jnp.int32, outs[0].shape, 1)
    return jnp.where(lane < HEAD_DIM, outs[0], outs[1])


def _lambda(lam_ref, layer):
    lam_init = 0.8 - 0.6 * math.exp(-0.3 * layer)
    l = lam_ref[...]
    a = jnp.sum(l[0:1] * l[1:2], axis=-1, keepdims=True)
    b = jnp.sum(l[2:3] * l[3:4], axis=-1, keepdims=True)
    return jnp.exp(a) - jnp.exp(b) + lam_init, lam_init


def _diff_attention(q_blk, k_segs, v_segs, lam_v, lam_init, subln):
    outs = []
    for h in range(H_A):
        j, half = h // 2, h % 2
        parts = []
        for m in range(2):
            qb = q_blk(2 * m + j)
            s_list = [_head_scores(qb, half, ks(2 * m + j)) for ks in k_segs]
            parts.append(_softmax_parts(s_list))
        (e1, inv1), (e2, inv2) = parts
        c2 = lam_v * inv2
        o = functools.reduce(operator.add, [
            _dot((e1[s] * inv1 - e2[s] * c2).astype(BF16), v_segs[s](h)) for s in range(len(k_segs))])
        ms = jnp.mean(o * o, axis=-1, keepdims=True)
        outs.append(o * lax.rsqrt(ms + RMS_EPS) * subln * (1.0 - lam_init))
    return outs


def _tile(ref, j, dtype=None):
    t = ref[:, j * LANES:(j + 1) * LANES]
    return t if dtype is None else t.astype(dtype)


def _ctx_attn_body(layer, q_ref, ka_ref, va_ref, kb_ref, vb_ref, kc_ref, vc_ref, lam_ref, subln_ref,
                   oa_ref, ob_ref, oc_ref):
    lam_v, lam_init = _lambda(lam_ref, layer)
    oa = _diff_attention(lambda j: _tile(q_ref, j),
                         [lambda j: _tile(ka_ref, j, BF16)],
                         [lambda h: _tile(va_ref, h, BF16)],
                         lam_v, lam_init, subln_ref[...])
    for h in range(H_A):
        oa_ref[:, h * LANES:(h + 1) * LANES] = oa[h].astype(BF16)
    for j in range(4):
        o = _pair_attention(_tile(q_ref, 4 + j), [(_tile(kb_ref, j, BF16), _tile(vb_ref, j, BF16), None)])
        ob_ref[:, j * LANES:(j + 1) * LANES] = o.astype(BF16)
    kc = kc_ref[...].astype(BF16)
    vc = vc_ref[...].astype(BF16)
    for g in range(4):
        o = _pair_attention(_tile(q_ref, 8 + g), [(kc, vc, None)])
        oc_ref[:, g * LANES:(g + 1) * LANES] = o.astype(BF16)


def _ctx_attention(layer, q, ka, va, kb, vb, kc, vc, lam_l, subln_l):
    n_tok = q.shape[0]
    tile = lambda width: pl.BlockSpec((SEQ, width), lambda b: (b, 0))
    slab = lambda width: pl.BlockSpec((None, None, SEQ, width), lambda b: (b, layer, 0, 0))
    const = lambda b: (0, 0)
    return pl.pallas_call(
        functools.partial(_ctx_attn_body, layer),
        out_shape=[jax.ShapeDtypeStruct((n_tok, BRANCH_W), BF16)] * 3,
        grid=(n_tok // SEQ,),
        in_specs=[tile(Q_PACK)] + [slab(w) for w in CACHE_WIDTHS]
                 + [pl.BlockSpec((4, HEAD_DIM), const), pl.BlockSpec((1, 2 * HEAD_DIM), const)],
        out_specs=[tile(BRANCH_W)] * 3,
        compiler_params=_params(1),
        name="context_attention",
    )(q, ka, va, kb, vb, kc, vc, lam_l, subln_l)


N_KEYS = PAST_LEN + DEC_SEQ
N_SUB = DEC_SEQ // TQ_SUB


def _sub_rows(i):
    return pl.ds(pl.multiple_of(i * TQ_SUB, TQ_SUB), TQ_SUB)


def _join_keys(all_ref, ctx_ref, lat_ref):
    all_ref[0:PAST_LEN, :] = ctx_ref[...].astype(BF16)
    all_ref[PAST_LEN:N_KEYS, :] = lat_ref[...]


def _join_values_t(vt_ref, ctx_ref, lat_ref):
    vt_ref[:, 0:PAST_LEN] = ctx_ref[...].T.astype(BF16)
    vt_ref[:, PAST_LEN:N_KEYS] = lat_ref[...].astype(F32).T.astype(BF16)


def _softmax_t(k_ref, q, keep):
    qm = jnp.where(keep, q, jnp.zeros_like(q))
    pieces = [_dot_nt(k_ref[c * KEY_CHUNK:(c + 1) * KEY_CHUNK, :], qm) for c in range(N_KEYS // KEY_CHUNK)]

    def over_keys(x, op):
        return op(op(x.reshape(KEY_CHUNK // 8, 8, x.shape[1]), axis=0), axis=0, keepdims=True)

    m = over_keys(functools.reduce(jnp.maximum, pieces), jnp.max)
    es = [jnp.exp2(s - m) for s in pieces]
    return es, 1.0 / over_keys(functools.reduce(operator.add, es), jnp.sum)


def _values_t(vt, weights):
    return functools.reduce(operator.add, [
        _dot(vt[:, c * KEY_CHUNK:(c + 1) * KEY_CHUNK], w.astype(BF16)) for c, w in enumerate(weights)])


def _lat_diff_body(layer, q1_ref, q2_ref, k1l_ref, k2l_ref, vl_ref, k1c_ref, k2c_ref, vc_ref, lam_ref,
                   subln_ref, o_ref, k1_ref, k2_ref, vt_ref):
    _join_keys(k1_ref, k1c_ref, k1l_ref)
    _join_keys(k2_ref, k2c_ref, k2l_ref)
    _join_values_t(vt_ref, vc_ref, vl_ref)
    lam_v, lam_init = _lambda(lam_ref, layer)
    subln = subln_ref[...] * (1.0 - lam_init)
    lo = (pl.program_id(1) % 2) * HEAD_DIM
    lane = lax.broadcasted_iota(jnp.int32, (TQ_SUB, LANES), 1)
    keep = (lane >= lo) & (lane < lo + HEAD_DIM)

    def sub_tile(i, carry):
        e1, inv1 = _softmax_t(k1_ref, q1_ref[_sub_rows(i), :], keep)
        e2, inv2 = _softmax_t(k2_ref, q2_ref[_sub_rows(i), :], keep)
        ratio = lam_v * inv2 / inv1
        o = (_values_t(vt_ref, [a - b * ratio for a, b in zip(e1, e2)]) * inv1).T
        ms = jnp.mean(o * o, axis=-1, keepdims=True)
        o_ref[_sub_rows(i), :] = (o * lax.rsqrt(ms + RMS_EPS) * subln).astype(BF16)
        return carry

    lax.fori_loop(0, N_SUB, sub_tile, 0, unroll=SUB_UNROLL)


def _lat_diff_attention(layer, q, k, v, cache_k, cache_v, lam_l, subln_l):
    const = lambda b, h: (0, 0)
    seq = lambda col: pl.BlockSpec((DEC_SEQ, LANES), lambda b, h: (b, col(h)))
    past = lambda col: pl.BlockSpec((None, None, PAST_LEN, LANES), lambda b, h: (b, layer, 0, col(h)))
    map1 = lambda h: h // 2
    map2 = lambda h: 2 + h // 2
    head = lambda h: h
    return pl.pallas_call(
        functools.partial(_lat_diff_body, layer),
        out_shape=jax.ShapeDtypeStruct((DEC_BATCH * DEC_SEQ, BRANCH_W), BF16),
        grid=(DEC_BATCH, H_A),
        in_specs=[seq(map1), seq(map2), seq(map1), seq(map2), seq(head),
                  past(map1), past(map2), past(head),
                  pl.BlockSpec((4, HEAD_DIM), const), pl.BlockSpec((1, 2 * HEAD_DIM), const)],
        out_specs=seq(head),
        scratch_shapes=[pltpu.VMEM((N_KEYS, LANES), BF16)] * 2 + [pltpu.VMEM((LANES, N_KEYS), BF16)],
        compiler_params=_params(2),
        name="latent_diff_attention",
    )(q, q, k, k, v, cache_k, cache_k, cache_v, lam_l, subln_l)


def _lat_gqa_body(q_ref, kl_ref, vl_ref, kc_ref, vc_ref, o_ref, k_ref, vt_ref):
    _join_keys(k_ref, kc_ref, kl_ref)
    _join_values_t(vt_ref, vc_ref, vl_ref)
    lane = lax.broadcasted_iota(jnp.int32, (TQ_SUB, LANES), 1)

    def sub_tile(i, carry):
        q = q_ref[_sub_rows(i), :]
        outs = []
        for n in range(KV_C):
            keep = (lane < HEAD_DIM) if n == 0 else (lane >= HEAD_DIM)
            es, inv = _softmax_t(k_ref, q, keep)
            outs.append(_values_t(vt_ref.at[n * HEAD_DIM:(n + 1) * HEAD_DIM], es) * inv)
        o_ref[_sub_rows(i), :] = jnp.concatenate(outs, axis=0).T.astype(BF16)
        return carry

    lax.fori_loop(0, N_SUB, sub_tile, 0, unroll=SUB_UNROLL)


def _lat_gqa_attention(layer, q, k, v, cache_k, cache_v):
    n_pairs = H_C // KV_C
    kv_col = 1024 // LANES
    past = pl.BlockSpec((None, None, PAST_LEN, LANES), lambda b, g: (b, layer, 0, 0))
    return pl.pallas_call(
        _lat_gqa_body,
        out_shape=jax.ShapeDtypeStruct((DEC_BATCH * DEC_SEQ, BRANCH_W), BF16),
        grid=(DEC_BATCH, n_pairs),
        in_specs=[
            pl.BlockSpec((DEC_SEQ, LANES), lambda b, g: (b, 1024 // LANES + g)),
            pl.BlockSpec((DEC_SEQ, LANES), lambda b, g: (b, kv_col)),
            pl.BlockSpec((DEC_SEQ, LANES), lambda b, g: (b, kv_col)),
            past, past,
        ],
        out_specs=pl.BlockSpec((DEC_SEQ, LANES), lambda b, g: (b, g)),
        scratch_shapes=[pltpu.VMEM((N_KEYS, LANES), BF16), pltpu.VMEM((LANES, N_KEYS), BF16)],
        compiler_params=_params(2),
        name="latent_gqa_attention",
    )(q, k, v, cache_k, cache_v)


def _nbr_key_start(g):
    return jnp.clip(g * NBR_ROWS - NA_KH // 2, 0, GRID_H - NBR_KEY_ROWS)


def _lat_nbr_body(q_ref, kl_ref, vl_ref, kc_ref, vc_ref, bias_ref, o_ref):
    base = pl.multiple_of(_nbr_key_start(pl.program_id(1)) * GRID_W, GRID_W)
    n_loc = NBR_KEY_ROWS * GRID_W
    for j in range(4):
        k_loc = kl_ref[pl.ds(base, n_loc), j * LANES:(j + 1) * LANES]
        v_loc = vl_ref[pl.ds(base, n_loc), j * LANES:(j + 1) * LANES]
        segs = [(k_loc, v_loc, lambda half, j=j: bias_ref[2 * j + half]),
                (_tile(kc_ref, j, BF16), _tile(vc_ref, j, BF16), None)]
        o = _pair_attention(_tile(q_ref, j), segs)
        o_ref[:, j * LANES:(j + 1) * LANES] = o.astype(BF16)


def _lat_nbr_attention(layer, q, k, v, cache_k, cache_v, bias_tab):
    n_groups = GRID_H // NBR_ROWS
    tq = NBR_ROWS * GRID_W
    case = lambda g: jnp.where(g == 0, 0, jnp.where(g == n_groups - 1, 2, 1))
    return pl.pallas_call(
        _lat_nbr_body,
        out_shape=jax.ShapeDtypeStruct((DEC_BATCH * DEC_SEQ, BRANCH_W), BF16),
        grid=(DEC_BATCH, n_groups),
        in_specs=[
            pl.BlockSpec((tq, 512), lambda b, g: (b * n_groups + g, 1)),
            pl.BlockSpec((DEC_SEQ, 512), lambda b, g: (b, 1)),
            pl.BlockSpec((DEC_SEQ, 512), lambda b, g: (b, 1)),
            pl.BlockSpec((None, None, PAST_LEN, 512), lambda b, g: (b, layer, 0, 0)),
            pl.BlockSpec((None, None, PAST_LEN, 512), lambda b, g: (b, layer, 0, 0)),
            pl.BlockSpec((None, H_B, tq, NBR_KEY_ROWS * GRID_W), lambda b, g: (case(g), 0, 0, 0)),
        ],
        out_specs=pl.BlockSpec((tq, BRANCH_W), lambda b, g: (b * n_groups + g, 0)),
        compiler_params=_params(2),
        name="latent_neighbourhood_attention",
    )(q, k, v, cache_k, cache_v, bias_tab)


def _merge_body(alpha, x_ref, oa_ref, ob_ref, oc_ref, mod_ref, wg_ref, bg_ref, wbr_ref, wo_ref,
                lng_ref, lnb_ref, out_ref):
    x = x_ref[...]
    sh1 = mod_ref[:, 0:D_MODEL]
    sc1 = mod_ref[:, D_MODEL:2 * D_MODEL]
    g1 = mod_ref[:, 2 * D_MODEL:3 * D_MODEL]
    h = (x * (1.0 + sc1) + sh1).astype(BF16)
    acc = None
    for i, o_ref in enumerate((oa_ref, ob_ref, oc_ref)):
        cols = slice(i * D_MODEL, (i + 1) * D_MODEL)
        gate = jax.nn.sigmoid(_dot(h, wg_ref[:, cols]) + bg_ref[:, cols])
        term = gate * _dot(o_ref[...], wbr_ref[i])
        acc = term if acc is None else acc + term
    y = _dot(acc.astype(BF16), wo_ref[...])
    out_ref[...] = _layernorm(alpha * x + g1 * y, lng_ref[...], lnb_ref[...])


def _merge(alpha, x, oa, ob, oc, mod_l, row_of_tile, wg, bg, wbr, wo, lng, lnb):
    n_tok = x.shape[0]
    const = lambda i: (0, 0)
    tile = lambda width: pl.BlockSpec((TM, width), lambda i: (i, 0))
    return pl.pallas_call(
        functools.partial(_merge_body, alpha),
        out_shape=jax.ShapeDtypeStruct((n_tok, D_MODEL), F32),
        grid=(n_tok // TM,),
        in_specs=[
            tile(D_MODEL), tile(BRANCH_W), tile(BRANCH_W), tile(BRANCH_W),
            pl.BlockSpec((None, 1, 6 * D_MODEL), lambda i: (row_of_tile(i), 0, 0)),
            pl.BlockSpec((D_MODEL, 3 * D_MODEL), const),
            pl.BlockSpec((1, 3 * D_MODEL), const),
            pl.BlockSpec((3, BRANCH_W, D_MODEL), lambda i: (0, 0, 0)),
            pl.BlockSpec((D_MODEL, D_MODEL), const),
            pl.BlockSpec((1, D_MODEL), const),
            pl.BlockSpec((1, D_MODEL), const),
        ],
        out_specs=tile(D_MODEL),
        compiler_params=_params(1),
        name="branch_merge",
    )(x, oa, ob, oc, mod_l, wg, bg, wbr, wo, lng, lnb)


def _routing(logits, bias):
    scores = jax.nn.sigmoid(logits)
    sel_all = scores + bias
    sel = [sel_all[e:e + 1] for e in range(N_EXPERTS)]
    gscore = []
    for g in range(N_GROUPS):
        a, b, c, d = sel[4 * g:4 * g + 4]
        hi1, lo1 = jnp.maximum(a, b), jnp.minimum(a, b)
        hi2, lo2 = jnp.maximum(c, d), jnp.minimum(c, d)
        gscore.append(jnp.maximum(hi1, hi2) + jnp.maximum(jnp.minimum(hi1, hi2), jnp.maximum(lo1, lo2)))
    best = []
    taken = None
    for g in range(N_GROUPS):
        ok = functools.reduce(operator.and_, [gscore[g] >= gscore[k] for k in range(g + 1, N_GROUPS)],
                              jnp.ones_like(gscore[g], dtype=jnp.bool_))
        if taken is not None:
            ok = ok & jnp.logical_not(taken)
        best.append(ok)
        taken = ok if taken is None else (taken | ok)
    picked = []
    for e in range(N_EXPERTS):
        g, i = divmod(e, EXPERTS_PER_GROUP)
        rank = jnp.zeros_like(sel[0])
        for k in range(EXPERTS_PER_GROUP):
            if k == i:
                continue
            other = sel[4 * g + k]
            ahead = (other >= sel[e]) if k < i else (other > sel[e])
            rank = rank + jnp.where(ahead, 1.0, 0.0)
        picked.append((rank < 2.0) & best[g])
    return [picked[4 * g + i] & picked[4 * g + j] for g in range(N_GROUPS) for i, j in GROUP_PAIRS]


def _split_bf16(v):
    hi = v.astype(BF16)
    return hi, (v - hi.astype(F32)).astype(BF16)


TILE_ROWS = D_MODEL // LANES


def _store_token_tiles(ref, val):
    for t0 in range(0, val.shape[0], 8):
        for j in range(TILE_ROWS):
            ref[pl.ds(t0 * TILE_ROWS + j, 8, stride=TILE_ROWS), :] = val[t0:t0 + 8, j * LANES:(j + 1) * LANES]


def _load_token_tiles(ref, n):
    rows = [jnp.concatenate([ref[pl.ds(t0 * TILE_ROWS + j, 8, stride=TILE_ROWS), :] for j in range(TILE_ROWS)],
                            axis=1) for t0 in range(0, n, 8)]
    return jnp.concatenate(rows, axis=0)


def _route_body(x_ref, mod_ref, wrt_ref, rb_ref, tri_ref, cnt_in_ref, h2_ref, meta_ref, cnt_ref):
    @pl.when(pl.program_id(0) == 0)
    def _():
        cnt_ref[...] = cnt_in_ref[...]

    x = x_ref[...]
    sh2 = mod_ref[:, 3 * D_MODEL:4 * D_MODEL]
    sc2 = mod_ref[:, 4 * D_MODEL:5 * D_MODEL]
    h2 = x * (1.0 + sc2) + sh2
    _store_token_tiles(h2_ref, h2)
    hi, lo = _split_bf16(h2)
    wr_hi, wr_lo = _split_bf16(wrt_ref[...])
    logits = _dot_nt(wr_hi, hi) + _dot_nt(wr_hi, lo) + _dot_nt(wr_lo, hi)
    masks = _routing(logits, rb_ref[...])
    tm = x.shape[0]
    onehot = jnp.concatenate([jnp.where(m, 1.0, 0.0) for m in masks]
                             + [jnp.zeros((BUCKET_ROWS - N_BUCKETS, tm), F32)], axis=0)
    ahead = _dot(onehot.astype(BF16), tri_ref[...]) + cnt_ref[:, 0:1]
    rank = jnp.sum(onehot * ahead, axis=0, keepdims=True)
    bucket_id = lax.broadcasted_iota(jnp.int32, onehot.shape, 0).astype(F32)
    bucket = jnp.sum(onehot * bucket_id, axis=0, keepdims=True)
    meta_ref[...] = jnp.concatenate([bucket, rank, jnp.zeros((6, tm), F32)], axis=0).astype(jnp.int32)
    cnt_ref[...] = cnt_ref[...] + jnp.sum(onehot, axis=1, keepdims=True)


def _route(x, mod_l, row_of_tile, wrt, rb, tri, cnt_in):
    n_tok = x.shape[0]
    const = lambda i: (0, 0)
    tile = pl.BlockSpec((TM, D_MODEL), lambda i: (i, 0))
    return pl.pallas_call(
        _route_body,
        out_shape=[jax.ShapeDtypeStruct((n_tok * TILE_ROWS, LANES), F32),
                   jax.ShapeDtypeStruct((8, n_tok), jnp.int32),
                   jax.ShapeDtypeStruct((BUCKET_ROWS, LANES), F32)],
        grid=(n_tok // TM,),
        in_specs=[
            tile,
            pl.BlockSpec((None, 1, 6 * D_MODEL), lambda i: (row_of_tile(i), 0, 0)),
            pl.BlockSpec((N_EXPERTS, D_MODEL), const),
            pl.BlockSpec((N_EXPERTS, 1), const),
            pl.BlockSpec((TM, TM), const),
            pl.BlockSpec((BUCKET_ROWS, LANES), const),
        ],
        out_specs=[pl.BlockSpec((TM * TILE_ROWS, LANES), lambda i: (i, 0)),
                   pl.BlockSpec((8, TM), lambda i: (0, i)), pl.BlockSpec((BUCKET_ROWS, LANES), const)],
        compiler_params=pltpu.CompilerParams(dimension_semantics=("arbitrary",), vmem_limit_bytes=VMEM_LIMIT),
        name="moe_route",
    )(x, mod_l, wrt, rb, tri, cnt_in)


def _row_copy(src_ref, src_row, dst_ref, dst_row, sem):
    src = src_ref.at[pl.ds(pl.multiple_of(src_row * TILE_ROWS, TILE_ROWS), TILE_ROWS)]
    dst = dst_ref.at[pl.ds(pl.multiple_of(dst_row * TILE_ROWS, TILE_ROWS), TILE_ROWS)]
    return pltpu.make_async_copy(src, dst, sem)


def _dest_row(base_ref, bucket_ref, rank_ref, token):
    return base_ref[bucket_ref[token]] + rank_ref[token]


def _scatter_body(base_ref, bucket_ref, rank_ref, h_ref, xs_in_ref, xs_ref, sem):
    del xs_in_ref
    first = pl.program_id(0) * TM

    def issue(t, carry):
        _row_copy(h_ref, t, xs_ref, _dest_row(base_ref, bucket_ref, rank_ref, first + t), sem).start()
        return carry

    def drain(t, carry):
        _row_copy(h_ref, t, xs_ref, 0, sem).wait()
        return carry

    lax.fori_loop(0, TM, issue, 0, unroll=8)
    lax.fori_loop(0, TM, drain, 0, unroll=8)


def _scatter(base, bucket, rank, h2, xs):
    n_tok = h2.shape[0] // TILE_ROWS
    return pl.pallas_call(
        _scatter_body,
        out_shape=jax.ShapeDtypeStruct(xs.shape, xs.dtype),
        grid_spec=pltpu.PrefetchScalarGridSpec(
            num_scalar_prefetch=3,
            grid=(n_tok // TM,),
            in_specs=[pl.BlockSpec((TM * TILE_ROWS, LANES), lambda i, *_: (i, 0)),
                      pl.BlockSpec(memory_space=pl.ANY)],
            out_specs=pl.BlockSpec(memory_space=pl.ANY),
            scratch_shapes=[pltpu.SemaphoreType.DMA(())],
        ),
        input_output_aliases={4: 0},
        compiler_params=pltpu.CompilerParams(dimension_semantics=("arbitrary",), vmem_limit_bytes=VMEM_LIMIT),
        name="moe_scatter",
    )(base, bucket, rank, h2, xs)


def _experts_body(e0_ref, e1_ref, used_ref, xs_ref, wrt_ref, w13a_ref, w2a_ref, w13b_ref, w2b_ref, y_ref):
    j = pl.program_id(0)

    @pl.when(j < used_ref[0])
    def _():
        hi, lo = _split_bf16(_load_token_tiles(xs_ref, TR))
        wr = jnp.concatenate([wrt_ref[pl.ds(e0_ref[j], 1), :], wrt_ref[pl.ds(e1_ref[j], 1), :],
                              jnp.zeros((6, D_MODEL), F32)], axis=0)
        wr_hi, wr_lo = _split_bf16(wr)
        logits = _dot_nt(hi, wr_hi) + _dot_nt(lo, wr_hi) + _dot_nt(hi, wr_lo)
        s = jax.nn.sigmoid(logits)
        s0 = s[:, 0:1]
        s1 = s[:, 1:2]
        inv = 1.0 / (s0 + s1)

        def ffn(w13_ref, w2_ref):
            h13 = _dot(hi, w13_ref[...])
            a = h13[:, :D_EXPERT]
            b = h13[:, D_EXPERT:]
            return _dot((a * jax.nn.sigmoid(a) * b).astype(BF16), w2_ref[...])

        _store_token_tiles(y_ref, (s0 * inv) * ffn(w13a_ref, w2a_ref) + (s1 * inv) * ffn(w13b_ref, w2b_ref))

    @pl.when(j >= used_ref[0])
    def _():
        y_ref[...] = jnp.zeros_like(y_ref)


def _experts(tile_e0, tile_e1, n_used, xs, wrt, w13, w2):
    n_rows = xs.shape[0] // TILE_ROWS
    tile = pl.BlockSpec((TR * TILE_ROWS, LANES), lambda j, *_: (j, 0))
    return pl.pallas_call(
        _experts_body,
        out_shape=jax.ShapeDtypeStruct(xs.shape, F32),
        grid_spec=pltpu.PrefetchScalarGridSpec(
            num_scalar_prefetch=3,
            grid=(n_rows // TR,),
            in_specs=[
                tile,
                pl.BlockSpec((N_EXPERTS, D_MODEL), lambda j, *_: (0, 0)),
                pl.BlockSpec((None, D_MODEL, 2 * D_EXPERT), lambda j, e0, e1, u: (e0[j], 0, 0)),
                pl.BlockSpec((None, D_EXPERT, D_MODEL), lambda j, e0, e1, u: (e0[j], 0, 0)),
                pl.BlockSpec((None, D_MODEL, 2 * D_EXPERT), lambda j, e0, e1, u: (e1[j], 0, 0)),
                pl.BlockSpec((None, D_EXPERT, D_MODEL), lambda j, e0, e1, u: (e1[j], 0, 0)),
            ],
            out_specs=tile,
        ),
        compiler_params=pltpu.CompilerParams(dimension_semantics=("arbitrary",), vmem_limit_bytes=VMEM_LIMIT),
        name="moe_experts",
    )(tile_e0, tile_e1, n_used, xs, wrt, w13, w2, w13, w2)


def _gather_norm_body(alpha, base_ref, bucket_ref, rank_ref, x_ref, mod_ref, y_ref, lng_ref, lnb_ref,
                      out_ref, m_ref, sem):
    i = pl.program_id(0)

    def fetch(step, slot):
        def issue(t, carry):
            src_row = _dest_row(base_ref, bucket_ref, rank_ref, step * TM + t)
            _row_copy(y_ref, src_row, m_ref.at[slot], t, sem.at[slot]).start()
            return carry

        lax.fori_loop(0, TM, issue, 0, unroll=8)

    @pl.when(i == 0)
    def _():
        fetch(0, 0)

    @pl.when(i + 1 < pl.num_programs(0))
    def _():
        fetch(i + 1, (i + 1) % 2)

    slot = i % 2

    def drain(t, carry):
        _row_copy(y_ref, 0, m_ref.at[slot], t, sem.at[slot]).wait()
        return carry

    lax.fori_loop(0, TM, drain, 0, unroll=8)
    g2 = mod_ref[:, 5 * D_MODEL:6 * D_MODEL]
    m = _load_token_tiles(m_ref.at[slot], TM)
    out_ref[...] = _layernorm(alpha * x_ref[...] + g2 * m, lng_ref[...], lnb_ref[...])


def _gather_norm(alpha, base, bucket, rank, x, mod_l, row_of_tile, y, lng, lnb):
    n_tok = x.shape[0]
    const = lambda i, *_: (0, 0)
    tile = pl.BlockSpec((TM, D_MODEL), lambda i, *_: (i, 0))
    return pl.pallas_call(
        functools.partial(_gather_norm_body, alpha),
        out_shape=jax.ShapeDtypeStruct((n_tok, D_MODEL), F32),
        grid_spec=pltpu.PrefetchScalarGridSpec(
            num_scalar_prefetch=3,
            grid=(n_tok // TM,),
            in_specs=[
                tile,
                pl.BlockSpec((None, 1, 6 * D_MODEL), lambda i, *_: (row_of_tile(i), 0, 0)),
                pl.BlockSpec(memory_space=pl.ANY),
                pl.BlockSpec((1, D_MODEL), const),
                pl.BlockSpec((1, D_MODEL), const),
            ],
            out_specs=tile,
            scratch_shapes=[pltpu.VMEM((2, TM * TILE_ROWS, LANES), F32), pltpu.SemaphoreType.DMA((2,))],
        ),
        compiler_params=pltpu.CompilerParams(dimension_semantics=("arbitrary",), vmem_limit_bytes=VMEM_LIMIT),
        name="moe_gather_norm",
    )(base, bucket, rank, x, mod_l, y, lng, lnb)


def _bucket_layout(counts):
    cnt = counts[:N_BUCKETS, 0].astype(jnp.int32)
    padded = (cnt + TR - 1) // TR * TR
    ends = jnp.cumsum(padded)
    base = jnp.zeros((BUCKET_ROWS,), jnp.int32).at[:N_BUCKETS].set(ends - padded)
    tile_start = jnp.arange(SORTED_ROWS // TR, dtype=jnp.int32) * TR
    tile_bucket = jnp.minimum(jnp.sum(tile_start[:, None] >= ends[None, :], axis=1), N_BUCKETS - 1)
    first = jnp.array([4 * g + i for g in range(N_GROUPS) for i, _ in GROUP_PAIRS], jnp.int32)
    second = jnp.array([4 * g + j for g in range(N_GROUPS) for _, j in GROUP_PAIRS], jnp.int32)
    return base, first[tile_bucket], second[tile_bucket], (ends[-1:] // TR).astype(jnp.int32)


def _rope_tables():
    t = jnp.arange(DEC_SEQ)
    row = (t // GRID_W).astype(F32)
    col = (t % GRID_W).astype(F32)
    n = HEAD_DIM // 4
    freqs = ROPE_THETA ** (-jnp.arange(n, dtype=F32) / n)
    ang = jnp.concatenate([row[:, None] * freqs, col[:, None] * freqs], axis=-1)
    cos = jnp.tile(jnp.repeat(jnp.cos(ang), 2, axis=-1), (1, LANES // HEAD_DIM))
    sin = jnp.tile(jnp.repeat(jnp.sin(ang), 2, axis=-1), (1, LANES // HEAD_DIM))
    even = (jnp.arange(LANES) % 2 == 0)[None, :]
    return cos, jnp.where(even, -sin, 0.0), jnp.where(even, 0.0, sin)


def _group_mean_matrix(width):
    g = jnp.arange(width) // HEAD_DIM
    return jnp.where(g[:, None] == g[None, :], 1.0 / HEAD_DIM, 0.0).astype(BF16)


def _neighbourhood_bias(rpb_l):
    cx = jnp.arange(GRID_W)
    col_start = jnp.clip(cx - NA_KW // 2, 0, GRID_W - NA_KW)
    kx = jnp.arange(GRID_W)
    inside_col = (kx[None, :] >= col_start[:, None]) & (kx[None, :] < col_start[:, None] + NA_KW)
    dx = jnp.clip(kx[None, :] - cx[:, None] + (NA_KW - 1), 0, 2 * NA_KW - 2)
    exact = lax.Precision.HIGHEST
    pick_dx = ((dx[:, :, None] == jnp.arange(2 * NA_KW - 1)) & inside_col[:, :, None]).astype(F32)
    n_groups = GRID_H // NBR_ROWS
    i = jnp.arange(NBR_KEY_ROWS)
    tabs = []
    for g in (0, 1, n_groups - 1):
        key_start = min(max(g * NBR_ROWS - NA_KH // 2, 0), GRID_H - NBR_KEY_ROWS)
        r = g * NBR_ROWS + jnp.arange(NBR_ROWS)
        row_start = jnp.clip(r - NA_KH // 2, 0, GRID_H - NA_KH)
        ky = key_start + i
        inside_row = (ky[None, :] >= row_start[:, None]) & (ky[None, :] < row_start[:, None] + NA_KH)
        dy = jnp.clip(ky[None, :] - r[:, None] + (NA_KH - 1), 0, 2 * NA_KH - 2)
        pick_dy = ((dy[:, :, None] == jnp.arange(2 * NA_KH - 1)) & inside_row[:, :, None]).astype(F32)
        by_row = jnp.einsum("hyd,jiy->hjid", rpb_l, pick_dy, precision=exact)
        vals = jnp.einsum("hjid,ckd->hjcik", by_row, pick_dx, precision=exact)
        ok = inside_row[:, None, :, None] & inside_col[None, :, None, :]
        tabs.append(jnp.where(ok[None], vals * LOG2E, NEG_BIAS)
                    .reshape(H_B, NBR_ROWS * GRID_W, NBR_KEY_ROWS * GRID_W))
    return jnp.stack(tabs, axis=0)


def _gqa_regroup(w, axis):
    shape = w.shape
    w = w.reshape(shape[:axis] + (KV_C, H_C // KV_C, HEAD_DIM) + shape[axis + 1:])
    return jnp.swapaxes(w, axis, axis + 1).reshape(shape)


def kernel(x_prompt, x_sample, c, cache_a_k, cache_a_v, cache_b_k, cache_b_v, cache_c_k, cache_c_v,
           c_ctx, w_mod, b_mod, w_in, w_gate, b_gate, lam, a_subln, rpb, c_qnorm, c_knorm,
           w_br, w_o, ln_g, ln_b, w_router, router_bias, moe_w1, moe_w3, moe_w2):
    alpha = (2.0 * DEPTH) ** 0.25
    cond = jnp.concatenate([c_ctx[None, :], c, jnp.zeros((N_COND - 1 - DEC_BATCH, D_MODEL), F32)], axis=0)
    mods = _modulation(cond, w_mod, b_mod).reshape(DEPTH, N_COND, 1, 6 * D_MODEL)

    rope_tabs = _rope_tables()
    g512 = _group_mean_matrix(512)
    g128 = _group_mean_matrix(128)
    wrt = w_router.T
    rb = router_bias.reshape(N_EXPERTS, 1)
    tok = jnp.arange(TM)
    tri = (tok[:, None] < tok[None, :]).astype(BF16)

    ck_a = cache_a_k.reshape(DEC_BATCH, DEPTH, PAST_LEN, 512)
    cv_a = cache_a_v.reshape(DEC_BATCH, DEPTH, PAST_LEN, 512)
    ck_b = cache_b_k.reshape(DEC_BATCH, DEPTH, PAST_LEN, 512)
    cv_b = cache_b_v.reshape(DEC_BATCH, DEPTH, PAST_LEN, 512)
    ck_c = cache_c_k.reshape(DEC_BATCH, DEPTH, PAST_LEN, 128)
    cv_c = cache_c_v.reshape(DEC_BATCH, DEPTH, PAST_LEN, 128)

    ctx_row = lambda i: 0
    lat_tiles = DEC_SEQ // TM
    lat_row = lambda i: 1 + i // lat_tiles

    xp = x_prompt.reshape(BATCH * SEQ, D_MODEL)
    xs = x_sample.reshape(DEC_BATCH * DEC_SEQ, D_MODEL)
    new_kv = [jnp.zeros((BATCH, DEPTH, SEQ, w), F32) for w in CACHE_WIDTHS]
    rows = jnp.zeros((SORTED_ROWS * TILE_ROWS, LANES), F32)
    for l in range(DEPTH):
        w_in_l = w_in[l]
        w_in_l = jnp.concatenate([w_in_l[:, :OFF_QC], _gqa_regroup(w_in_l[:, OFF_QC:OFF_KC], 1),
                                  w_in_l[:, OFF_KC:]], axis=1).astype(BF16)
        qn = jnp.tile(c_qnorm[l], 512 // HEAD_DIM)[None, :]
        kn = jnp.tile(c_knorm[l], 128 // HEAD_DIM)[None, :]
        wg = w_gate[l].astype(BF16)
        bg = b_gate[l][None, :]
        wbr = jnp.stack([w_br[l, 0], w_br[l, 1], _gqa_regroup(w_br[l, 2], 0)], axis=0).astype(BF16)
        wo = w_o[l].astype(BF16)
        w13 = jnp.concatenate([moe_w1[l], moe_w3[l]], axis=-1).astype(BF16)
        w2 = moe_w2[l].astype(BF16)
        lam_l = lam[l]
        subln_l = a_subln[l][None, :]
        bias_tab = _neighbourhood_bias(rpb[l])
        mod_l = mods[l]

        q, *new_kv = _in_proj(xp, mod_l, ctx_row, w_in_l, qn, kn, g512, g128, None, new_kv, l)
        oa, ob, oc = _ctx_attention(l, q, *new_kv, lam_l, subln_l)
        xp = _merge(alpha, xp, oa, ob, oc, mod_l, ctx_row, wg, bg, wbr, wo, ln_g[l, 0:1], ln_b[l, 0:1])

        q, k, v = _in_proj(xs, mod_l, lat_row, w_in_l, qn, kn, g512, g128, rope_tabs)
        oa = _lat_diff_attention(l, q, k, v, ck_a, cv_a, lam_l, subln_l)
        ob = _lat_nbr_attention(l, q, k, v, ck_b, cv_b, bias_tab)
        oc = _lat_gqa_attention(l, q, k, v, ck_c, cv_c)
        xs = _merge(alpha, xs, oa, ob, oc, mod_l, lat_row, wg, bg, wbr, wo, ln_g[l, 0:1], ln_b[l, 0:1])

        h2p, meta_p, counts = _route(xp, mod_l, ctx_row, wrt, rb, tri, jnp.zeros((BUCKET_ROWS, LANES), F32))
        h2s, meta_s, counts = _route(xs, mod_l, lat_row, wrt, rb, tri, counts)
        base, tile_e0, tile_e1, n_used = _bucket_layout(counts)
        rows = _scatter(base, meta_p[0], meta_p[1], h2p, rows)
        rows = _scatter(base, meta_s[0], meta_s[1], h2s, rows)
        y = _experts(tile_e0, tile_e1, n_used, rows, wrt, w13, w2)
        xp = _gather_norm(alpha, base, meta_p[0], meta_p[1], xp, mod_l, ctx_row, y, ln_g[l, 1:2], ln_b[l, 1:2])
        xs = _gather_norm(alpha, base, meta_s[0], meta_s[1], xs, mod_l, lat_row, y, ln_g[l, 1:2], ln_b[l, 1:2])

    def cache(idx, shape):
        return new_kv[idx].reshape(BATCH, DEPTH, SEQ, *shape)

    return (xp.reshape(BATCH, SEQ, D_MODEL), xs.reshape(DEC_BATCH, DEC_SEQ, D_MODEL),
            cache(0, (2, H_A, HEAD_DIM)), cache(1, (H_A, 2 * HEAD_DIM)),
            cache(2, (H_B, HEAD_DIM)), cache(3, (H_B, HEAD_DIM)),
            cache(4, (KV_C, HEAD_DIM)), cache(5, (KV_C, HEAD_DIM)))
```

```python
import functools
import math
import operator

import jax
import jax.numpy as jnp
from jax import lax
from jax.experimental import pallas as pl
from jax.experimental.pallas import tpu as pltpu

F32 = jnp.float32
BF16 = jnp.bfloat16

D_MODEL = 1024
BATCH = 32
SEQ = 256
DEPTH = 2
DEC_BATCH = 4
DEC_SEQ = 4096
PAST_LEN = 256
GRID_W = 64
GRID_H = DEC_SEQ // GRID_W
HEAD_DIM = 64
H_A = 4
H_B = 8
H_C = 8
KV_C = 2
NA_KH = 8
NA_KW = 16
ROPE_THETA = 10000.0
N_EXPERTS = 16
N_GROUPS = 4
EXPERTS_PER_GROUP = N_EXPERTS // N_GROUPS
D_EXPERT = 256
BRANCH_W = 512
LN_EPS = 1e-5
RMS_EPS = 1e-6
LOG2E = math.log2(math.e)
QK_SCALE = HEAD_DIM ** -0.5 * LOG2E
NEG_BIAS = -1e30
NBR_ROWS = 4
NBR_KEY_ROWS = 12

LANES = 128
N_COND = 8
VMEM_LIMIT = 56 * 1024 * 1024

OFF_QA, OFF_KA, OFF_VA, OFF_QB, OFF_KB, OFF_VB, OFF_QC, OFF_KC, OFF_VC = (
    0, 512, 1024, 1536, 2048, 2560, 3072, 3584, 3712)
IN_WIDTH = 3840
CACHE_WIDTHS = (512, 512, 512, 512, 128, 128)
Q_PACK = 1536
KV_PACK = 1152

TM = 256
TR = 256
GROUP_PAIRS = tuple((i, j) for i in range(EXPERTS_PER_GROUP) for j in range(i + 1, EXPERTS_PER_GROUP))
N_BUCKETS = N_GROUPS * len(GROUP_PAIRS)
BUCKET_ROWS = 32
N_TOKENS = BATCH * SEQ + DEC_BATCH * DEC_SEQ
SORTED_ROWS = N_TOKENS + N_BUCKETS * TR
KEY_CHUNK = 256
SUB_UNROLL = 8
TQ_SUB = 256


def _params(n_axes):
    return pltpu.CompilerParams(dimension_semantics=("parallel",) * n_axes,
                                vmem_limit_bytes=VMEM_LIMIT)


def _dot(a, b):
    return jnp.dot(a, b, preferred_element_type=F32)


def _dot_nt(a, b):
    return lax.dot_general(a, b, (((1,), (1,)), ((), ())), preferred_element_type=F32)


def _layernorm(z, g, b):
    mu = jnp.mean(z, axis=-1, keepdims=True)
    zc = z - mu
    var = jnp.mean(zc * zc, axis=-1, keepdims=True)
    return zc * lax.rsqrt(var + LN_EPS) * g + b


def _mod_body(c_ref, w_ref, b_ref, o_ref):
    c = c_ref[...]
    s = (c * jax.nn.sigmoid(c)).astype(BF16)
    o_ref[...] = _dot(s, w_ref[...].astype(BF16)) + b_ref[...]


def _modulation(cond, w_mod, b_mod):
    tn = 1536
    return pl.pallas_call(
        _mod_body,
        out_shape=jax.ShapeDtypeStruct((DEPTH, N_COND, 6 * D_MODEL), F32),
        grid=(DEPTH, 6 * D_MODEL // tn),
        in_specs=[
            pl.BlockSpec((N_COND, D_MODEL), lambda l, j: (0, 0)),
            pl.BlockSpec((None, D_MODEL, tn), lambda l, j: (l, 0, j)),
            pl.BlockSpec((None, 1, tn), lambda l, j: (l, 0, j)),
        ],
        out_specs=pl.BlockSpec((None, N_COND, tn), lambda l, j: (l, 0, j)),
        compiler_params=_params(2),
        name="modulation",
    )(cond, w_mod, b_mod.reshape(DEPTH, 1, 6 * D_MODEL))


def _in_proj_body(rope, x_ref, mod_ref, w_ref, qn_ref, kn_ref, g512_ref, g128_ref, *rest):
    if rope:
        ca_ref, cb_ref, cc_ref = rest[:3]
        outs = rest[3:]
    else:
        outs = rest[len(CACHE_WIDTHS):]
    x = x_ref[...]
    sh1 = mod_ref[:, 0:D_MODEL]
    sc1 = mod_ref[:, D_MODEL:2 * D_MODEL]
    h = (x * (1.0 + sc1) + sh1).astype(BF16)

    def proj(off, width):
        return _dot(h, w_ref[:, off:off + width])

    def rms(t, g_ref, wn_ref):
        t2 = t * t
        hi = t2.astype(BF16)
        lo = (t2 - hi.astype(F32)).astype(BF16)
        ms = _dot(hi, g_ref[...]) + _dot(lo, g_ref[...])
        return t * lax.rsqrt(ms + RMS_EPS) * wn_ref[...]

    def rot(t):
        if not rope:
            return t
        ca, cb, cc = ca_ref[...], cb_ref[...], cc_ref[...]
        pieces = []
        for j in range(t.shape[1] // LANES):
            blk = t[:, j * LANES:(j + 1) * LANES]
            pieces.append(blk * ca + pltpu.roll(blk, LANES - 1, 1) * cb + pltpu.roll(blk, 1, 1) * cc)
        return pieces[0] if len(pieces) == 1 else jnp.concatenate(pieces, axis=1)

    qa = rot(proj(OFF_QA, 512)) * QK_SCALE
    ka = rot(proj(OFF_KA, 512))
    va = proj(OFF_VA, 512)
    qb = proj(OFF_QB, 512) * QK_SCALE
    kb = proj(OFF_KB, 512)
    vb = proj(OFF_VB, 512)
    qc = rot(rms(proj(OFF_QC, 512), g512_ref, qn_ref)) * QK_SCALE
    kc = rot(rms(proj(OFF_KC, 128), g128_ref, kn_ref))
    vc = proj(OFF_VC, 128)

    q_ref = outs[0]
    q_ref[:, 0:512] = qa.astype(BF16)
    q_ref[:, 512:1024] = qb.astype(BF16)
    q_ref[:, 1024:1536] = qc.astype(BF16)
    if rope:
        k_ref, v_ref = outs[1:]
        k_ref[:, 0:512] = ka.astype(BF16)
        k_ref[:, 512:1024] = kb.astype(BF16)
        k_ref[:, 1024:1152] = kc.astype(BF16)
        v_ref[:, 0:512] = va.astype(BF16)
        v_ref[:, 512:1024] = vb.astype(BF16)
        v_ref[:, 1024:1152] = vc.astype(BF16)
    else:
        ka_ref, va_ref, kb_ref, vb_ref, kc_ref, vc_ref = outs[1:]
        ka_ref[...] = ka
        va_ref[...] = va
        kb_ref[...] = kb
        vb_ref[...] = vb
        kc_ref[...] = kc
        vc_ref[...] = vc


def _in_proj(x, mod_l, row_of_tile, w_bf, qn, kn, g512, g128, rope_tabs, caches=None, layer=None):
    n_tok = x.shape[0]
    n_tiles = n_tok // TM
    rope = rope_tabs is not None
    aliases = {}
    const = lambda i: (0, 0)
    in_specs = [
        pl.BlockSpec((TM, D_MODEL), lambda i: (i, 0)),
        pl.BlockSpec((None, 1, 6 * D_MODEL), lambda i: (row_of_tile(i), 0, 0)),
        pl.BlockSpec((D_MODEL, IN_WIDTH), const),
        pl.BlockSpec((1, 512), const),
        pl.BlockSpec((1, 128), const),
        pl.BlockSpec((512, 512), const),
        pl.BlockSpec((128, 128), const),
    ]
    args = [x, mod_l, w_bf, qn, kn, g512, g128]
    tile = lambda width: pl.BlockSpec((TM, width), lambda i: (i, 0))
    if rope:
        tiles_per_seq = DEC_SEQ // TM
        in_specs += [pl.BlockSpec((TM, LANES), lambda i: (i % tiles_per_seq, 0))] * 3
        args += list(rope_tabs)
        out_shape = [jax.ShapeDtypeStruct((n_tok, Q_PACK), BF16),
                     jax.ShapeDtypeStruct((n_tok, KV_PACK), BF16),
                     jax.ShapeDtypeStruct((n_tok, KV_PACK), BF16)]
        out_specs = [tile(Q_PACK), tile(KV_PACK), tile(KV_PACK)]
    else:
        assert TM == SEQ and n_tiles == BATCH
        aliases = {len(args) + k: 1 + k for k in range(len(CACHE_WIDTHS))}
        in_specs += [pl.BlockSpec(memory_space=pl.ANY)] * len(CACHE_WIDTHS)
        args += list(caches)
        out_shape = [jax.ShapeDtypeStruct((n_tok, Q_PACK), BF16)] + [
            jax.ShapeDtypeStruct(c.shape, F32) for c in caches]
        out_specs = [tile(Q_PACK)] + [pl.BlockSpec((None, None, SEQ, w), lambda i: (i, layer, 0, 0))
                                      for w in CACHE_WIDTHS]
    return pl.pallas_call(
        functools.partial(_in_proj_body, rope),
        out_shape=out_shape,
        grid=(n_tiles,),
        in_specs=in_specs,
        out_specs=out_specs,
        input_output_aliases=aliases,
        compiler_params=_params(1),
        name="in_proj_latent" if rope else "in_proj_context",
    )(*args)


def _head_scores(q_pair, half, k_pair):
    lane = lax.broadcasted_iota(jnp.int32, q_pair.shape, 1)
    keep = (lane < HEAD_DIM) if half == 0 else (lane >= HEAD_DIM)
    qm = jnp.where(keep, q_pair, jnp.zeros_like(q_pair))
    return _dot_nt(qm, k_pair)


def _softmax_parts(s_list):
    mx = functools.reduce(jnp.maximum, [jnp.max(s, axis=-1, keepdims=True) for s in s_list])
    e_list = [jnp.exp2(s - mx) for s in s_list]
    den = functools.reduce(operator.add, [jnp.sum(e, axis=-1, keepdims=True) for e in e_list])
    return e_list, 1.0 / den


def _pair_attention(q_pair, segs):
    outs = []
    for half in (0, 1):
        s_list = []
        for k_pair, _, bias in segs:
            s = _head_scores(q_pair, half, k_pair)
            if bias is not None:
                s = s + bias(half)
            s_list.append(s)
        e_list, inv = _softmax_parts(s_list)
        o = functools.reduce(operator.add, [_dot(e.astype(BF16), seg[1]) for e, seg in zip(e_list, segs)])
        outs.append(o * inv)
    lane = lax.broadcasted_iota(jnp.int32, outs[0].shape, 1)
    return jnp.where(lane < HEAD_DIM, outs[0], outs[1])


def _lambda(lam_ref, layer):
    lam_init = 0.8 - 0.6 * math.exp(-0.3 * layer)
    l = lam_ref[...]
    a = jnp.sum(l[0:1] * l[1:2], axis=-1, keepdims=True)
    b = jnp.sum(l[2:3] * l[3:4], axis=-1, keepdims=True)
    return jnp.exp(a) - jnp.exp(b) + lam_init, lam_init


def _diff_attention(q_blk, k_segs, v_segs, lam_v, lam_init, subln):
    outs = []
    for h in range(H_A):
        j, half = h // 2, h % 2
        parts = []
        for m in range(2):
            qb = q_blk(2 * m + j)
            s_list = [_head_scores(qb, half, ks(2 * m + j)) for ks in k_segs]
            parts.append(_softmax_parts(s_list))
        (e1, inv1), (e2, inv2) = parts
        c2 = lam_v * inv2
        o = functools.reduce(operator.add, [
            _dot((e1[s] * inv1 - e2[s] * c2).astype(BF16), v_segs[s](h)) for s in range(len(k_segs))])
        ms = jnp.mean(o * o, axis=-1, keepdims=True)
        outs.append(o * lax.rsqrt(ms + RMS_EPS) * subln * (1.0 - lam_init))
    return outs


def _tile(ref, j, dtype=None):
    t = ref[:, j * LANES:(j + 1) * LANES]
    return t if dtype is None else t.astype(dtype)


def _ctx_attn_body(layer, q_ref, ka_ref, va_ref, kb_ref, vb_ref, kc_ref, vc_ref, lam_ref, subln_ref,
                   oa_ref, ob_ref, oc_ref):
    lam_v, lam_init = _lambda(lam_ref, layer)
    oa = _diff_attention(lambda j: _tile(q_ref, j),
                         [lambda j: _tile(ka_ref, j, BF16)],
                         [lambda h: _tile(va_ref, h, BF16)],
                         lam_v, lam_init, subln_ref[...])
    for h in range(H_A):
        oa_ref[:, h * LANES:(h + 1) * LANES] = oa[h].astype(BF16)
    for j in range(4):
        o = _pair_attention(_tile(q_ref, 4 + j), [(_tile(kb_ref, j, BF16), _tile(vb_ref, j, BF16), None)])
        ob_ref[:, j * LANES:(j + 1) * LANES] = o.astype(BF16)
    kc = kc_ref[...].astype(BF16)
    vc = vc_ref[...].astype(BF16)
    for g in range(4):
        o = _pair_attention(_tile(q_ref, 8 + g), [(kc, vc, None)])
        oc_ref[:, g * LANES:(g + 1) * LANES] = o.astype(BF16)


def _ctx_attention(layer, q, ka, va, kb, vb, kc, vc, lam_l, subln_l):
    n_tok = q.shape[0]
    tile = lambda width: pl.BlockSpec((SEQ, width), lambda b: (b, 0))
    slab = lambda width: pl.BlockSpec((None, None, SEQ, width), lambda b: (b, layer, 0, 0))
    const = lambda b: (0, 0)
    return pl.pallas_call(
        functools.partial(_ctx_attn_body, layer),
        out_shape=[jax.ShapeDtypeStruct((n_tok, BRANCH_W), BF16)] * 3,
        grid=(n_tok // SEQ,),
        in_specs=[tile(Q_PACK)] + [slab(w) for w in CACHE_WIDTHS]
                 + [pl.BlockSpec((4, HEAD_DIM), const), pl.BlockSpec((1, 2 * HEAD_DIM), const)],
        out_specs=[tile(BRANCH_W)] * 3,
        compiler_params=_params(1),
        name="context_attention",
    )(q, ka, va, kb, vb, kc, vc, lam_l, subln_l)


N_KEYS = PAST_LEN + DEC_SEQ
N_SUB = DEC_SEQ // TQ_SUB


def _sub_rows(i):
    return pl.ds(pl.multiple_of(i * TQ_SUB, TQ_SUB), TQ_SUB)


def _join_keys(all_ref, ctx_ref, lat_ref):
    all_ref[0:PAST_LEN, :] = ctx_ref[...].astype(BF16)
    all_ref[PAST_LEN:N_KEYS, :] = lat_ref[...]


def _join_values_t(vt_ref, ctx_ref, lat_ref):
    vt_ref[:, 0:PAST_LEN] = ctx_ref[...].T.astype(BF16)
    vt_ref[:, PAST_LEN:N_KEYS] = lat_ref[...].astype(F32).T.astype(BF16)


def _softmax_t(k_ref, q, keep):
    qm = jnp.where(keep, q, jnp.zeros_like(q))
    pieces = [_dot_nt(k_ref[c * KEY_CHUNK:(c + 1) * KEY_CHUNK, :], qm) for c in range(N_KEYS // KEY_CHUNK)]

    def over_keys(x, op):
        return op(op(x.reshape(KEY_CHUNK // 8, 8, x.shape[1]), axis=0), axis=0, keepdims=True)

    m = over_keys(functools.reduce(jnp.maximum, pieces), jnp.max)
    es = [jnp.exp2(s - m) for s in pieces]
    return es, 1.0 / over_keys(functools.reduce(operator.add, es), jnp.sum)


def _values_t(vt, weights):
    return functools.reduce(operator.add, [
        _dot(vt[:, c * KEY_CHUNK:(c + 1) * KEY_CHUNK], w.astype(BF16)) for c, w in enumerate(weights)])


def _lat_diff_body(layer, q1_ref, q2_ref, k1l_ref, k2l_ref, vl_ref, k1c_ref, k2c_ref, vc_ref, lam_ref,
                   subln_ref, o_ref, k1_ref, k2_ref, vt_ref):
    _join_keys(k1_ref, k1c_ref, k1l_ref)
    _join_keys(k2_ref, k2c_ref, k2l_ref)
    _join_values_t(vt_ref, vc_ref, vl_ref)
    lam_v, lam_init = _lambda(lam_ref, layer)
    subln = subln_ref[...] * (1.0 - lam_init)
    lo = (pl.program_id(1) % 2) * HEAD_DIM
    lane = lax.broadcasted_iota(jnp.int32, (TQ_SUB, LANES), 1)
    keep = (lane >= lo) & (lane < lo + HEAD_DIM)

    def sub_tile(i, carry):
        e1, inv1 = _softmax_t(k1_ref, q1_ref[_sub_rows(i), :], keep)
        e2, inv2 = _softmax_t(k2_ref, q2_ref[_sub_rows(i), :], keep)
        ratio = lam_v * inv2 / inv1
        o = (_values_t(vt_ref, [a - b * ratio for a, b in zip(e1, e2)]) * inv1).T
        ms = jnp.mean(o * o, axis=-1, keepdims=True)
        o_ref[_sub_rows(i), :] = (o * lax.rsqrt(ms + RMS_EPS) * subln).astype(BF16)
        return carry

    lax.fori_loop(0, N_SUB, sub_tile, 0, unroll=SUB_UNROLL)


def _lat_diff_attention(layer, q, k, v, cache_k, cache_v, lam_l, subln_l):
    const = lambda b, h: (0, 0)
    seq = lambda col: pl.BlockSpec((DEC_SEQ, LANES), lambda b, h: (b, col(h)))
    past = lambda col: pl.BlockSpec((None, None, PAST_LEN, LANES), lambda b, h: (b, layer, 0, col(h)))
    map1 = lambda h: h // 2
    map2 = lambda h: 2 + h // 2
    head = lambda h: h
    return pl.pallas_call(
        functools.partial(_lat_diff_body, layer),
        out_shape=jax.ShapeDtypeStruct((DEC_BATCH * DEC_SEQ, BRANCH_W), BF16),
        grid=(DEC_BATCH, H_A),
        in_specs=[seq(map1), seq(map2), seq(map1), seq(map2), seq(head),
                  past(map1), past(map2), past(head),
                  pl.BlockSpec((4, HEAD_DIM), const), pl.BlockSpec((1, 2 * HEAD_DIM), const)],
        out_specs=seq(head),
        scratch_shapes=[pltpu.VMEM((N_KEYS, LANES), BF16)] * 2 + [pltpu.VMEM((LANES, N_KEYS), BF16)],
        compiler_params=_params(2),
        name="latent_diff_attention",
    )(q, q, k, k, v, cache_k, cache_k, cache_v, lam_l, subln_l)


def _lat_gqa_body(q_ref, kl_ref, vl_ref, kc_ref, vc_ref, o_ref, k_ref, vt_ref):
    _join_keys(k_ref, kc_ref, kl_ref)
    _join_values_t(vt_ref, vc_ref, vl_ref)
    lane = lax.broadcasted_iota(jnp.int32, (TQ_SUB, LANES), 1)

    def sub_tile(i, carry):
        q = q_ref[_sub_rows(i), :]
        outs = []
        for n in range(KV_C):
            keep = (lane < HEAD_DIM) if n == 0 else (lane >= HEAD_DIM)
            es, inv = _softmax_t(k_ref, q, keep)
            outs.append(_values_t(vt_ref.at[n * HEAD_DIM:(n + 1) * HEAD_DIM], es) * inv)
        o_ref[_sub_rows(i), :] = jnp.concatenate(outs, axis=0).T.astype(BF16)
        return carry

    lax.fori_loop(0, N_SUB, sub_tile, 0, unroll=SUB_UNROLL)


def _lat_gqa_attention(layer, q, k, v, cache_k, cache_v):
    n_pairs = H_C // KV_C
    kv_col = 1024 // LANES
    past = pl.BlockSpec((None, None, PAST_LEN, LANES), lambda b, g: (b, layer, 0, 0))
    return pl.pallas_call(
        _lat_gqa_body,
        out_shape=jax.ShapeDtypeStruct((DEC_BATCH * DEC_SEQ, BRANCH_W), BF16),
        grid=(DEC_BATCH, n_pairs),
        in_specs=[
            pl.BlockSpec((DEC_SEQ, LANES), lambda b, g: (b, 1024 // LANES + g)),
            pl.BlockSpec((DEC_SEQ, LANES), lambda b, g: (b, kv_col)),
            pl.BlockSpec((DEC_SEQ, LANES), lambda b, g: (b, kv_col)),
            past, past,
        ],
        out_specs=pl.BlockSpec((DEC_SEQ, LANES), lambda b, g: (b, g)),
        scratch_shapes=[pltpu.VMEM((N_KEYS, LANES), BF16), pltpu.VMEM((LANES, N_KEYS), BF16)],
        compiler_params=_params(2),
        name="latent_gqa_attention",
    )(q, k, v, cache_k, cache_v)


def _nbr_key_start(g):
    return jnp.clip(g * NBR_ROWS - NA_KH // 2, 0, GRID_H - NBR_KEY_ROWS)


def _lat_nbr_body(q_ref, kl_ref, vl_ref, kc_ref, vc_ref, bias_ref, o_ref):
    base = pl.multiple_of(_nbr_key_start(pl.program_id(1)) * GRID_W, GRID_W)
    n_loc = NBR_KEY_ROWS * GRID_W
    for j in range(4):
        k_loc = kl_ref[pl.ds(base, n_loc), j * LANES:(j + 1) * LANES]
        v_loc = vl_ref[pl.ds(base, n_loc), j * LANES:(j + 1) * LANES]
        segs = [(k_loc, v_loc, lambda half, j=j: bias_ref[2 * j + half]),
                (_tile(kc_ref, j, BF16), _tile(vc_ref, j, BF16), None)]
        o = _pair_attention(_tile(q_ref, j), segs)
        o_ref[:, j * LANES:(j + 1) * LANES] = o.astype(BF16)


def _lat_nbr_attention(layer, q, k, v, cache_k, cache_v, bias_tab):
    n_groups = GRID_H // NBR_ROWS
    tq = NBR_ROWS * GRID_W
    case = lambda g: jnp.where(g == 0, 0, jnp.where(g == n_groups - 1, 2, 1))
    return pl.pallas_call(
        _lat_nbr_body,
        out_shape=jax.ShapeDtypeStruct((DEC_BATCH * DEC_SEQ, BRANCH_W), BF16),
        grid=(DEC_BATCH, n_groups),
        in_specs=[
            pl.BlockSpec((tq, 512), lambda b, g: (b * n_groups + g, 1)),
            pl.BlockSpec((DEC_SEQ, 512), lambda b, g: (b, 1)),
            pl.BlockSpec((DEC_SEQ, 512), lambda b, g: (b, 1)),
            pl.BlockSpec((None, None, PAST_LEN, 512), lambda b, g: (b, layer, 0, 0)),
            pl.BlockSpec((None, None, PAST_LEN, 512), lambda b, g: (b, layer, 0, 0)),
            pl.BlockSpec((None, H_B, tq, NBR_KEY_ROWS * GRID_W), lambda b, g: (case(g), 0, 0, 0)),
        ],
        out_specs=pl.BlockSpec((tq, BRANCH_W), lambda b, g: (b * n_groups + g, 0)),
        compiler_params=_params(2),
        name="latent_neighbourhood_attention",
    )(q, k, v, cache_k, cache_v, bias_tab)


def _merge_body(alpha, x_ref, oa_ref, ob_ref, oc_ref, mod_ref, wg_ref, bg_ref, wbr_ref, wo_ref,
                lng_ref, lnb_ref, out_ref):
    x = x_ref[...]
    sh1 = mod_ref[:, 0:D_MODEL]
    sc1 = mod_ref[:, D_MODEL:2 * D_MODEL]
    g1 = mod_ref[:, 2 * D_MODEL:3 * D_MODEL]
    h = (x * (1.0 + sc1) + sh1).astype(BF16)
    acc = None
    for i, o_ref in enumerate((oa_ref, ob_ref, oc_ref)):
        cols = slice(i * D_MODEL, (i + 1) * D_MODEL)
        gate = jax.nn.sigmoid(_dot(h, wg_ref[:, cols]) + bg_ref[:, cols])
        term = gate * _dot(o_ref[...], wbr_ref[i])
        acc = term if acc is None else acc + term
    y = _dot(acc.astype(BF16), wo_ref[...])
    out_ref[...] = _layernorm(alpha * x + g1 * y, lng_ref[...], lnb_ref[...])


def _merge(alpha, x, oa, ob, oc, mod_l, row_of_tile, wg, bg, wbr, wo, lng, lnb):
    n_tok = x.shape[0]
    const = lambda i: (0, 0)
    tile = lambda width: pl.BlockSpec((TM, width), lambda i: (i, 0))
    return pl.pallas_call(
        functools.partial(_merge_body, alpha),
        out_shape=jax.ShapeDtypeStruct((n_tok, D_MODEL), F32),
        grid=(n_tok // TM,),
        in_specs=[
            tile(D_MODEL), tile(BRANCH_W), tile(BRANCH_W), tile(BRANCH_W),
            pl.BlockSpec((None, 1, 6 * D_MODEL), lambda i: (row_of_tile(i), 0, 0)),
            pl.BlockSpec((D_MODEL, 3 * D_MODEL), const),
            pl.BlockSpec((1, 3 * D_MODEL), const),
            pl.BlockSpec((3, BRANCH_W, D_MODEL), lambda i: (0, 0, 0)),
            pl.BlockSpec((D_MODEL, D_MODEL), const),
            pl.BlockSpec((1, D_MODEL), const),
            pl.BlockSpec((1, D_MODEL), const),
        ],
        out_specs=tile(D_MODEL),
        compiler_params=_params(1),
        name="branch_merge",
    )(x, oa, ob, oc, mod_l, wg, bg, wbr, wo, lng, lnb)


def _routing(logits, bias):
    scores = jax.nn.sigmoid(logits)
    sel_all = scores + bias
    sel = [sel_all[e:e + 1] for e in range(N_EXPERTS)]
    gscore = []
    for g in range(N_GROUPS):
        a, b, c, d = sel[4 * g:4 * g + 4]
        hi1, lo1 = jnp.maximum(a, b), jnp.minimum(a, b)
        hi2, lo2 = jnp.maximum(c, d), jnp.minimum(c, d)
        gscore.append(jnp.maximum(hi1, hi2) + jnp.maximum(jnp.minimum(hi1, hi2), jnp.maximum(lo1, lo2)))
    best = []
    taken = None
    for g in range(N_GROUPS):
        ok = functools.reduce(operator.and_, [gscore[g] >= gscore[k] for k in range(g + 1, N_GROUPS)],
                              jnp.ones_like(gscore[g], dtype=jnp.bool_))
        if taken is not None:
            ok = ok & jnp.logical_not(taken)
        best.append(ok)
        taken = ok if taken is None else (taken | ok)
    picked = []
    for e in range(N_EXPERTS):
        g, i = divmod(e, EXPERTS_PER_GROUP)
        rank = jnp.zeros_like(sel[0])
        for k in range(EXPERTS_PER_GROUP):
            if k == i:
                continue
            other = sel[4 * g + k]
            ahead = (other >= sel[e]) if k < i else (other > sel[e])
            rank = rank + jnp.where(ahead, 1.0, 0.0)
        picked.append((rank < 2.0) & best[g])
    return [picked[4 * g + i] & picked[4 * g + j] for g in range(N_GROUPS) for i, j in GROUP_PAIRS]


def _split_bf16(v):
    hi = v.astype(BF16)
    return hi, (v - hi.astype(F32)).astype(BF16)


TILE_ROWS = D_MODEL // LANES


def _store_token_tiles(ref, val):
    for t0 in range(0, val.shape[0], 8):
        for j in range(TILE_ROWS):
            ref[pl.ds(t0 * TILE_ROWS + j, 8, stride=TILE_ROWS), :] = val[t0:t0 + 8, j * LANES:(j + 1) * LANES]


def _load_token_tiles(ref, n):
    rows = [jnp.concatenate([ref[pl.ds(t0 * TILE_ROWS + j, 8, stride=TILE_ROWS), :] for j in range(TILE_ROWS)],
                            axis=1) for t0 in range(0, n, 8)]
    return jnp.concatenate(rows, axis=0)


def _route_body(x_ref, mod_ref, wrt_ref, rb_ref, tri_ref, cnt_in_ref, h2_ref, meta_ref, cnt_ref):
    @pl.when(pl.program_id(0) == 0)
    def _():
        cnt_ref[...] = cnt_in_ref[...]

    x = x_ref[...]
    sh2 = mod_ref[:, 3 * D_MODEL:4 * D_MODEL]
    sc2 = mod_ref[:, 4 * D_MODEL:5 * D_MODEL]
    h2 = x * (1.0 + sc2) + sh2
    _store_token_tiles(h2_ref, h2)
    hi, lo = _split_bf16(h2)
    wr_hi, wr_lo = _split_bf16(wrt_ref[...])
    logits = _dot_nt(wr_hi, hi) + _dot_nt(wr_hi, lo) + _dot_nt(wr_lo, hi)
    masks = _routing(logits, rb_ref[...])
    tm = x.shape[0]
    onehot = jnp.concatenate([jnp.where(m, 1.0, 0.0) for m in masks]
                             + [jnp.zeros((BUCKET_ROWS - N_BUCKETS, tm), F32)], axis=0)
    ahead = _dot(onehot.astype(BF16), tri_ref[...]) + cnt_ref[:, 0:1]
    rank = jnp.sum(onehot * ahead, axis=0, keepdims=True)
    bucket_id = lax.broadcasted_iota(jnp.int32, onehot.shape, 0).astype(F32)
    bucket = jnp.sum(onehot * bucket_id, axis=0, keepdims=True)
    meta_ref[...] = jnp.concatenate([bucket, rank, jnp.zeros((6, tm), F32)], axis=0).astype(jnp.int32)
    cnt_ref[...] = cnt_ref[...] + jnp.sum(onehot, axis=1, keepdims=True)


def _route(x, mod_l, row_of_tile, wrt, rb, tri, cnt_in):
    n_tok = x.shape[0]
    const = lambda i: (0, 0)
    tile = pl.BlockSpec((TM, D_MODEL), lambda i: (i, 0))
    return pl.pallas_call(
        _route_body,
        out_shape=[jax.ShapeDtypeStruct((n_tok * TILE_ROWS, LANES), F32),
                   jax.ShapeDtypeStruct((8, n_tok), jnp.int32),
                   jax.ShapeDtypeStruct((BUCKET_ROWS, LANES), F32)],
        grid=(n_tok // TM,),
        in_specs=[
            tile,
            pl.BlockSpec((None, 1, 6 * D_MODEL), lambda i: (row_of_tile(i), 0, 0)),
            pl.BlockSpec((N_EXPERTS, D_MODEL), const),
            pl.BlockSpec((N_EXPERTS, 1), const),
            pl.BlockSpec((TM, TM), const),
            pl.BlockSpec((BUCKET_ROWS, LANES), const),
        ],
        out_specs=[pl.BlockSpec((TM * TILE_ROWS, LANES), lambda i: (i, 0)),
                   pl.BlockSpec((8, TM), lambda i: (0, i)), pl.BlockSpec((BUCKET_ROWS, LANES), const)],
        compiler_params=pltpu.CompilerParams(dimension_semantics=("arbitrary",), vmem_limit_bytes=VMEM_LIMIT),
        name="moe_route",
    )(x, mod_l, wrt, rb, tri, cnt_in)


def _row_copy(src_ref, src_row, dst_ref, dst_row, sem):
    src = src_ref.at[pl.ds(pl.multiple_of(src_row * TILE_ROWS, TILE_ROWS), TILE_ROWS)]
    dst = dst_ref.at[pl.ds(pl.multiple_of(dst_row * TILE_ROWS, TILE_ROWS), TILE_ROWS)]
    return pltpu.make_async_copy(src, dst, sem)


def _issue_rows(n, start_copy):
    group = 8

    def body(g, carry):
        for u in range(group):
            start_copy(g * group + u, u % 2)
        return carry

    lax.fori_loop(0, n // group, body, 0)


def _dest_row(base_ref, bucket_ref, rank_ref, token):
    return base_ref[bucket_ref[token]] + rank_ref[token]


def _scatter_body(base_ref, bucket_ref, rank_ref, h_ref, xs_in_ref, xs_ref, sem):
    del xs_in_ref
    first = pl.program_id(0) * TM

    def issue(t, priority):
        dst_row = _dest_row(base_ref, bucket_ref, rank_ref, first + t)
        _row_copy(h_ref, t, xs_ref, dst_row, sem).start(priority=priority)

    def drain(t, carry):
        _row_copy(h_ref, t, xs_ref, 0, sem).wait()
        return carry

    _issue_rows(TM, issue)
    lax.fori_loop(0, TM, drain, 0, unroll=8)


def _scatter(base, bucket, rank, h2, xs):
    n_tok = h2.shape[0] // TILE_ROWS
    return pl.pallas_call(
        _scatter_body,
        out_shape=jax.ShapeDtypeStruct(xs.shape, xs.dtype),
        grid_spec=pltpu.PrefetchScalarGridSpec(
            num_scalar_prefetch=3,
            grid=(n_tok // TM,),
            in_specs=[pl.BlockSpec((TM * TILE_ROWS, LANES), lambda i, *_: (i, 0)),
                      pl.BlockSpec(memory_space=pl.ANY)],
            out_specs=pl.BlockSpec(memory_space=pl.ANY),
            scratch_shapes=[pltpu.SemaphoreType.DMA(())],
        ),
        input_output_aliases={4: 0},
        compiler_params=pltpu.CompilerParams(dimension_semantics=("arbitrary",), vmem_limit_bytes=VMEM_LIMIT),
        name="moe_scatter",
    )(base, bucket, rank, h2, xs)


def _experts_body(e0_ref, e1_ref, used_ref, xs_ref, wrt_ref, w13a_ref, w2a_ref, w13b_ref, w2b_ref, y_ref):
    j = pl.program_id(0)

    @pl.when(j < used_ref[0])
    def _():
        hi, lo = _split_bf16(_load_token_tiles(xs_ref, TR))
        wr = jnp.concatenate([wrt_ref[pl.ds(e0_ref[j], 1), :], wrt_ref[pl.ds(e1_ref[j], 1), :],
                              jnp.zeros((6, D_MODEL), F32)], axis=0)
        wr_hi, wr_lo = _split_bf16(wr)
        logits = _dot_nt(hi, wr_hi) + _dot_nt(lo, wr_hi) + _dot_nt(hi, wr_lo)
        s = jax.nn.sigmoid(logits)
        s0 = s[:, 0:1]
        s1 = s[:, 1:2]
        inv = 1.0 / (s0 + s1)

        def ffn(w13_ref, w2_ref):
            h13 = _dot(hi, w13_ref[...])
            a = h13[:, :D_EXPERT]
            b = h13[:, D_EXPERT:]
            return _dot((a * jax.nn.sigmoid(a) * b).astype(BF16), w2_ref[...])

        _store_token_tiles(y_ref, (s0 * inv) * ffn(w13a_ref, w2a_ref) + (s1 * inv) * ffn(w13b_ref, w2b_ref))

    @pl.when(j >= used_ref[0])
    def _():
        y_ref[...] = jnp.zeros_like(y_ref)


def _experts(tile_e0, tile_e1, n_used, xs, wrt, w13, w2):
    n_rows = xs.shape[0] // TILE_ROWS
    tile = pl.BlockSpec((TR * TILE_ROWS, LANES), lambda j, *_: (j, 0))
    return pl.pallas_call(
        _experts_body,
        out_shape=jax.ShapeDtypeStruct(xs.shape, F32),
        grid_spec=pltpu.PrefetchScalarGridSpec(
            num_scalar_prefetch=3,
            grid=(n_rows // TR,),
            in_specs=[
                tile,
                pl.BlockSpec((N_EXPERTS, D_MODEL), lambda j, *_: (0, 0)),
                pl.BlockSpec((None, D_MODEL, 2 * D_EXPERT), lambda j, e0, e1, u: (e0[j], 0, 0)),
                pl.BlockSpec((None, D_EXPERT, D_MODEL), lambda j, e0, e1, u: (e0[j], 0, 0)),
                pl.BlockSpec((None, D_MODEL, 2 * D_EXPERT), lambda j, e0, e1, u: (e1[j], 0, 0)),
                pl.BlockSpec((None, D_EXPERT, D_MODEL), lambda j, e0, e1, u: (e1[j], 0, 0)),
            ],
            out_specs=tile,
        ),
        compiler_params=pltpu.CompilerParams(dimension_semantics=("arbitrary",), vmem_limit_bytes=VMEM_LIMIT),
        name="moe_experts",
    )(tile_e0, tile_e1, n_used, xs, wrt, w13, w2, w13, w2)


def _gather_norm_body(alpha, base_ref, bucket_ref, rank_ref, x_ref, mod_ref, y_ref, lng_ref, lnb_ref,
                      out_ref, m_ref, sem):
    i = pl.program_id(0)

    def fetch(step, slot):
        def issue(t, priority):
            src_row = _dest_row(base_ref, bucket_ref, rank_ref, step * TM + t)
            _row_copy(y_ref, src_row, m_ref.at[slot], t, sem.at[slot]).start(priority=priority)

        _issue_rows(TM, issue)

    @pl.when(i == 0)
    def _():
        fetch(0, 0)

    @pl.when(i + 1 < pl.num_programs(0))
    def _():
        fetch(i + 1, (i + 1) % 2)

    slot = i % 2

    def drain(t, carry):
        _row_copy(y_ref, 0, m_ref.at[slot], t, sem.at[slot]).wait()
        return carry

    lax.fori_loop(0, TM, drain, 0, unroll=8)
    g2 = mod_ref[:, 5 * D_MODEL:6 * D_MODEL]
    m = _load_token_tiles(m_ref.at[slot], TM)
    out_ref[...] = _layernorm(alpha * x_ref[...] + g2 * m, lng_ref[...], lnb_ref[...])


def _gather_norm(alpha, base, bucket, rank, x, mod_l, row_of_tile, y, lng, lnb):
    n_tok = x.shape[0]
    const = lambda i, *_: (0, 0)
    tile = pl.BlockSpec((TM, D_MODEL), lambda i, *_: (i, 0))
    return pl.pallas_call(
        functools.partial(_gather_norm_body, alpha),
        out_shape=jax.ShapeDtypeStruct((n_tok, D_MODEL), F32),
        grid_spec=pltpu.PrefetchScalarGridSpec(
            num_scalar_prefetch=3,
            grid=(n_tok // TM,),
            in_specs=[
                tile,
                pl.BlockSpec((None, 1, 6 * D_MODEL), lambda i, *_: (row_of_tile(i), 0, 0)),
                pl.BlockSpec(memory_space=pl.ANY),
                pl.BlockSpec((1, D_MODEL), const),
                pl.BlockSpec((1, D_MODEL), const),
            ],
            out_specs=tile,
            scratch_shapes=[pltpu.VMEM((2, TM * TILE_ROWS, LANES), F32), pltpu.SemaphoreType.DMA((2,))],
        ),
        compiler_params=pltpu.CompilerParams(dimension_semantics=("arbitrary",), vmem_limit_bytes=VMEM_LIMIT),
        name="moe_gather_norm",
    )(base, bucket, rank, x, mod_l, y, lng, lnb)


def _bucket_layout(counts):
    cnt = counts[:N_BUCKETS, 0].astype(jnp.int32)
    padded = (cnt + TR - 1) // TR * TR
    ends = jnp.cumsum(padded)
    base = jnp.zeros((BUCKET_ROWS,), jnp.int32).at[:N_BUCKETS].set(ends - padded)
    tile_start = jnp.arange(SORTED_ROWS // TR, dtype=jnp.int32) * TR
    tile_bucket = jnp.minimum(jnp.sum(tile_start[:, None] >= ends[None, :], axis=1), N_BUCKETS - 1)
    first = jnp.array([4 * g + i for g in range(N_GROUPS) for i, _ in GROUP_PAIRS], jnp.int32)
    second = jnp.array([4 * g + j for g in range(N_GROUPS) for _, j in GROUP_PAIRS], jnp.int32)
    return base, first[tile_bucket], second[tile_bucket], (ends[-1:] // TR).astype(jnp.int32)


def _rope_tables():
    t = jnp.arange(DEC_SEQ)
    row = (t // GRID_W).astype(F32)
    col = (t % GRID_W).astype(F32)
    n = HEAD_DIM // 4
    freqs = ROPE_THETA ** (-jnp.arange(n, dtype=F32) / n)
    ang = jnp.concatenate([row[:, None] * freqs, col[:, None] * freqs], axis=-1)
    cos = jnp.tile(jnp.repeat(jnp.cos(ang), 2, axis=-1), (1, LANES // HEAD_DIM))
    sin = jnp.tile(jnp.repeat(jnp.sin(ang), 2, axis=-1), (1, LANES // HEAD_DIM))
    even = (jnp.arange(LANES) % 2 == 0)[None, :]
    return cos, jnp.where(even, -sin, 0.0), jnp.where(even, 0.0, sin)


def _group_mean_matrix(width):
    g = jnp.arange(width) // HEAD_DIM
    return jnp.where(g[:, None] == g[None, :], 1.0 / HEAD_DIM, 0.0).astype(BF16)


def _neighbourhood_bias(rpb_l):
    cx = jnp.arange(GRID_W)
    col_start = jnp.clip(cx - NA_KW // 2, 0, GRID_W - NA_KW)
    kx = jnp.arange(GRID_W)
    inside_col = (kx[None, :] >= col_start[:, None]) & (kx[None, :] < col_start[:, None] + NA_KW)
    dx = jnp.clip(kx[None, :] - cx[:, None] + (NA_KW - 1), 0, 2 * NA_KW - 2)
    exact = lax.Precision.HIGHEST
    pick_dx = ((dx[:, :, None] == jnp.arange(2 * NA_KW - 1)) & inside_col[:, :, None]).astype(F32)
    n_groups = GRID_H // NBR_ROWS
    i = jnp.arange(NBR_KEY_ROWS)
    tabs = []
    for g in (0, 1, n_groups - 1):
        key_start = min(max(g * NBR_ROWS - NA_KH // 2, 0), GRID_H - NBR_KEY_ROWS)
        r = g * NBR_ROWS + jnp.arange(NBR_ROWS)
        row_start = jnp.clip(r - NA_KH // 2, 0, GRID_H - NA_KH)
        ky = key_start + i
        inside_row = (ky[None, :] >= row_start[:, None]) & (ky[None, :] < row_start[:, None] + NA_KH)
        dy = jnp.clip(ky[None, :] - r[:, None] + (NA_KH - 1), 0, 2 * NA_KH - 2)
        pick_dy = ((dy[:, :, None] == jnp.arange(2 * NA_KH - 1)) & inside_row[:, :, None]).astype(F32)
        by_row = jnp.einsum("hyd,jiy->hjid", rpb_l, pick_dy, precision=exact)
        vals = jnp.einsum("hjid,ckd->hjcik", by_row, pick_dx, precision=exact)
        ok = inside_row[:, None, :, None] & inside_col[None, :, None, :]
        tabs.append(jnp.where(ok[None], vals * LOG2E, NEG_BIAS)
                    .reshape(H_B, NBR_ROWS * GRID_W, NBR_KEY_ROWS * GRID_W))
    return jnp.stack(tabs, axis=0)


def _gqa_regroup(w, axis):
    shape = w.shape
    w = w.reshape(shape[:axis] + (KV_C, H_C // KV_C, HEAD_DIM) + shape[axis + 1:])
    return jnp.swapaxes(w, axis, axis + 1).reshape(shape)


def kernel(x_prompt, x_sample, c, cache_a_k, cache_a_v, cache_b_k, cache_b_v, cache_c_k, cache_c_v,
           c_ctx, w_mod, b_mod, w_in, w_gate, b_gate, lam, a_subln, rpb, c_qnorm, c_knorm,
           w_br, w_o, ln_g, ln_b, w_router, router_bias, moe_w1, moe_w3, moe_w2):
    alpha = (2.0 * DEPTH) ** 0.25
    cond = jnp.concatenate([c_ctx[None, :], c, jnp.zeros((N_COND - 1 - DEC_BATCH, D_MODEL), F32)], axis=0)
    mods = _modulation(cond, w_mod, b_mod).reshape(DEPTH, N_COND, 1, 6 * D_MODEL)

    rope_tabs = _rope_tables()
    g512 = _group_mean_matrix(512)
    g128 = _group_mean_matrix(128)
    wrt = w_router.T
    rb = router_bias.reshape(N_EXPERTS, 1)
    tok = jnp.arange(TM)
    tri = (tok[:, None] < tok[None, :]).astype(BF16)

    ck_a = cache_a_k.reshape(DEC_BATCH, DEPTH, PAST_LEN, 512)
    cv_a = cache_a_v.reshape(DEC_BATCH, DEPTH, PAST_LEN, 512)
    ck_b = cache_b_k.reshape(DEC_BATCH, DEPTH, PAST_LEN, 512)
    cv_b = cache_b_v.reshape(DEC_BATCH, DEPTH, PAST_LEN, 512)
    ck_c = cache_c_k.reshape(DEC_BATCH, DEPTH, PAST_LEN, 128)
    cv_c = cache_c_v.reshape(DEC_BATCH, DEPTH, PAST_LEN, 128)

    ctx_row = lambda i: 0
    lat_tiles = DEC_SEQ // TM
    lat_row = lambda i: 1 + i // lat_tiles

    xp = x_prompt.reshape(BATCH * SEQ, D_MODEL)
    xs = x_sample.reshape(DEC_BATCH * DEC_SEQ, D_MODEL)
    new_kv = [jnp.zeros((BATCH, DEPTH, SEQ, w), F32) for w in CACHE_WIDTHS]
    rows = jnp.zeros((SORTED_ROWS * TILE_ROWS, LANES), F32)
    for l in range(DEPTH):
        w_in_l = w_in[l]
        w_in_l = jnp.concatenate([w_in_l[:, :OFF_QC], _gqa_regroup(w_in_l[:, OFF_QC:OFF_KC], 1),
                                  w_in_l[:, OFF_KC:]], axis=1).astype(BF16)
        qn = jnp.tile(c_qnorm[l], 512 // HEAD_DIM)[None, :]
        kn = jnp.tile(c_knorm[l], 128 // HEAD_DIM)[None, :]
        wg = w_gate[l].astype(BF16)
        bg = b_gate[l][None, :]
        wbr = jnp.stack([w_br[l, 0], w_br[l, 1], _gqa_regroup(w_br[l, 2], 0)], axis=0).astype(BF16)
        wo = w_o[l].astype(BF16)
        w13 = jnp.concatenate([moe_w1[l], moe_w3[l]], axis=-1).astype(BF16)
        w2 = moe_w2[l].astype(BF16)
        lam_l = lam[l]
        subln_l = a_subln[l][None, :]
        bias_tab = _neighbourhood_bias(rpb[l])
        mod_l = mods[l]

        q, *new_kv = _in_proj(xp, mod_l, ctx_row, w_in_l, qn, kn, g512, g128, None, new_kv, l)
        oa, ob, oc = _ctx_attention(l, q, *new_kv, lam_l, subln_l)
        xp = _merge(alpha, xp, oa, ob, oc, mod_l, ctx_row, wg, bg, wbr, wo, ln_g[l, 0:1], ln_b[l, 0:1])

        q, k, v = _in_proj(xs, mod_l, lat_row, w_in_l, qn, kn, g512, g128, rope_tabs)
        oa = _lat_diff_attention(l, q, k, v, ck_a, cv_a, lam_l, subln_l)
        ob = _lat_nbr_attention(l, q, k, v, ck_b, cv_b, bias_tab)
        oc = _lat_gqa_attention(l, q, k, v, ck_c, cv_c)
        xs = _merge(alpha, xs, oa, ob, oc, mod_l, lat_row, wg, bg, wbr, wo, ln_g[l, 0:1], ln_b[l, 0:1])

        h2p, meta_p, counts = _route(xp, mod_l, ctx_row, wrt, rb, tri, jnp.zeros((BUCKET_ROWS, LANES), F32))
        h2s, meta_s, counts = _route(xs, mod_l, lat_row, wrt, rb, tri, counts)
        base, tile_e0, tile_e1, n_used = _bucket_layout(counts)
        rows = _scatter(base, meta_p[0], meta_p[1], h2p, rows)
        rows = _scatter(base, meta_s[0], meta_s[1], h2s, rows)
        y = _experts(tile_e0, tile_e1, n_used, rows, wrt, w13, w2)
        xp = _gather_norm(alpha, base, meta_p[0], meta_p[1], xp, mod_l, ctx_row, y, ln_g[l, 1:2], ln_b[l, 1:2])
        xs = _gather_norm(alpha, base, meta_s[0], meta_s[1], xs, mod_l, lat_row, y, ln_g[l, 1:2], ln_b[l, 1:2])

    def cache(idx, shape):
        return new_kv[idx].reshape(BATCH, DEPTH, SEQ, *shape)

    return (xp.reshape(BATCH, SEQ, D_MODEL), xs.reshape(DEC_BATCH, DEC_SEQ, D_MODEL),
            cache(0, (2, H_A, HEAD_DIM)), cache(1, (H_A, 2 * HEAD_DIM)),
            cache(2, (H_B, HEAD_DIM)), cache(3, (H_B, HEAD_DIM)),
            cache(4, (KV_C, HEAD_DIM)), cache(5, (KV_C, HEAD_DIM)))
```
